```python
import jax, jax.numpy as jnp
from jax import lax
import numpy as np

D_MODEL = 4096
BATCH = 2
SEQ = 4096
DEPTH = 4

MEM_LEN = 256
D_FF = 3 * D_MODEL // 2
NORM_EPS = 1e-6
LN_EPS = 1e-5
ROPE_THETA = 500000.0

MIX_WIDTH = D_MODEL
GLA_HEADS = 8
GLA_V_WIDTH = MIX_WIDTH // 2
GLA_K_WIDTH = GLA_V_WIDTH // 2
GLA_DK = GLA_K_WIDTH // GLA_HEADS
GLA_DV = GLA_V_WIDTH // GLA_HEADS
GLA_GATE_RANK = 16
GLA_TAU = 16.0
GLA_CHUNK = 64
SG_WIDTH = MIX_WIDTH - GLA_V_WIDTH
SG_GROUPS = 4
SG_GROUP_WIDTH = SG_WIDTH // SG_GROUPS
SG_CHUNK = 128
AB_SIZES = (GLA_K_WIDTH, GLA_K_WIDTH, GLA_V_WIDTH, GLA_V_WIDTH, GLA_GATE_RANK, SG_WIDTH, SG_WIDTH)
AB_IN = sum(AB_SIZES)

NSA_HEAD_DIM = 128
NSA_HEADS = D_MODEL // NSA_HEAD_DIM
NSA_KV_GROUPS = 4
NSA_HPG = NSA_HEADS // NSA_KV_GROUPS
NSA_KV_WIDTH = NSA_KV_GROUPS * NSA_HEAD_DIM
NSA_CMP_STRIDE = 16
NSA_CMP_LEN = 2 * NSA_CMP_STRIDE
NSA_SEL_BLOCK = 64
NSA_TOPK = 16
NSA_WINDOW = 512
NSA_Q_BLOCK = 64
NSA_FORCE_BONUS = 1e4
NSA_SIZES = (NSA_HEADS * NSA_HEAD_DIM,) + (NSA_KV_WIDTH,) * 6 + (3 * NSA_HEADS,)
NSA_IN = sum(NSA_SIZES)
NSA_SCALE = NSA_HEAD_DIM ** -0.5
ROPE_DIM = NSA_HEAD_DIM // 4

X_HEADS = 4
X_HEAD_DIM = 128
X_WIDTH = X_HEADS * X_HEAD_DIM
X_SCALE = X_HEAD_DIM ** -0.5

kernel_name = 'hybrid_gla_sgmlp_nsa_macaron_trunk'


def rms_norm(x, g):
    xf = x.astype(jnp.float32)
    y = xf * lax.rsqrt(jnp.mean(xf * xf, axis=-1, keepdims=True) + NORM_EPS)
    return (y * g.astype(jnp.float32)).astype(x.dtype)


def layer_norm(x, g, b):
    xf = x.astype(jnp.float32)
    mu = jnp.mean(xf, axis=-1, keepdims=True)
    xc = xf - mu
    y = xc * lax.rsqrt(jnp.mean(xc * xc, axis=-1, keepdims=True) + LN_EPS)
    return (y * g.astype(jnp.float32) + b.astype(jnp.float32)).astype(x.dtype)


def split_cols(z, sizes):
    out, start = [], 0
    for s in sizes:
        out.append(z[..., start:start + s])
        start += s
    return out


def swiglu(h, w_gate, w_up, w_down):
    return (jax.nn.silu(h @ w_gate) * (h @ w_up)) @ w_down


def masked_softmax(s, mask):
    s = jnp.where(mask, s.astype(jnp.float32), -jnp.inf)
    m = jnp.max(s, axis=-1, keepdims=True)
    m = jnp.where(jnp.isfinite(m), m, 0.0)
    p = jnp.where(mask, jnp.exp(s - m), 0.0)
    return p / jnp.maximum(jnp.sum(p, axis=-1, keepdims=True), 1e-30)


def rope_partial(x, pos):
    half = ROPE_DIM // 2
    inv_freq = jnp.power(ROPE_THETA, -jnp.arange(half, dtype=jnp.float32) / half)
    ang = pos.astype(jnp.float32)[:, None] * inv_freq[None, :]
    bshape = (1, x.shape[1]) + (1,) * (x.ndim - 3) + (half,)
    cos = jnp.cos(ang).reshape(bshape).astype(x.dtype)
    sin = jnp.sin(ang).reshape(bshape).astype(x.dtype)
    x1, x2 = x[..., :half], x[..., half:ROPE_DIM]
    return jnp.concatenate([x1 * cos - x2 * sin, x2 * cos + x1 * sin, x[..., ROPE_DIM:]], axis=-1)


def gla_chunked(q, k, v, log_f):
    B, T, H, dk = q.shape
    dv = v.shape[-1]
    C = GLA_CHUNK
    n = T // C
    def to_chunks(a):
        return a.reshape(B, n, C, H, a.shape[-1]).transpose(1, 0, 3, 2, 4).astype(jnp.float32)
    qc, kc, vc, gc = (to_chunks(a) for a in (q * (dk ** -0.5), k, v, log_f))
    causal = jnp.tril(jnp.ones((C, C), dtype=bool))[None, None, :, :, None]
    def step(S, inp):
        qn, kn, vn, gn = inp
        b = jnp.cumsum(gn, axis=2)
        o_inter = jnp.einsum('bhcd,bhde->bhce', qn * jnp.exp(b), S)
        diff = b[:, :, :, None, :] - b[:, :, None, :, :]
        decay = jnp.where(causal, jnp.exp(jnp.minimum(diff, 0.0)), 0.0)
        a = jnp.einsum('bhijd,bhjd->bhij', decay * qn[:, :, :, None, :], kn)
        o_intra = jnp.einsum('bhij,bhje->bhie', a, vn)
        b_last = b[:, :, -1:, :]
        S_new = jnp.exp(b_last[:, :, 0, :])[..., None] * S + jnp.einsum('bhjd,bhje->bhde', kn * jnp.exp(b_last - b), vn)
        return S_new, o_inter + o_intra
    S0 = jnp.zeros((B, H, dk, dv), jnp.float32)
    _, o = lax.scan(step, S0, (qc, kc, vc, gc))
    return o.transpose(1, 0, 3, 2, 4).reshape(B, T, H, dv).astype(v.dtype)


def mixer_gla_sg(h, w_in, w_gate_up, b_gate, gla_norm_g, sg_ln_g, sg_ln_b, sg_w_s, sg_b_s, w_out):
    B, T, _ = h.shape
    q, k, v, r, a, u, s = split_cols(h @ w_in, AB_SIZES)
    log_f = jax.nn.log_sigmoid((a @ w_gate_up + b_gate).astype(jnp.float32)) / GLA_TAU
    heads = lambda z, d: z.reshape(B, T, GLA_HEADS, d)
    o = gla_chunked(heads(q, GLA_DK), heads(k, GLA_DK), heads(v, GLA_DV), heads(log_f, GLA_DK))
    y_gla = (rms_norm(o, gla_norm_g) * jax.nn.silu(heads(r, GLA_DV))).reshape(B, T, GLA_V_WIDTH)
    u = jax.nn.gelu(u)
    s = layer_norm(jax.nn.gelu(s), sg_ln_g, sg_ln_b)
    n_chunks = T // SG_CHUNK
    s = s.reshape(B, n_chunks, SG_CHUNK, SG_GROUPS, SG_GROUP_WIDTH)
    w_s = jnp.tril(sg_w_s)
    s = jnp.einsum('gij,bnjgc->bnigc', w_s, s) + sg_b_s.T[None, None, :, :, None]
    y_sg = u * s.reshape(B, T, SG_WIDTH)
    return jnp.concatenate([y_gla, y_sg], axis=-1) @ w_out


def nsa_compress(x, pos_emb, w1, b1, w2, b2):
    B, T, G, hd = x.shape
    xs = x.reshape(B, T // NSA_CMP_STRIDE, NSA_CMP_STRIDE, G, hd)
    blocks = jnp.concatenate([xs[:, :-1], xs[:, 1:]], axis=2)
    blocks = blocks + pos_emb[None, None, :, None, :]
    flat = blocks.transpose(0, 1, 3, 2, 4).reshape(B, blocks.shape[1], G, NSA_CMP_LEN * hd)
    return jax.nn.silu(flat @ w1 + b1) @ w2 + b2


def mixer_nsa(h, w_in, cmp_pos, cmp_w1, cmp_b1, cmp_w2, cmp_b2, w_out):
    B, T, _ = h.shape
    G, P, hd, W, QB, SB = NSA_KV_GROUPS, NSA_HPG, NSA_HEAD_DIM, NSA_WINDOW, NSA_Q_BLOCK, NSA_SEL_BLOCK
    q, k_c, v_c, k_s, v_s, k_w, v_w, g_l = split_cols(h @ w_in, NSA_SIZES)
    kv = lambda z: z.reshape(B, T, G, hd)
    pos = jnp.arange(T)
    q = q.reshape(B, T, G, P, hd)
    q_rot = rope_partial(q, pos)
    k_s = rope_partial(kv(k_s), pos)
    k_w = rope_partial(kv(k_w), pos)
    v_s, v_w = kv(v_s), kv(v_w)
    gates = jax.nn.sigmoid(g_l).reshape(B, T, G, P, 3)
    k_cmp = nsa_compress(kv(k_c), cmp_pos[0], cmp_w1[0], cmp_b1[0], cmp_w2[0], cmp_b2[0])
    v_cmp = nsa_compress(kv(v_c), cmp_pos[1], cmp_w1[1], cmp_b1[1], cmp_w2[1], cmp_b2[1])
    n_cmp = k_cmp.shape[1]
    n_sel = T // SB
    top_n = min(NSA_TOPK, n_sel)
    cmp_start = jnp.arange(n_cmp) * NSA_CMP_STRIDE
    cmp_end = cmp_start + NSA_CMP_LEN - 1
    sel_start = jnp.arange(n_sel) * SB
    overlap = ((cmp_start[:, None] <= sel_start[None, :] + SB - 1)
               & (cmp_end[:, None] >= sel_start[None, :])).astype(jnp.float32)
    ks_blocks = k_s.reshape(B, n_sel, SB, G, hd).transpose(0, 3, 1, 2, 4)
    vs_blocks = v_s.reshape(B, n_sel, SB, G, hd).transpose(0, 3, 1, 2, 4)
    pad = ((0, 0), (W, 0), (0, 0), (0, 0))
    kw_pad, vw_pad = jnp.pad(k_w, pad), jnp.pad(v_w, pad)
    bi = jnp.arange(B)[:, None, None, None]
    gi = jnp.arange(G)[None, :, None, None]
    blk = jnp.arange(n_sel)
    n_qb = T // QB
    to_blocks = lambda z: jnp.moveaxis(z.reshape((B, n_qb, QB) + z.shape[2:]), 1, 0)

    def attend_block(args):
        i, qb, qrb, gb = args
        t = i * QB + jnp.arange(QB)
        s_c = jnp.einsum('bqgpd,bkgd->bgpqk', qb, k_cmp) * NSA_SCALE
        p_c = masked_softmax(s_c, cmp_end[None, :] <= t[:, None])
        o_c = jnp.einsum('bgpqk,bkgd->bqgpd', p_c.astype(v_cmp.dtype), v_cmp)
        imp = jnp.einsum('bgpqk,ks->bgqs', p_c, overlap)
        cur = (t // SB)[:, None]
        forced = (blk == 0) | (blk == cur) | (blk == cur - 1)
        score = jnp.where(blk <= cur, imp + jnp.where(forced, NSA_FORCE_BONUS, 0.0), -jnp.inf)
        _, idx = lax.top_k(score, top_n)
        k_sel = ks_blocks[bi, gi, idx].reshape(B, G, QB, top_n * SB, hd)
        v_sel = vs_blocks[bi, gi, idx].reshape(B, G, QB, top_n * SB, hd)
        key_pos = (idx[..., None] * SB + jnp.arange(SB)).reshape(B, G, 1, QB, top_n * SB)
        s_s = jnp.einsum('bqgpd,bgqkd->bgpqk', qrb, k_sel) * NSA_SCALE
        p_s = masked_softmax(s_s, key_pos <= t[:, None])
        o_s = jnp.einsum('bgpqk,bgqkd->bqgpd', p_s.astype(v_sel.dtype), v_sel)
        start = i * QB
        k_win = lax.dynamic_slice_in_dim(kw_pad, start, QB + W, axis=1)
        v_win = lax.dynamic_slice_in_dim(vw_pad, start, QB + W, axis=1)
        wpos = start - W + jnp.arange(QB + W)
        delta = t[:, None] - wpos[None, :]
        m_w = (delta >= 0) & (delta < W) & (wpos >= 0)[None, :]
        s_w = jnp.einsum('bqgpd,bkgd->bgpqk', qrb, k_win) * NSA_SCALE
        p_w = masked_softmax(s_w, m_w)
        o_w = jnp.einsum('bgpqk,bkgd->bqgpd', p_w.astype(v_win.dtype), v_win)
        return gb[..., 0:1] * o_c + gb[..., 1:2] * o_s + gb[..., 2:3] * o_w

    o = lax.map(attend_block, (jnp.arange(n_qb), to_blocks(q), to_blocks(q_rot), to_blocks(gates)))
    o = jnp.moveaxis(o, 0, 1).reshape(B, T, NSA_HEADS * hd)
    return o @ w_out


def cross_attention(h, mem_n, w_q, w_kv, w_o):
    B, T, _ = h.shape
    M = mem_n.shape[1]
    q = (h @ w_q).reshape(B, T, X_HEADS, X_HEAD_DIM)
    kv = (mem_n @ w_kv).reshape(B, M, 2, X_HEADS, X_HEAD_DIM)
    k, v = kv[:, :, 0], kv[:, :, 1]
    s = jnp.einsum('bthd,bmhd->bhtm', q, k).astype(jnp.float32) * X_SCALE
    p = jax.nn.softmax(s, axis=-1).astype(v.dtype)
    o = jnp.einsum('bhtm,bmhd->bthd', p, v).reshape(B, T, X_WIDTH)
    return o @ w_o


def setup_inputs(seed: int = 0) -> dict:
    keys = iter(jax.random.split(jax.random.key(seed), 48))
    f32 = jnp.float32
    def dense(shape, fan_in, scale=1.0):
        return jax.random.normal(next(keys), shape, f32) * (scale * fan_in ** -0.5)
    def gain(shape):
        return 1.0 + 0.02 * jax.random.normal(next(keys), shape, f32)
    def small(shape, s=0.01):
        return s * jax.random.normal(next(keys), shape, f32)
    L, LE, LO, D = DEPTH, (DEPTH + 1) // 2, DEPTH // 2, D_MODEL
    hd, CL = NSA_HEAD_DIM, NSA_CMP_LEN
    return {
        'x': jax.random.normal(next(keys), (BATCH, SEQ, D), f32),
        'mem': jax.random.normal(next(keys), (BATCH, MEM_LEN, D), f32),
        'mem_norm_g': gain((D,)),
        'ffn1_norm_g': gain((L, D)),
        'ffn1_w_gate': dense((L, D, D_FF), D),
        'ffn1_w_up': dense((L, D, D_FF), D),
        'ffn1_w_down': dense((L, D_FF, D), D_FF),
        'mix_norm_g': gain((L, D)),
        'ab_w_in': dense((LE, D, AB_IN), D),
        'gla_w_gate_up': dense((LE, GLA_GATE_RANK, GLA_K_WIDTH), GLA_GATE_RANK),
        'gla_b_gate': small((LE, GLA_K_WIDTH), 0.1),
        'gla_norm_g': gain((LE, GLA_DV)),
        'sg_ln_g': gain((LE, SG_WIDTH)),
        'sg_ln_b': small((LE, SG_WIDTH)),
        'sg_w_s': dense((LE, SG_GROUPS, SG_CHUNK, SG_CHUNK), SG_CHUNK, 0.5),
        'sg_b_s': 1.0 + small((LE, SG_GROUPS, SG_CHUNK)),
        'ab_w_out': dense((LE, MIX_WIDTH, D), MIX_WIDTH),
        'nsa_w_in': dense((LO, D, NSA_IN), D),
        'nsa_cmp_pos': dense((LO, 2, CL, hd), hd),
        'nsa_cmp_w1': dense((LO, 2, CL * hd, hd), CL * hd),
        'nsa_cmp_b1': small((LO, 2, hd)),
        'nsa_cmp_w2': dense((LO, 2, hd, hd), hd),
        'nsa_cmp_b2': small((LO, 2, hd)),
        'nsa_w_out': dense((LO, NSA_HEADS * hd, D), NSA_HEADS * hd),
        'cross_norm_g': gain((L, D)),
        'cross_w_q': dense((L, D, X_WIDTH), D),
        'cross_w_kv': dense((L, D, 2 * X_WIDTH), D),
        'cross_w_o': dense((L, X_WIDTH, D), X_WIDTH),
        'ffn2_norm_g': gain((L, D)),
        'ffn2_w_gate': dense((L, D, D_FF), D),
        'ffn2_w_up': dense((L, D, D_FF), D),
        'ffn2_w_down': dense((L, D_FF, D), D_FF),
        'final_norm_g': gain((D,)),
    }


def reference(x, mem, mem_norm_g,
              ffn1_norm_g, ffn1_w_gate, ffn1_w_up, ffn1_w_down,
              mix_norm_g,
              ab_w_in, gla_w_gate_up, gla_b_gate, gla_norm_g, sg_ln_g, sg_ln_b, sg_w_s, sg_b_s, ab_w_out,
              nsa_w_in, nsa_cmp_pos, nsa_cmp_w1, nsa_cmp_b1, nsa_cmp_w2, nsa_cmp_b2, nsa_w_out,
              cross_norm_g, cross_w_q, cross_w_kv, cross_w_o,
              ffn2_norm_g, ffn2_w_gate, ffn2_w_up, ffn2_w_down,
              final_norm_g):
    mem_n = rms_norm(mem, mem_norm_g)
    for i in range(DEPTH):
        x = x + 0.5 * swiglu(rms_norm(x, ffn1_norm_g[i]), ffn1_w_gate[i], ffn1_w_up[i], ffn1_w_down[i])
        h = rms_norm(x, mix_norm_g[i])
        j = i // 2
        if i % 2 == 0:
            x = x + mixer_gla_sg(h, ab_w_in[j], gla_w_gate_up[j], gla_b_gate[j], gla_norm_g[j],
                                 sg_ln_g[j], sg_ln_b[j], sg_w_s[j], sg_b_s[j], ab_w_out[j])
        else:
            x = x + mixer_nsa(h, nsa_w_in[j], nsa_cmp_pos[j], nsa_cmp_w1[j], nsa_cmp_b1[j],
                              nsa_cmp_w2[j], nsa_cmp_b2[j], nsa_w_out[j])
        x = x + cross_attention(rms_norm(x, cross_norm_g[i]), mem_n, cross_w_q[i], cross_w_kv[i], cross_w_o[i])
        x = x + 0.5 * swiglu(rms_norm(x, ffn2_norm_g[i]), ffn2_w_gate[i], ffn2_w_up[i], ffn2_w_down[i])
    return rms_norm(x, final_norm_g)
```

```python
import functools

import jax
import jax.numpy as jnp
from jax import lax
from jax.experimental import pallas as pl
from jax.experimental.pallas import tpu as pltpu

F32 = jnp.float32
BF16 = jnp.bfloat16

NORM_EPS = 1e-6
LN_EPS = 1e-5
ROPE_THETA = 500000.0

GLA_HEADS = 8
GLA_DK = 128
GLA_DV = 256
GLA_K_WIDTH = GLA_HEADS * GLA_DK
GLA_V_WIDTH = GLA_HEADS * GLA_DV
GLA_GATE_RANK = 16
GLA_INV_TAU = 1.0 / 16.0
GLA_CHUNK = 64
GLA_SUB = 16
SG_WIDTH = 2048
SG_GROUPS = 4
SG_GROUP_WIDTH = SG_WIDTH // SG_GROUPS
SG_CHUNK = 128

NSA_HEAD_DIM = 128
NSA_KV_GROUPS = 4
NSA_HPG = 8
NSA_Q_WIDTH = NSA_KV_GROUPS * NSA_HPG * NSA_HEAD_DIM
NSA_KV_WIDTH = NSA_KV_GROUPS * NSA_HEAD_DIM
NSA_CMP_STRIDE = 16
NSA_CMP_LEN = 32
NSA_SEL_BLOCK = 64
NSA_TOPK = 16
NSA_WINDOW = 512
NSA_FORCE_BONUS = 1e4
NSA_SCALE = NSA_HEAD_DIM ** -0.5
ROPE_DIM = NSA_HEAD_DIM // 4
ROPE_HALF = ROPE_DIM // 2

X_HEADS = 4
X_HEAD_DIM = 128
X_WIDTH = X_HEADS * X_HEAD_DIM
X_SCALE = X_HEAD_DIM ** -0.5

LANES = 128
MASK_NEG = -1e30
VMEM_LIMIT_BYTES = 56 * 1024 * 1024


def _params(*sem):
    return pltpu.CompilerParams(dimension_semantics=sem, vmem_limit_bytes=VMEM_LIMIT_BYTES)


def _dot(a, b):
    return jnp.dot(a, b, preferred_element_type=F32)


def _dot_nt(a, b):
    return lax.dot_general(a, b, (((1,), (1,)), ((), ())), preferred_element_type=F32)


def _dot_tn(a, b):
    return lax.dot_general(a, b, (((0,), (0,)), ((), ())), preferred_element_type=F32)


def _sigmoid(x):
    return 1.0 / (1.0 + jnp.exp(-x))


def _silu(x):
    return x * _sigmoid(x)


def _gelu_tanh(x):
    c = 0.7978845608028654
    return x * (0.5 * (1.0 + jnp.tanh(c * (x + 0.044715 * (x * x * x)))))


def _tile(n, pref):
    if n <= pref:
        return n
    t = pref
    while n % t:
        t //= 2
    return t


def _rmsnorm_kernel(x_ref, g_ref, o_ref):
    x = x_ref[...]
    ms = jnp.mean(x * x, axis=-1, keepdims=True)
    o_ref[...] = (x * lax.rsqrt(ms + NORM_EPS) * g_ref[...]).astype(o_ref.dtype)


def rmsnorm(x, g, out_dtype=BF16):
    m, d = x.shape
    tm = _tile(m, 256)
    return pl.pallas_call(
        _rmsnorm_kernel,
        grid=(m // tm,),
        in_specs=[pl.BlockSpec((tm, d), lambda i: (i, 0)), pl.BlockSpec((1, d), lambda i: (0, 0))],
        out_specs=pl.BlockSpec((tm, d), lambda i: (i, 0)),
        out_shape=jax.ShapeDtypeStruct((m, d), out_dtype),
        compiler_params=_params("parallel"),
        name="rmsnorm",
    )(x, g.reshape(1, d))


def _mm_kernel(a_ref, w_ref, o_ref):
    o_ref[...] = _dot(a_ref[...], w_ref[...]).astype(o_ref.dtype)


def _mm_res_kernel(a_ref, w_ref, r_ref, o_ref, *, scale):
    o_ref[...] = r_ref[...] + scale * _dot(a_ref[...], w_ref[...])


def matmul(a, w, *, out_dtype=F32, tm=1024, tn=1024, residual=None, scale=1.0, name="matmul"):
    m, k = a.shape
    n = w.shape[1]
    tm = _tile(m, tm)
    tn = _tile(n, tn)
    in_specs = [pl.BlockSpec((tm, k), lambda i, j: (i, 0)), pl.BlockSpec((k, tn), lambda i, j: (0, j))]
    args = [a, w]
    if residual is None:
        body = _mm_kernel
    else:
        body = functools.partial(_mm_res_kernel, scale=scale)
        in_specs.append(pl.BlockSpec((tm, tn), lambda i, j: (i, j)))
        args.append(residual)
        out_dtype = F32
    return pl.pallas_call(
        body,
        grid=(m // tm, n // tn),
        in_specs=in_specs,
        out_specs=pl.BlockSpec((tm, tn), lambda i, j: (i, j)),
        out_shape=jax.ShapeDtypeStruct((m, n), out_dtype),
        compiler_params=_params("parallel", "arbitrary"),
        name=name,
    )(*args)


def _ffn_up_kernel(h_ref, wg_ref, wu_ref, o_ref):
    h = h_ref[...]
    g = _dot(h, wg_ref[...])
    u = _dot(h, wu_ref[...])
    o_ref[...] = (_silu(g) * u).astype(o_ref.dtype)


def ffn_up(h, wg, wu, *, tm=1024, tn=512):
    m, k = h.shape
    n = wg.shape[1]
    tm = _tile(m, tm)
    tn = _tile(n, tn)
    return pl.pallas_call(
        _ffn_up_kernel,
        grid=(m // tm, n // tn),
        in_specs=[
            pl.BlockSpec((tm, k), lambda i, j: (i, 0)),
            pl.BlockSpec((k, tn), lambda i, j: (0, j)),
            pl.BlockSpec((k, tn), lambda i, j: (0, j)),
        ],
        out_specs=pl.BlockSpec((tm, tn), lambda i, j: (i, j)),
        out_shape=jax.ShapeDtypeStruct((m, n), BF16),
        compiler_params=_params("parallel", "arbitrary"),
        name="ffn_up",
    )(h, wg, wu)


def swiglu_half_step(x, norm_g, w_gate, w_up, w_down):
    h = rmsnorm(x, norm_g)
    act = ffn_up(h, w_gate.astype(BF16), w_up.astype(BF16))
    return matmul(act, w_down.astype(BF16), residual=x, scale=0.5, tn=512, name="ffn_down")


def _gla_kernel(q_ref, k_ref, v_ref, r_ref, a_ref, wgu_ref, bg_ref, ng_ref, o_ref, st_ref, *, rt):
    C, SB = GLA_CHUNK, GLA_SUB

    @pl.when(pl.program_id(2) == 0)
    def _():
        st_ref[...] = jnp.zeros_like(st_ref)

    zg = _dot(a_ref[...].astype(BF16), wgu_ref[...]) + bg_ref[...]
    lf = (jnp.minimum(zg, 0.0) - jnp.log(1.0 + jnp.exp(-jnp.abs(zg)))) * GLA_INV_TAU

    row = lax.broadcasted_iota(jnp.int32, (rt, GLA_DK), 0)
    rc = row & (C - 1)
    b = lf
    for s in (1, 2, 4, 8, 16, 32):
        b = b + jnp.where(rc >= s, pltpu.roll(b, s, 0), 0.0)

    q = q_ref[...] * (GLA_DK ** -0.5)
    k = k_ref[...]
    v = v_ref[...].astype(BF16)

    nb = rt // SB
    b3 = b.reshape(nb, SB, GLA_DK)
    q3 = q.reshape(nb, SB, GLA_DK)
    k3 = k.reshape(nb, SB, GLA_DK)
    rowc = lax.broadcasted_iota(jnp.int32, (rt, C), 0)
    lane = lax.broadcasted_iota(jnp.int32, (rt, C), 1)
    sub_row = rowc & (SB - 1)
    sub_base = (rowc & (C - 1)) - sub_row
    a_diag = jnp.zeros((rt, C), F32)
    for j in range(SB):
        bj = jnp.broadcast_to(b3[:, j:j + 1, :], (nb, SB, GLA_DK))
        kj = jnp.broadcast_to(k3[:, j:j + 1, :], (nb, SB, GLA_DK))
        pj = (q3 * kj * jnp.exp(jnp.minimum(b3 - bj, 0.0))).reshape(rt, GLA_DK)
        rs = jnp.sum(pj, axis=-1, keepdims=True)
        a_diag = jnp.where((lane == sub_base + j) & (sub_row >= j), rs, a_diag)

    lane_c = lax.broadcasted_iota(jnp.int32, (SB, C), 1)
    outs = []
    for c in range(rt // C):
        sl = slice(c * C, (c + 1) * C)
        bc, qc, kc, vc = b[sl], q[sl], k[sl], v[sl]
        rows = [jnp.zeros((SB, C), F32)]
        for i in range(1, C // SB):
            b0 = bc[i * SB:i * SB + 1]
            qi = qc[i * SB:(i + 1) * SB] * jnp.exp(bc[i * SB:(i + 1) * SB] - b0)
            ki = kc * jnp.exp(jnp.minimum(b0 - bc, 0.0))
            ai = _dot_nt(qi.astype(BF16), ki.astype(BF16))
            rows.append(jnp.where(lane_c < i * SB, ai, 0.0))
        a_c = a_diag[sl] + jnp.concatenate(rows, axis=0)
        o_intra = _dot(a_c.astype(BF16), vc)
        st = st_ref[...]
        o_inter = _dot_nt((qc * jnp.exp(bc)).astype(BF16), st.astype(BF16))
        bl = bc[C - 1:C]
        kd = kc * jnp.exp(bl - bc)
        st_ref[...] = st * jnp.exp(bl) + _dot_tn(vc, kd.astype(BF16))
        outs.append(o_inter + o_intra)
    o = jnp.concatenate(outs, axis=0)

    ms = jnp.mean(o * o, axis=-1, keepdims=True)
    y = o * lax.rsqrt(ms + NORM_EPS) * ng_ref[...]
    o_ref[...] = (y * _silu(r_ref[...])).astype(o_ref.dtype)


def gla_mixer(z, w_gate_up, b_gate, norm_g, batch, seq, *, rt=512):
    m = z.shape[0]
    rt = _tile(seq, rt)
    nt = seq // rt
    kb = GLA_K_WIDTH // GLA_DK
    vb = 2 * GLA_K_WIDTH // GLA_DV
    rb = vb + GLA_V_WIDTH // GLA_DV
    ab = (2 * GLA_K_WIDTH + 2 * GLA_V_WIDTH + 2 * SG_WIDTH) // LANES
    wgu = jnp.zeros((LANES, GLA_K_WIDTH), BF16).at[:GLA_GATE_RANK].set(w_gate_up.astype(BF16))
    row = lambda b, h, t: b * nt + t
    return pl.pallas_call(
        functools.partial(_gla_kernel, rt=rt),
        grid=(batch, GLA_HEADS, nt),
        in_specs=[
            pl.BlockSpec((rt, GLA_DK), lambda b, h, t: (row(b, h, t), h)),
            pl.BlockSpec((rt, GLA_DK), lambda b, h, t: (row(b, h, t), kb + h)),
            pl.BlockSpec((rt, GLA_DV), lambda b, h, t: (row(b, h, t), vb + h)),
            pl.BlockSpec((rt, GLA_DV), lambda b, h, t: (row(b, h, t), rb + h)),
            pl.BlockSpec((rt, LANES), lambda b, h, t: (row(b, h, t), ab)),
            pl.BlockSpec((LANES, GLA_DK), lambda b, h, t: (0, h)),
            pl.BlockSpec((1, GLA_DK), lambda b, h, t: (0, h)),
            pl.BlockSpec((1, GLA_DV), lambda b, h, t: (0, 0)),
        ],
        out_specs=pl.BlockSpec((rt, GLA_DV), lambda b, h, t: (row(b, h, t), h)),
        out_shape=jax.ShapeDtypeStruct((m, GLA_V_WIDTH), BF16),
        scratch_shapes=[pltpu.VMEM((GLA_DV, GLA_DK), F32)],
        compiler_params=_params("parallel", "parallel", "arbitrary"),
        name="gla",
    )(z, z, z, z, z, wgu, b_gate.reshape(1, GLA_K_WIDTH), norm_g.reshape(1, GLA_DV))


def _sg_kernel(u_ref, s_ref, lng_ref, lnb_ref, ws_ref, bs_ref, o_ref, *, rt):
    s = _gelu_tanh(s_ref[...])
    mu = jnp.mean(s, axis=-1, keepdims=True)
    sc = s - mu
    var = jnp.mean(sc * sc, axis=-1, keepdims=True)
    sn = (sc * lax.rsqrt(var + LN_EPS) * lng_ref[...] + lnb_ref[...]).astype(BF16)
    u = _gelu_tanh(u_ref[...])
    ri = lax.broadcasted_iota(jnp.int32, (SG_CHUNK, SG_CHUNK), 0)
    ci = lax.broadcasted_iota(jnp.int32, (SG_CHUNK, SG_CHUNK), 1)
    for g in range(SG_GROUPS):
        w = jnp.where(ci <= ri, ws_ref[g], 0.0).astype(BF16)
        cols = slice(g * SG_GROUP_WIDTH, (g + 1) * SG_GROUP_WIDTH)
        for c in range(rt // SG_CHUNK):
            rows = slice(c * SG_CHUNK, (c + 1) * SG_CHUNK)
            mixed = _dot(w, sn[rows, cols]) + bs_ref[g]
            o_ref[rows, cols] = (u[rows, cols] * mixed).astype(o_ref.dtype)


def sg_mixer(z, ln_g, ln_b, w_s, b_s, *, rt=512):
    m = z.shape[0]
    rt = _tile(m, rt)
    ub = (2 * GLA_K_WIDTH + 2 * GLA_V_WIDTH) // SG_WIDTH
    return pl.pallas_call(
        functools.partial(_sg_kernel, rt=rt),
        grid=(m // rt,),
        in_specs=[
            pl.BlockSpec((rt, SG_WIDTH), lambda i: (i, ub)),
            pl.BlockSpec((rt, SG_WIDTH), lambda i: (i, ub + 1)),
            pl.BlockSpec((1, SG_WIDTH), lambda i: (0, 0)),
            pl.BlockSpec((1, SG_WIDTH), lambda i: (0, 0)),
            pl.BlockSpec((SG_GROUPS, SG_CHUNK, SG_CHUNK), lambda i: (0, 0, 0)),
            pl.BlockSpec((SG_GROUPS, SG_CHUNK, 1), lambda i: (0, 0, 0)),
        ],
        out_specs=pl.BlockSpec((rt, SG_WIDTH), lambda i: (i, 0)),
        out_shape=jax.ShapeDtypeStruct((m, SG_WIDTH), BF16),
        compiler_params=_params("parallel"),
        name="spatial_gate",
    )(z, z, ln_g.reshape(1, SG_WIDTH), ln_b.reshape(1, SG_WIDTH), w_s, b_s.reshape(SG_GROUPS, SG_CHUNK, 1))


def mixer_gla_sg(x, h, batch, seq, w_in, w_gate_up, b_gate, gla_norm_g, sg_ln_g, sg_ln_b, sg_w_s, sg_b_s, w_out):
    d = w_in.shape[0]
    a0 = 2 * GLA_K_WIDTH + 2 * GLA_V_WIDTH
    w = jnp.concatenate(
        [w_in[:, :a0], w_in[:, a0 + GLA_GATE_RANK:], w_in[:, a0:a0 + GLA_GATE_RANK],
         jnp.zeros((d, LANES - GLA_GATE_RANK), w_in.dtype)], axis=1).astype(BF16)
    z = matmul(h, w, tn=1152, name="ab_in_proj")
    y_gla = gla_mixer(z, w_gate_up, b_gate, gla_norm_g, batch, seq)
    y_sg = sg_mixer(z, sg_ln_g, sg_ln_b, sg_w_s, sg_b_s)
    y = jnp.concatenate([y_gla, y_sg], axis=1)
    return matmul(y, w_out.astype(BF16), residual=x, name="ab_out_proj")


def _cross_kernel(q_ref, k_ref, v_ref, o_ref):
    for hd in range(X_HEADS):
        cols = slice(hd * X_HEAD_DIM, (hd + 1) * X_HEAD_DIM)
        s = _dot_nt(q_ref[:, cols], k_ref[:, cols]) * X_SCALE
        p = jnp.exp(s - jnp.max(s, axis=-1, keepdims=True))
        p = p / jnp.sum(p, axis=-1, keepdims=True)
        o_ref[:, cols] = _dot(p.astype(BF16), v_ref[:, cols]).astype(o_ref.dtype)


def cross_attention(x, h, kv, batch, seq, w_q, w_o, *, tq=1024):
    m = h.shape[0]
    mem = kv.shape[0] // batch
    tq = _tile(seq, tq)
    nt = seq // tq
    q = matmul(h, w_q.astype(BF16), out_dtype=BF16, name="cross_q_proj")
    o = pl.pallas_call(
        _cross_kernel,
        grid=(batch, nt),
        in_specs=[
            pl.BlockSpec((tq, X_WIDTH), lambda b, t: (b * nt + t, 0)),
            pl.BlockSpec((mem, X_WIDTH), lambda b, t: (b, 0)),
            pl.BlockSpec((mem, X_WIDTH), lambda b, t: (b, 1)),
        ],
        out_specs=pl.BlockSpec((tq, X_WIDTH), lambda b, t: (b * nt + t, 0)),
        out_shape=jax.ShapeDtypeStruct((m, X_WIDTH), BF16),
        compiler_params=_params("parallel", "parallel"),
        name="cross_attn",
    )(q, kv, kv)
    return matmul(o, w_o.astype(BF16), residual=x, name="cross_o_proj")


def _rope_tables(seq):
    inv_freq = jnp.power(ROPE_THETA, -jnp.arange(ROPE_HALF, dtype=F32) / ROPE_HALF)
    ang = jnp.arange(seq).astype(F32)[:, None] * inv_freq[None, :]
    cos, sin = jnp.cos(ang), jnp.sin(ang)
    rest = NSA_HEAD_DIM - ROPE_DIM
    c = jnp.concatenate([cos, cos, jnp.ones((seq, rest), F32)], axis=1)
    s_up = jnp.concatenate([jnp.zeros_like(sin), sin, jnp.zeros((seq, rest), F32)], axis=1)
    s_dn = jnp.concatenate([-sin, jnp.zeros_like(sin), jnp.zeros((seq, rest), F32)], axis=1)
    return c, s_up, s_dn


def _rope(x, c, s_up, s_dn, heads):
    width = heads * NSA_HEAD_DIM
    rep = lambda t: jnp.concatenate([t] * heads, axis=1)
    return (x * rep(c) + pltpu.roll(x, ROPE_HALF, 1) * rep(s_up)
            + pltpu.roll(x, width - ROPE_HALF, 1) * rep(s_dn))


def _nsa_kv_kernel(ks_ref, vs_ref, kw_ref, vw_ref, c_ref, su_ref, sd_ref, oks_ref, ovs_ref, okw_ref, ovw_ref):
    c, su, sd = c_ref[...], su_ref[...], sd_ref[...]
    oks_ref[...] = _rope(ks_ref[...], c, su, sd, NSA_KV_GROUPS).astype(BF16)
    okw_ref[...] = _rope(kw_ref[...], c, su, sd, NSA_KV_GROUPS).astype(BF16)
    ovs_ref[...] = vs_ref[...].astype(BF16)
    ovw_ref[...] = vw_ref[...].astype(BF16)


def nsa_kv_prep(z, tables, batch, seq, *, rt=1024):
    m = z.shape[0]
    rt = _tile(seq, rt)
    nt = seq // rt
    cb = NSA_Q_WIDTH // NSA_KV_WIDTH + 2
    zspec = lambda j: pl.BlockSpec((rt, NSA_KV_WIDTH), lambda b, t: (b * nt + t, cb + j))
    tspec = pl.BlockSpec((rt, NSA_HEAD_DIM), lambda b, t: (t, 0))
    ospec = pl.BlockSpec((rt, NSA_KV_WIDTH), lambda b, t: (b * nt + t, 0))
    osds = jax.ShapeDtypeStruct((m, NSA_KV_WIDTH), BF16)
    return pl.pallas_call(
        _nsa_kv_kernel,
        grid=(batch, nt),
        in_specs=[zspec(0), zspec(1), zspec(2), zspec(3), tspec, tspec, tspec],
        out_specs=[ospec] * 4,
        out_shape=[osds] * 4,
        compiler_params=_params("parallel", "parallel"),
        name="nsa_kv_prep",
    )(z, z, z, z, *tables)


def _nsa_cmp_kernel(x_ref, pos_ref, w1_ref, b1_ref, w2_ref, b2_ref, o_ref, *, nblk):
    half = NSA_CMP_STRIDE * NSA_HEAD_DIM
    pos_bias = _dot(pos_ref[...], w1_ref[...])[0:1] + b1_ref[...]
    row = lax.broadcasted_iota(jnp.int32, (nblk, NSA_HEAD_DIM), 0)
    for g in range(NSA_KV_GROUPS):
        xg = jnp.concatenate(
            [x_ref[:, l * NSA_KV_WIDTH + g * NSA_HEAD_DIM:l * NSA_KV_WIDTH + (g + 1) * NSA_HEAD_DIM]
             for l in range(NSA_CMP_STRIDE)], axis=1).astype(BF16)
        first = _dot(xg, w1_ref[:half])
        second = _dot(xg, w1_ref[half:])
        pre = first + pltpu.roll(second, nblk - 1, 0) + pos_bias
        out = _dot(_silu(pre).astype(BF16), w2_ref[...]) + b2_ref[...]
        o_ref[g] = jnp.where(row < nblk - 1, out, 0.0).astype(o_ref.dtype)


def nsa_compress(z, batch, seq, cmp_pos, cmp_w1, cmp_b1, cmp_w2, cmp_b2):
    nblk = seq // NSA_CMP_STRIDE
    c0 = NSA_Q_WIDTH
    xr = jnp.stack([z[:, c0:c0 + NSA_KV_WIDTH], z[:, c0 + NSA_KV_WIDTH:c0 + 2 * NSA_KV_WIDTH]])
    xr = xr.reshape(2, batch, nblk, NSA_CMP_STRIDE * NSA_KV_WIDTH)
    flat = NSA_CMP_LEN * NSA_HEAD_DIM
    pos = jnp.zeros((2, 8, flat), BF16).at[:, 0].set(cmp_pos.reshape(2, flat).astype(BF16))
    return pl.pallas_call(
        functools.partial(_nsa_cmp_kernel, nblk=nblk),
        grid=(2, batch),
        in_specs=[
            pl.BlockSpec((None, None, nblk, NSA_CMP_STRIDE * NSA_KV_WIDTH), lambda s, b: (s, b, 0, 0)),
            pl.BlockSpec((None, 8, flat), lambda s, b: (s, 0, 0)),
            pl.BlockSpec((None, flat, NSA_HEAD_DIM), lambda s, b: (s, 0, 0)),
            pl.BlockSpec((None, 1, NSA_HEAD_DIM), lambda s, b: (s, 0, 0)),
            pl.BlockSpec((None, NSA_HEAD_DIM, NSA_HEAD_DIM), lambda s, b: (s, 0, 0)),
            pl.BlockSpec((None, 1, NSA_HEAD_DIM), lambda s, b: (s, 0, 0)),
        ],
        out_specs=pl.BlockSpec((None, None, NSA_KV_GROUPS, nblk, NSA_HEAD_DIM), lambda s, b: (s, b, 0, 0, 0)),
        out_shape=jax.ShapeDtypeStruct((2, batch, NSA_KV_GROUPS, nblk, NSA_HEAD_DIM), BF16),
        compiler_params=_params("parallel", "parallel"),
        name="nsa_compress",
    )(xr, pos, cmp_w1.astype(BF16), cmp_b1.reshape(2, 1, NSA_HEAD_DIM), cmp_w2.astype(BF16),
      cmp_b2.reshape(2, 1, NSA_HEAD_DIM))


def _masked_softmax_rows(s, valid):
    sm = jnp.where(valid, s, MASK_NEG)
    mx = jnp.max(sm, axis=-1, keepdims=True)
    p = jnp.where(valid, jnp.exp(sm - mx), 0.0)
    return p / jnp.maximum(jnp.sum(p, axis=-1, keepdims=True), 1e-30)


def _nsa_attn_kernel(q_ref, gate_ref, c_ref, su_ref, sd_ref, kc_ref, vc_ref, ks_ref, vs_ref, kw_ref, vw_ref,
                     ovl_ref, blk_ref, o_ref, bias_ref, m_ref, l_ref, acc_ref, *, tq, tk, seq, top_n):
    P, HD, SB, W = NSA_HPG, NSA_HEAD_DIM, NSA_SEL_BLOCK, NSA_WINDOW
    R = P * tq
    nsel = seq // SB
    ncmp = seq // NSA_CMP_STRIDE
    t0 = pl.program_id(2) * tq

    stack = lambda x: jnp.concatenate([x[:, p * HD:(p + 1) * HD] for p in range(P)], axis=0)
    qf = q_ref[...]
    q_plain = stack(qf * NSA_SCALE).astype(BF16)
    q_rot = stack(_rope(qf, c_ref[...], su_ref[...], sd_ref[...], P) * NSA_SCALE).astype(BF16)

    s_c = _dot_nt(q_plain, kc_ref[...]).reshape(P, tq, ncmp)
    t_c = t0 + lax.broadcasted_iota(jnp.int32, (tq, ncmp), 0)
    end_c = lax.broadcasted_iota(jnp.int32, (tq, ncmp), 1) * NSA_CMP_STRIDE + (NSA_CMP_LEN - 1)
    p_c = _masked_softmax_rows(s_c, (end_c <= t_c)[None])
    o_c = _dot(p_c.reshape(R, ncmp).astype(BF16), vc_ref[...])

    p_sum = jnp.sum(p_c, axis=0)
    p_hi = p_sum.astype(BF16)
    p_lo = (p_sum - p_hi.astype(F32)).astype(BF16)
    imp_t = _dot_nt(ovl_ref[...], p_hi) + _dot_nt(ovl_ref[...], p_lo)

    blk = lax.broadcasted_iota(jnp.int32, (nsel, tq), 0)
    cur = (t0 + lax.broadcasted_iota(jnp.int32, (nsel, tq), 1)) // SB
    forced = (blk == 0) | (blk == cur) | (blk == cur - 1)
    score = jnp.where(blk <= cur, imp_t + jnp.where(forced, NSA_FORCE_BONUS, 0.0), -jnp.inf)
    rank = jnp.zeros((nsel, tq), jnp.int32)
    for sp in range(nsel):
        other = jnp.broadcast_to(score[sp:sp + 1, :], (nsel, tq))
        rank = rank + jnp.where(blk > sp, jnp.where(other >= score, 1, 0), jnp.where(other > score, 1, 0))
    sel = jnp.where((rank < top_n) & (blk <= cur), 1.0, 0.0).T.astype(BF16)

    t_k = t0 + lax.broadcasted_iota(jnp.int32, (tq, tk), 0)
    lane_k = lax.broadcasted_iota(jnp.int32, (tq, tk), 1)
    for kt in range(seq // tk):
        picked = _dot(sel, blk_ref[:, kt * tk:(kt + 1) * tk])
        bias_ref[kt] = jnp.where((picked > 0.5) & (lane_k + kt * tk <= t_k), 0.0, MASK_NEG)

    m_ref[...] = jnp.full_like(m_ref, MASK_NEG)
    l_ref[...] = jnp.zeros_like(l_ref)
    acc_ref[...] = jnp.zeros_like(acc_ref)

    def sel_step(kt, carry):
        k0 = pl.multiple_of(kt * tk, tk)
        s = _dot_nt(q_rot, ks_ref[pl.ds(k0, tk), :]).reshape(P, tq, tk) + bias_ref[kt][None]
        s = s.reshape(R, tk)
        m_old = m_ref[...]
        m_new = jnp.maximum(m_old, jnp.max(s, axis=-1, keepdims=True))
        alpha = jnp.exp(m_old - m_new)
        p = jnp.exp(s - m_new)
        l_ref[...] = alpha * l_ref[...] + jnp.sum(p, axis=-1, keepdims=True)
        acc_ref[...] = alpha * acc_ref[...] + _dot(p.astype(BF16), vs_ref[pl.ds(k0, tk), :])
        m_ref[...] = m_new
        return carry

    lax.fori_loop(0, (t0 + tq + tk - 1) // tk, sel_step, 0)
    o_s = acc_ref[...] / l_ref[...]

    wlen = W + tq
    base = pl.multiple_of(jnp.clip(t0 - W, 0, seq - wlen), tq)
    s_w = _dot_nt(q_rot, kw_ref[pl.ds(base, wlen), :]).reshape(P, tq, wlen)
    delta = (t0 + lax.broadcasted_iota(jnp.int32, (tq, wlen), 0)) - (base + lax.broadcasted_iota(jnp.int32, (tq, wlen), 1))
    p_w = _masked_softmax_rows(s_w, ((delta >= 0) & (delta < W))[None])
    o_w = _dot(p_w.reshape(R, wlen).astype(BF16), vw_ref[pl.ds(base, wlen), :])

    gates = _sigmoid(gate_ref[...])
    for p in range(P):
        rows = slice(p * tq, (p + 1) * tq)
        o_p = (gates[:, 3 * p:3 * p + 1] * o_c[rows] + gates[:, 3 * p + 1:3 * p + 2] * o_s[rows]
               + gates[:, 3 * p + 2:3 * p + 3] * o_w[rows])
        o_ref[:, p * HD:(p + 1) * HD] = o_p.astype(o_ref.dtype)


def nsa_attention(z, kv, cmp, tables, batch, seq, *, tq=128, tk=512):
    m = z.shape[0]
    ks, vs, kw, vw = kv
    tq = _tile(seq, tq)
    tk = _tile(seq, tk)
    nt = seq // tq
    nsel = seq // NSA_SEL_BLOCK
    ncmp = seq // NSA_CMP_STRIDE
    gw = NSA_HPG * NSA_HEAD_DIM
    gate_cb = (NSA_Q_WIDTH + 6 * NSA_KV_WIDTH) // LANES
    c_start = jnp.arange(ncmp) * NSA_CMP_STRIDE
    s_start = jnp.arange(nsel) * NSA_SEL_BLOCK
    ovl = ((c_start[None, :] <= s_start[:, None] + NSA_SEL_BLOCK - 1)
           & (c_start[None, :] + NSA_CMP_LEN - 1 >= s_start[:, None])
           & (jnp.arange(ncmp)[None, :] < ncmp - 1)).astype(BF16)
    blk_of_key = (jnp.arange(seq)[None, :] // NSA_SEL_BLOCK == jnp.arange(nsel)[:, None]).astype(BF16)
    kvspec = pl.BlockSpec((seq, NSA_HEAD_DIM), lambda b, g, i: (b, g))
    cmpspec = lambda s: pl.BlockSpec((None, None, None, ncmp, NSA_HEAD_DIM), lambda b, g, i: (s, b, g, 0, 0))
    tspec = pl.BlockSpec((tq, NSA_HEAD_DIM), lambda b, g, i: (i, 0))
    rows = NSA_HPG * tq
    return pl.pallas_call(
        functools.partial(_nsa_attn_kernel, tq=tq, tk=tk, seq=seq, top_n=min(NSA_TOPK, nsel)),
        grid=(batch, NSA_KV_GROUPS, nt),
        in_specs=[
            pl.BlockSpec((tq, gw), lambda b, g, i: (b * nt + i, g)),
            pl.BlockSpec((tq, LANES), lambda b, g, i: (b * nt + i, gate_cb + g)),
            tspec, tspec, tspec,
            cmpspec(0), cmpspec(1),
            kvspec, kvspec, kvspec, kvspec,
            pl.BlockSpec((nsel, ncmp), lambda b, g, i: (0, 0)),
            pl.BlockSpec((nsel, seq), lambda b, g, i: (0, 0)),
        ],
        out_specs=pl.BlockSpec((tq, gw), lambda b, g, i: (b * nt + i, g)),
        out_shape=jax.ShapeDtypeStruct((m, NSA_Q_WIDTH), BF16),
        scratch_shapes=[
            pltpu.VMEM((seq // tk, tq, tk), F32),
            pltpu.VMEM((rows, 1), F32),
            pltpu.VMEM((rows, 1), F32),
            pltpu.VMEM((rows, NSA_HEAD_DIM), F32),
        ],
        compiler_params=_params("parallel", "parallel", "arbitrary"),
        name="nsa_attn",
    )(z, z, *tables, cmp, cmp, ks, vs, kw, vw, ovl, blk_of_key)


def mixer_nsa(x, h, batch, seq, w_in, cmp_pos, cmp_w1, cmp_b1, cmp_w2, cmp_b2, w_out):
    d = w_in.shape[0]
    c0 = NSA_Q_WIDTH + 6 * NSA_KV_WIDTH
    per_group = 3 * NSA_HPG
    gate_cols = []
    for g in range(NSA_KV_GROUPS):
        gate_cols += [w_in[:, c0 + g * per_group:c0 + (g + 1) * per_group], jnp.zeros((d, LANES - per_group), w_in.dtype)]
    w = jnp.concatenate([w_in[:, :c0]] + gate_cols, axis=1).astype(BF16)
    z = matmul(h, w, tn=1280, name="nsa_in_proj")
    tables = _rope_tables(seq)
    kv = nsa_kv_prep(z, tables, batch, seq)
    cmp = nsa_compress(z, batch, seq, cmp_pos, cmp_w1, cmp_b1, cmp_w2, cmp_b2)
    o = nsa_attention(z, kv, cmp, tables, batch, seq)
    return matmul(o, w_out.astype(BF16), residual=x, name="nsa_out_proj")


def kernel(x, mem, mem_norm_g,
           ffn1_norm_g, ffn1_w_gate, ffn1_w_up, ffn1_w_down,
           mix_norm_g,
           ab_w_in, gla_w_gate_up, gla_b_gate, gla_norm_g, sg_ln_g, sg_ln_b, sg_w_s, sg_b_s, ab_w_out,
           nsa_w_in, nsa_cmp_pos, nsa_cmp_w1, nsa_cmp_b1, nsa_cmp_w2, nsa_cmp_b2, nsa_w_out,
           cross_norm_g, cross_w_q, cross_w_kv, cross_w_o,
           ffn2_norm_g, ffn2_w_gate, ffn2_w_up, ffn2_w_down,
           final_norm_g):
    batch, seq, d = x.shape
    depth = ffn1_norm_g.shape[0]
    mem_n = rmsnorm(mem.reshape(-1, d), mem_norm_g)
    x = x.reshape(batch * seq, d)
    for i in range(depth):
        x = swiglu_half_step(x, ffn1_norm_g[i], ffn1_w_gate[i], ffn1_w_up[i], ffn1_w_down[i])
        h = rmsnorm(x, mix_norm_g[i])
        j = i // 2
        if i % 2 == 0:
            x = mixer_gla_sg(x, h, batch, seq, ab_w_in[j], gla_w_gate_up[j], gla_b_gate[j], gla_norm_g[j],
                             sg_ln_g[j], sg_ln_b[j], sg_w_s[j], sg_b_s[j], ab_w_out[j])
        else:
            x = mixer_nsa(x, h, batch, seq, nsa_w_in[j], nsa_cmp_pos[j], nsa_cmp_w1[j], nsa_cmp_b1[j],
                          nsa_cmp_w2[j], nsa_cmp_b2[j], nsa_w_out[j])
        kv = matmul(mem_n, cross_w_kv[i].astype(BF16), out_dtype=BF16, name="cross_kv_proj")
        x = cross_attention(x, rmsnorm(x, cross_norm_g[i]), kv, batch, seq, cross_w_q[i], cross_w_o[i])
        x = swiglu_half_step(x, ffn2_norm_g[i], ffn2_w_gate[i], ffn2_w_up[i], ffn2_w_down[i])
    return rmsnorm(x, final_norm_g, out_dtype=F32).reshape(batch, seq, d)
```

```python
import functools

import jax
import jax.numpy as jnp
from jax import lax
from jax.experimental import pallas as pl
from jax.experimental.pallas import tpu as pltpu

F32 = jnp.float32
BF16 = jnp.bfloat16

NORM_EPS = 1e-6
LN_EPS = 1e-5
ROPE_THETA = 500000.0

GLA_HEADS = 8
GLA_DK = 128
GLA_DV = 256
GLA_K_WIDTH = GLA_HEADS * GLA_DK
GLA_V_WIDTH = GLA_HEADS * GLA_DV
GLA_GATE_RANK = 16
GLA_INV_TAU = 1.0 / 16.0
GLA_CHUNK = 64
GLA_SUB = 16
SG_WIDTH = 2048
SG_GROUPS = 4
SG_GROUP_WIDTH = SG_WIDTH // SG_GROUPS
SG_CHUNK = 128

NSA_HEAD_DIM = 128
NSA_KV_GROUPS = 4
NSA_HPG = 8
NSA_Q_WIDTH = NSA_KV_GROUPS * NSA_HPG * NSA_HEAD_DIM
NSA_KV_WIDTH = NSA_KV_GROUPS * NSA_HEAD_DIM
NSA_CMP_STRIDE = 16
NSA_CMP_LEN = 32
NSA_SEL_BLOCK = 64
NSA_TOPK = 16
NSA_WINDOW = 512
NSA_FORCE_BONUS = 1e4
NSA_SCALE = NSA_HEAD_DIM ** -0.5
ROPE_DIM = NSA_HEAD_DIM // 4
ROPE_HALF = ROPE_DIM // 2
NSA_SLAB = 64
NSA_GROUP_ROWS = 256
LOG2_E = 1.4426950408889634

X_HEADS = 4
X_HEAD_DIM = 128
X_WIDTH = X_HEADS * X_HEAD_DIM
X_SCALE = X_HEAD_DIM ** -0.5

LANES = 128
MASK_NEG = -1e30
VMEM_LIMIT_BYTES = 56 * 1024 * 1024


def _params(*sem):
    return pltpu.CompilerParams(dimension_semantics=sem, vmem_limit_bytes=VMEM_LIMIT_BYTES)


def _dot(a, b):
    return jnp.dot(a, b, preferred_element_type=F32)


def _dot_nt(a, b):
    return lax.dot_general(a, b, (((1,), (1,)), ((), ())), preferred_element_type=F32)


def _dot_tn(a, b):
    return lax.dot_general(a, b, (((0,), (0,)), ((), ())), preferred_element_type=F32)


def _sigmoid(x):
    return 1.0 / (1.0 + jnp.exp(-x))


def _silu(x):
    return x * _sigmoid(x)


def _gelu_tanh(x):
    c = 0.7978845608028654
    return x * (0.5 * (1.0 + jnp.tanh(c * (x + 0.044715 * (x * x * x)))))


def _tile(n, pref):
    if n <= pref:
        return n
    t = pref
    while n % t:
        t //= 2
    return t


def _rmsnorm_kernel(x_ref, g_ref, o_ref):
    x = x_ref[...]
    ms = jnp.mean(x * x, axis=-1, keepdims=True)
    o_ref[...] = (x * lax.rsqrt(ms + NORM_EPS) * g_ref[...]).astype(o_ref.dtype)


def rmsnorm(x, g, out_dtype=BF16):
    m, d = x.shape
    tm = _tile(m, 256)
    return pl.pallas_call(
        _rmsnorm_kernel,
        grid=(m // tm,),
        in_specs=[pl.BlockSpec((tm, d), lambda i: (i, 0)), pl.BlockSpec((1, d), lambda i: (0, 0))],
        out_specs=pl.BlockSpec((tm, d), lambda i: (i, 0)),
        out_shape=jax.ShapeDtypeStruct((m, d), out_dtype),
        compiler_params=_params("parallel"),
        name="rmsnorm",
    )(x, g.reshape(1, d))


def _mm_kernel(a_ref, w_ref, o_ref):
    o_ref[...] = _dot(a_ref[...], w_ref[...]).astype(o_ref.dtype)


def _mm_res_kernel(a_ref, w_ref, r_ref, o_ref, *, scale):
    o_ref[...] = r_ref[...] + scale * _dot(a_ref[...], w_ref[...])


def matmul(a, w, *, out_dtype=F32, tm=1024, tn=1024, residual=None, scale=1.0, n_cols=None, name="matmul"):
    m, k = a.shape
    n = w.shape[1] if n_cols is None else n_cols
    tm = _tile(m, tm)
    tn = _tile(n, tn)
    in_specs = [pl.BlockSpec((tm, k), lambda i, j: (i, 0)), pl.BlockSpec((k, tn), lambda i, j: (0, j))]
    args = [a, w]
    if residual is None:
        body = _mm_kernel
    else:
        body = functools.partial(_mm_res_kernel, scale=scale)
        in_specs.append(pl.BlockSpec((tm, tn), lambda i, j: (i, j)))
        args.append(residual)
        out_dtype = F32
    return pl.pallas_call(
        body,
        grid=(m // tm, n // tn),
        in_specs=in_specs,
        out_specs=pl.BlockSpec((tm, tn), lambda i, j: (i, j)),
        out_shape=jax.ShapeDtypeStruct((m, n), out_dtype),
        compiler_params=_params("parallel", "arbitrary"),
        name=name,
    )(*args)


def _mm2_res_kernel(a1_ref, a2_ref, w1_ref, w2_ref, r_ref, o_ref):
    o_ref[...] = r_ref[...] + (_dot(a1_ref[...], w1_ref[...]) + _dot(a2_ref[...], w2_ref[...]))


def matmul2_residual(a1, a2, w, residual, *, tm=1024, tn=1024, name="matmul2"):
    m, kh = a1.shape
    n = w.shape[1]
    tm = _tile(m, tm)
    tn = _tile(n, tn)
    return pl.pallas_call(
        _mm2_res_kernel,
        grid=(m // tm, n // tn),
        in_specs=[
            pl.BlockSpec((tm, kh), lambda i, j: (i, 0)),
            pl.BlockSpec((tm, kh), lambda i, j: (i, 0)),
            pl.BlockSpec((kh, tn), lambda i, j: (0, j)),
            pl.BlockSpec((kh, tn), lambda i, j: (1, j)),
            pl.BlockSpec((tm, tn), lambda i, j: (i, j)),
        ],
        out_specs=pl.BlockSpec((tm, tn), lambda i, j: (i, j)),
        out_shape=jax.ShapeDtypeStruct((m, n), F32),
        compiler_params=_params("parallel", "arbitrary"),
        name=name,
    )(a1, a2, w, w, residual)


def _ffn_up_kernel(h_ref, wg_ref, wu_ref, o_ref):
    h = h_ref[...]
    g = _dot(h, wg_ref[...])
    u = _dot(h, wu_ref[...])
    o_ref[...] = (_silu(g) * u).astype(o_ref.dtype)


def ffn_up(h, wg, wu, *, tm=1024, tn=512):
    m, k = h.shape
    n = wg.shape[1]
    tm = _tile(m, tm)
    tn = _tile(n, tn)
    return pl.pallas_call(
        _ffn_up_kernel,
        grid=(m // tm, n // tn),
        in_specs=[
            pl.BlockSpec((tm, k), lambda i, j: (i, 0)),
            pl.BlockSpec((k, tn), lambda i, j: (0, j)),
            pl.BlockSpec((k, tn), lambda i, j: (0, j)),
        ],
        out_specs=pl.BlockSpec((tm, tn), lambda i, j: (i, j)),
        out_shape=jax.ShapeDtypeStruct((m, n), BF16),
        compiler_params=_params("parallel", "arbitrary"),
        name="ffn_up",
    )(h, wg, wu)


def swiglu_half_step(x, norm_g, w_gate, w_up, w_down):
    h = rmsnorm(x, norm_g)
    act = ffn_up(h, w_gate.astype(BF16), w_up.astype(BF16))
    return matmul(act, w_down.astype(BF16), residual=x, scale=0.5, tn=512, name="ffn_down")


def _gla_kernel(q_ref, k_ref, v_ref, r_ref, a_ref, wgu_ref, bg_ref, ng_ref, o_ref, st_ref, *, rt):
    C, SB = GLA_CHUNK, GLA_SUB

    @pl.when(pl.program_id(2) == 0)
    def _():
        st_ref[...] = jnp.zeros_like(st_ref)

    zg = _dot(a_ref[...].astype(BF16), wgu_ref[...]) + bg_ref[...]
    lf = (jnp.minimum(zg, 0.0) - jnp.log(1.0 + jnp.exp(-jnp.abs(zg)))) * GLA_INV_TAU

    row = lax.broadcasted_iota(jnp.int32, (rt, GLA_DK), 0)
    rc = row & (C - 1)
    b = lf
    for s in (1, 2, 4, 8, 16, 32):
        b = b + jnp.where(rc >= s, pltpu.roll(b, s, 0), 0.0)

    q = q_ref[...] * (GLA_DK ** -0.5)
    k = k_ref[...]
    v = v_ref[...].astype(BF16)

    nb = rt // SB
    b3 = b.reshape(nb, SB, GLA_DK)
    q3 = q.reshape(nb, SB, GLA_DK)
    k3 = k.reshape(nb, SB, GLA_DK)
    rowc = lax.broadcasted_iota(jnp.int32, (rt, C), 0)
    lane = lax.broadcasted_iota(jnp.int32, (rt, C), 1)
    sub_row = rowc & (SB - 1)
    sub_base = (rowc & (C - 1)) - sub_row
    a_diag = jnp.zeros((rt, C), F32)
    for j in range(SB):
        bj = jnp.broadcast_to(b3[:, j:j + 1, :], (nb, SB, GLA_DK))
        kj = jnp.broadcast_to(k3[:, j:j + 1, :], (nb, SB, GLA_DK))
        pj = (q3 * kj * jnp.exp(jnp.minimum(b3 - bj, 0.0))).reshape(rt, GLA_DK)
        rs = jnp.sum(pj, axis=-1, keepdims=True)
        a_diag = jnp.where((lane == sub_base + j) & (sub_row >= j), rs, a_diag)

    lane_c = lax.broadcasted_iota(jnp.int32, (SB, C), 1)
    outs = []
    for c in range(rt // C):
        sl = slice(c * C, (c + 1) * C)
        bc, qc, kc, vc = b[sl], q[sl], k[sl], v[sl]
        rows = [jnp.zeros((SB, C), F32)]
        for i in range(1, C // SB):
            b0 = bc[i * SB:i * SB + 1]
            qi = qc[i * SB:(i + 1) * SB] * jnp.exp(bc[i * SB:(i + 1) * SB] - b0)
            ki = kc * jnp.exp(jnp.minimum(b0 - bc, 0.0))
            ai = _dot_nt(qi.astype(BF16), ki.astype(BF16))
            rows.append(jnp.where(lane_c < i * SB, ai, 0.0))
        a_c = a_diag[sl] + jnp.concatenate(rows, axis=0)
        o_intra = _dot(a_c.astype(BF16), vc)
        st = st_ref[...]
        o_inter = _dot_nt((qc * jnp.exp(bc)).astype(BF16), st.astype(BF16))
        bl = bc[C - 1:C]
        kd = kc * jnp.exp(bl - bc)
        st_ref[...] = st * jnp.exp(bl) + _dot_tn(vc, kd.astype(BF16))
        outs.append(o_inter + o_intra)
    o = jnp.concatenate(outs, axis=0)

    ms = jnp.mean(o * o, axis=-1, keepdims=True)
    y = o * lax.rsqrt(ms + NORM_EPS) * ng_ref[...]
    o_ref[...] = (y * _silu(r_ref[...])).astype(o_ref.dtype)


def gla_mixer(z, za, w_gate_up, b_gate, norm_g, batch, seq, *, rt=512):
    m = z.shape[0]
    rt = _tile(seq, rt)
    nt = seq // rt
    kb = GLA_K_WIDTH // GLA_DK
    vb = 2 * GLA_K_WIDTH // GLA_DV
    rb = vb + GLA_V_WIDTH // GLA_DV
    wgu = jnp.zeros((LANES, GLA_K_WIDTH), BF16).at[:GLA_GATE_RANK].set(w_gate_up.astype(BF16))
    row = lambda b, h, t: b * nt + t
    return pl.pallas_call(
        functools.partial(_gla_kernel, rt=rt),
        grid=(batch, GLA_HEADS, nt),
        in_specs=[
            pl.BlockSpec((rt, GLA_DK), lambda b, h, t: (row(b, h, t), h)),
            pl.BlockSpec((rt, GLA_DK), lambda b, h, t: (row(b, h, t), kb + h)),
            pl.BlockSpec((rt, GLA_DV), lambda b, h, t: (row(b, h, t), vb + h)),
            pl.BlockSpec((rt, GLA_DV), lambda b, h, t: (row(b, h, t), rb + h)),
            pl.BlockSpec((rt, LANES), lambda b, h, t: (row(b, h, t), 0)),
            pl.BlockSpec((LANES, GLA_DK), lambda b, h, t: (0, h)),
            pl.BlockSpec((1, GLA_DK), lambda b, h, t: (0, h)),
            pl.BlockSpec((1, GLA_DV), lambda b, h, t: (0, 0)),
        ],
        out_specs=pl.BlockSpec((rt, GLA_DV), lambda b, h, t: (row(b, h, t), h)),
        out_shape=jax.ShapeDtypeStruct((m, GLA_V_WIDTH), BF16),
        scratch_shapes=[pltpu.VMEM((GLA_DV, GLA_DK), F32)],
        compiler_params=_params("parallel", "parallel", "arbitrary"),
        name="gla",
    )(z, z, z, z, za, wgu, b_gate.reshape(1, GLA_K_WIDTH), norm_g.reshape(1, GLA_DV))


def _sg_kernel(u_ref, s_ref, lng_ref, lnb_ref, ws_ref, bs_ref, o_ref, *, rt):
    s = _gelu_tanh(s_ref[...])
    mu = jnp.mean(s, axis=-1, keepdims=True)
    sc = s - mu
    var = jnp.mean(sc * sc, axis=-1, keepdims=True)
    sn = (sc * lax.rsqrt(var + LN_EPS) * lng_ref[...] + lnb_ref[...]).astype(BF16)
    u = _gelu_tanh(u_ref[...])
    ri = lax.broadcasted_iota(jnp.int32, (SG_CHUNK, SG_CHUNK), 0)
    ci = lax.broadcasted_iota(jnp.int32, (SG_CHUNK, SG_CHUNK), 1)
    for g in range(SG_GROUPS):
        w = jnp.where(ci <= ri, ws_ref[g], 0.0).astype(BF16)
        cols = slice(g * SG_GROUP_WIDTH, (g + 1) * SG_GROUP_WIDTH)
        for c in range(rt // SG_CHUNK):
            rows = slice(c * SG_CHUNK, (c + 1) * SG_CHUNK)
            mixed = _dot(w, sn[rows, cols]) + bs_ref[g]
            o_ref[rows, cols] = (u[rows, cols] * mixed).astype(o_ref.dtype)


def sg_mixer(z, ln_g, ln_b, w_s, b_s, *, rt=512):
    m = z.shape[0]
    rt = _tile(m, rt)
    return pl.pallas_call(
        functools.partial(_sg_kernel, rt=rt),
        grid=(m // rt,),
        in_specs=[
            pl.BlockSpec((rt, SG_WIDTH), lambda i: (i, 0)),
            pl.BlockSpec((rt, SG_WIDTH), lambda i: (i, 1)),
            pl.BlockSpec((1, SG_WIDTH), lambda i: (0, 0)),
            pl.BlockSpec((1, SG_WIDTH), lambda i: (0, 0)),
            pl.BlockSpec((SG_GROUPS, SG_CHUNK, SG_CHUNK), lambda i: (0, 0, 0)),
            pl.BlockSpec((SG_GROUPS, SG_CHUNK, 1), lambda i: (0, 0, 0)),
        ],
        out_specs=pl.BlockSpec((rt, SG_WIDTH), lambda i: (i, 0)),
        out_shape=jax.ShapeDtypeStruct((m, SG_WIDTH), BF16),
        compiler_params=_params("parallel"),
        name="spatial_gate",
    )(z, z, ln_g.reshape(1, SG_WIDTH), ln_b.reshape(1, SG_WIDTH), w_s, b_s.reshape(SG_GROUPS, SG_CHUNK, 1))


def mixer_gla_sg(x, h, batch, seq, w_in, w_gate_up, b_gate, gla_norm_g, sg_ln_g, sg_ln_b, sg_w_s, sg_b_s, w_out):
    d = w_in.shape[0]
    a0 = 2 * GLA_K_WIDTH + 2 * GLA_V_WIDTH
    wb = w_in.astype(BF16)
    w_a = jnp.concatenate([wb[:, a0:a0 + GLA_GATE_RANK], jnp.zeros((d, LANES - GLA_GATE_RANK), BF16)], axis=1)
    z = matmul(h, wb, n_cols=a0, name="ab_in_proj")
    za = matmul(h, w_a, name="ab_gate_proj")
    zus = matmul(h, wb[:, a0 + GLA_GATE_RANK:], name="ab_sg_proj")
    y_gla = gla_mixer(z, za, w_gate_up, b_gate, gla_norm_g, batch, seq)
    y_sg = sg_mixer(zus, sg_ln_g, sg_ln_b, sg_w_s, sg_b_s)
    return matmul2_residual(y_gla, y_sg, w_out.astype(BF16), x, name="ab_out_proj")


def _cross_kernel(q_ref, k_ref, v_ref, o_ref):
    for hd in range(X_HEADS):
        cols = slice(hd * X_HEAD_DIM, (hd + 1) * X_HEAD_DIM)
        s = _dot_nt(q_ref[:, cols], k_ref[:, cols]) * X_SCALE
        p = jnp.exp(s - jnp.max(s, axis=-1, keepdims=True))
        p = p / jnp.sum(p, axis=-1, keepdims=True)
        o_ref[:, cols] = _dot(p.astype(BF16), v_ref[:, cols]).astype(o_ref.dtype)


def cross_attention(x, h, kv, batch, seq, w_q, w_o, *, tq=1024):
    m = h.shape[0]
    mem = kv.shape[0] // batch
    tq = _tile(seq, tq)
    nt = seq // tq
    q = matmul(h, w_q.astype(BF16), out_dtype=BF16, name="cross_q_proj")
    o = pl.pallas_call(
        _cross_kernel,
        grid=(batch, nt),
        in_specs=[
            pl.BlockSpec((tq, X_WIDTH), lambda b, t: (b * nt + t, 0)),
            pl.BlockSpec((mem, X_WIDTH), lambda b, t: (b, 0)),
            pl.BlockSpec((mem, X_WIDTH), lambda b, t: (b, 1)),
        ],
        out_specs=pl.BlockSpec((tq, X_WIDTH), lambda b, t: (b * nt + t, 0)),
        out_shape=jax.ShapeDtypeStruct((m, X_WIDTH), BF16),
        compiler_params=_params("parallel", "parallel"),
        name="cross_attn",
    )(q, kv, kv)
    return matmul(o, w_o.astype(BF16), residual=x, name="cross_o_proj")


def _rope_tables(seq):
    inv_freq = jnp.power(ROPE_THETA, -jnp.arange(ROPE_HALF, dtype=F32) / ROPE_HALF)
    ang = jnp.arange(seq).astype(F32)[:, None] * inv_freq[None, :]
    cos, sin = jnp.cos(ang), jnp.sin(ang)
    rest = NSA_HEAD_DIM - ROPE_DIM
    c = jnp.concatenate([cos, cos, jnp.ones((seq, rest), F32)], axis=1)
    s_up = jnp.concatenate([jnp.zeros_like(sin), sin, jnp.zeros((seq, rest), F32)], axis=1)
    s_dn = jnp.concatenate([-sin, jnp.zeros_like(sin), jnp.zeros((seq, rest), F32)], axis=1)
    return c, s_up, s_dn


def _rope(x, c, s_up, s_dn, heads):
    width = heads * NSA_HEAD_DIM
    rep = lambda t: jnp.concatenate([t] * heads, axis=1)
    return (x * rep(c) + pltpu.roll(x, ROPE_HALF, 1) * rep(s_up)
            + pltpu.roll(x, width - ROPE_HALF, 1) * rep(s_dn))


def _nsa_kv_kernel(ks_ref, vs_ref, kw_ref, vw_ref, c_ref, su_ref, sd_ref, oks_ref, ovs_ref, okw_ref, ovw_ref):
    c, su, sd = c_ref[...], su_ref[...], sd_ref[...]
    oks_ref[...] = _rope(ks_ref[...], c, su, sd, NSA_KV_GROUPS).astype(BF16)
    okw_ref[...] = _rope(kw_ref[...], c, su, sd, NSA_KV_GROUPS).astype(BF16)
    ones = jnp.ones((ks_ref.shape[0], NSA_HEAD_DIM), BF16)
    for v_ref, o_ref in ((vs_ref, ovs_ref), (vw_ref, ovw_ref)):
        v = v_ref[...].astype(BF16)
        o_ref[...] = jnp.concatenate(
            [piece for g in range(NSA_KV_GROUPS) for piece in (v[:, g * NSA_HEAD_DIM:(g + 1) * NSA_HEAD_DIM], ones)],
            axis=1)


def nsa_kv_prep(z, tables, batch, seq, *, rt=1024):
    m = z.shape[0]
    rt = _tile(seq, rt)
    nt = seq // rt
    cb = NSA_Q_WIDTH // NSA_KV_WIDTH + 2
    zspec = lambda j: pl.BlockSpec((rt, NSA_KV_WIDTH), lambda b, t: (b * nt + t, cb + j))
    tspec = pl.BlockSpec((rt, NSA_HEAD_DIM), lambda b, t: (t, 0))
    kspec = pl.BlockSpec((rt, NSA_KV_WIDTH), lambda b, t: (b * nt + t, 0))
    vspec = pl.BlockSpec((rt, 2 * NSA_KV_WIDTH), lambda b, t: (b * nt + t, 0))
    ksds = jax.ShapeDtypeStruct((m, NSA_KV_WIDTH), BF16)
    vsds = jax.ShapeDtypeStruct((m, 2 * NSA_KV_WIDTH), BF16)
    return pl.pallas_call(
        _nsa_kv_kernel,
        grid=(batch, nt),
        in_specs=[zspec(0), zspec(1), zspec(2), zspec(3), tspec, tspec, tspec],
        out_specs=[kspec, vspec, kspec, vspec],
        out_shape=[ksds, vsds, ksds, vsds],
        compiler_params=_params("parallel", "parallel"),
        name="nsa_kv_prep",
    )(z, z, z, z, *tables)


def _nsa_cmp_kernel(x_ref, pos_ref, w1_ref, b1_ref, w2_ref, b2_ref, o_ref, *, nblk):
    half = NSA_CMP_STRIDE * NSA_HEAD_DIM
    pos_bias = _dot(pos_ref[...], w1_ref[...])[0:1] + b1_ref[...]
    row = lax.broadcasted_iota(jnp.int32, (nblk, NSA_HEAD_DIM), 0)
    for g in range(NSA_KV_GROUPS):
        xg = jnp.concatenate(
            [x_ref[:, l * NSA_KV_WIDTH + g * NSA_HEAD_DIM:l * NSA_KV_WIDTH + (g + 1) * NSA_HEAD_DIM]
             for l in range(NSA_CMP_STRIDE)], axis=1).astype(BF16)
        first = _dot(xg, w1_ref[:half])
        second = _dot(xg, w1_ref[half:])
        pre = first + pltpu.roll(second, nblk - 1, 0) + pos_bias
        out = _dot(_silu(pre).astype(BF16), w2_ref[...]) + b2_ref[...]
        o_ref[g] = jnp.where(row < nblk - 1, out, 0.0).astype(o_ref.dtype)


def nsa_compress(z, batch, seq, cmp_pos, cmp_w1, cmp_b1, cmp_w2, cmp_b2):
    nblk = seq // NSA_CMP_STRIDE
    c0 = NSA_Q_WIDTH
    xr = jnp.stack([z[:, c0:c0 + NSA_KV_WIDTH], z[:, c0 + NSA_KV_WIDTH:c0 + 2 * NSA_KV_WIDTH]])
    xr = xr.reshape(2, batch, nblk, NSA_CMP_STRIDE * NSA_KV_WIDTH)
    flat = NSA_CMP_LEN * NSA_HEAD_DIM
    pos = jnp.zeros((2, 8, flat), BF16).at[:, 0].set(cmp_pos.reshape(2, flat).astype(BF16))
    return pl.pallas_call(
        functools.partial(_nsa_cmp_kernel, nblk=nblk),
        grid=(2, batch),
        in_specs=[
            pl.BlockSpec((None, None, nblk, NSA_CMP_STRIDE * NSA_KV_WIDTH), lambda s, b: (s, b, 0, 0)),
            pl.BlockSpec((None, 8, flat), lambda s, b: (s, 0, 0)),
            pl.BlockSpec((None, flat, NSA_HEAD_DIM), lambda s, b: (s, 0, 0)),
            pl.BlockSpec((None, 1, NSA_HEAD_DIM), lambda s, b: (s, 0, 0)),
            pl.BlockSpec((None, NSA_HEAD_DIM, NSA_HEAD_DIM), lambda s, b: (s, 0, 0)),
            pl.BlockSpec((None, 1, NSA_HEAD_DIM), lambda s, b: (s, 0, 0)),
        ],
        out_specs=pl.BlockSpec((None, None, NSA_KV_GROUPS, nblk, NSA_HEAD_DIM), lambda s, b: (s, b, 0, 0, 0)),
        out_shape=jax.ShapeDtypeStruct((2, batch, NSA_KV_GROUPS, nblk, NSA_HEAD_DIM), BF16),
        compiler_params=_params("parallel", "parallel"),
        name="nsa_compress",
    )(xr, pos, cmp_w1.astype(BF16), cmp_b1.reshape(2, 1, NSA_HEAD_DIM), cmp_w2.astype(BF16),
      cmp_b2.reshape(2, 1, NSA_HEAD_DIM))


def _row_reduce(s, op, lane_op):
    acc = s[:, :LANES]
    for c in range(1, s.shape[1] // LANES):
        acc = op(acc, s[:, c * LANES:(c + 1) * LANES])
    return lane_op(acc, axis=-1, keepdims=True)


def _nsa_attn_kernel(q_ref, gate_ref, c_ref, su_ref, sd_ref, kc_ref, vc_ref, ks_ref, vs_ref, kw_ref, vw_ref,
                     ovl_ref, blk_ref, spread_ref, o_ref,
                     qp_ref, qr_ref, s_ref, p_ref, bias_ref, m_ref, acc_ref, psum_ref, sel_ref, mix_ref, gx_ref,
                     *, tq, tk, seq, top_n):
    P, HD, SB, W, SL = NSA_HPG, NSA_HEAD_DIM, NSA_SEL_BLOCK, NSA_WINDOW, NSA_SLAB
    R = P * tq
    nsel = seq // SB
    ncmp = seq // NSA_CMP_STRIDE
    t0 = pl.program_id(2) * tq

    qf = q_ref[...]
    q_scale = NSA_SCALE * LOG2_E
    qr = _rope(qf, c_ref[...], su_ref[...], sd_ref[...], P)
    for p in range(P):
        qp_ref[p * tq:(p + 1) * tq] = (qf[:, p * HD:(p + 1) * HD] * q_scale).astype(BF16)
        qr_ref[p * tq:(p + 1) * tq] = (qr[:, p * HD:(p + 1) * HD] * q_scale).astype(BF16)

    def reset_state():
        m_ref[...] = jnp.full_like(m_ref, MASK_NEG)
        acc_ref[...] = jnp.zeros_like(acc_ref)

    def softmax_tile(k_t, v_t, n):
        for g0 in range(0, R, NSA_GROUP_ROWS):
            grp = slice(g0, g0 + NSA_GROUP_ROWS)
            s_grp = _dot_nt(qr_ref[grp], k_t)
            alphas, probs = [], []
            for r0 in range(g0, g0 + NSA_GROUP_ROWS, SL):
                rows = slice(r0, r0 + SL)
                b0 = r0 % tq
                s = s_grp[r0 - g0:r0 - g0 + SL] + bias_ref[b0:b0 + SL, :n]
                m_old = m_ref[rows]
                m_new = jnp.maximum(m_old, _row_reduce(s, jnp.maximum, jnp.max))
                alphas.append(jnp.exp2(m_old - m_new))
                m_ref[rows] = m_new
                probs.append(jnp.exp2(s - m_new).astype(BF16))
            alpha = jnp.concatenate(alphas, axis=0)
            acc_ref[grp] = alpha * acc_ref[grp] + _dot(jnp.concatenate(probs, axis=0), v_t)

    t_c = t0 + lax.broadcasted_iota(jnp.int32, (tq, ncmp), 0)
    end_c = lax.broadcasted_iota(jnp.int32, (tq, ncmp), 1) * NSA_CMP_STRIDE + (NSA_CMP_LEN - 1)
    bias_ref[:, :ncmp] = jnp.where(end_c <= t_c, 0.0, MASK_NEG)
    s_ref[:, :ncmp] = _dot_nt(qp_ref[...], kc_ref[...])
    psum_ref[...] = jnp.zeros_like(psum_ref)
    for j in range(R // SL):
        rows = slice(j * SL, (j + 1) * SL)
        b0 = (j * SL) % tq
        bias = bias_ref[b0:b0 + SL, :ncmp]
        s = s_ref[rows, :ncmp] + bias
        e = jnp.where(bias == 0.0, jnp.exp2(s - _row_reduce(s, jnp.maximum, jnp.max)), 0.0)
        pn = e / jnp.maximum(_row_reduce(e, jnp.add, jnp.sum), 1e-30)
        p_ref[rows, :ncmp] = pn.astype(BF16)
        psum_ref[b0:b0 + SL] += pn
    o_c = _dot(p_ref[:, :ncmp], vc_ref[...])
    gates = _sigmoid(gate_ref[...])
    g_hi = gates.astype(BF16)
    g_lo = (gates - g_hi.astype(F32)).astype(BF16)
    gx_ref[...] = _dot(g_hi, spread_ref[...]) + _dot(g_lo, spread_ref[...])
    gate = lambda p, j: gx_ref[:, (3 * p + j) * LANES:(3 * p + j + 1) * LANES]
    for p in range(P):
        rows = slice(p * tq, (p + 1) * tq)
        mix_ref[rows] = gate(p, 0) * o_c[rows]

    p_sum = psum_ref[...]
    p_hi = p_sum.astype(BF16)
    p_lo = (p_sum - p_hi.astype(F32)).astype(BF16)
    imp_t = _dot_nt(ovl_ref[...], p_hi) + _dot_nt(ovl_ref[...], p_lo)

    blk = lax.broadcasted_iota(jnp.int32, (nsel, tq), 0)
    cur = (t0 + lax.broadcasted_iota(jnp.int32, (nsel, tq), 1)) // SB
    forced = (blk == 0) | (blk == cur) | (blk == cur - 1)
    score = jnp.where(blk <= cur, imp_t + jnp.where(forced, NSA_FORCE_BONUS, 0.0), -jnp.inf)
    rank = jnp.zeros((nsel, tq), jnp.int32)
    for sp in range(nsel):
        other = jnp.broadcast_to(score[sp:sp + 1, :], (nsel, tq))
        rank = rank + jnp.where(blk > sp, jnp.where(other >= score, 1, 0), jnp.where(other > score, 1, 0))
    sel_ref[...] = jnp.where((rank < top_n) & (blk <= cur), 1.0, 0.0).T.astype(BF16)

    wlen = W + tq
    base = pl.multiple_of(jnp.clip(t0 - W, 0, seq - wlen), tq)
    delta = (t0 + lax.broadcasted_iota(jnp.int32, (tq, wlen), 0)) - (base + lax.broadcasted_iota(jnp.int32, (tq, wlen), 1))
    bias_ref[:, :wlen] = jnp.where((delta >= 0) & (delta < W), 0.0, MASK_NEG)
    reset_state()
    softmax_tile(kw_ref[pl.ds(base, wlen), :], vw_ref[pl.ds(base, wlen), :], wlen)
    for p in range(P):
        rows = slice(p * tq, (p + 1) * tq)
        o_w = acc_ref[rows, :HD] / acc_ref[rows, HD:]
        mix_ref[rows] += gate(p, 2) * o_w

    t_k = t0 + lax.broadcasted_iota(jnp.int32, (tq, tk), 0)
    lane_k = lax.broadcasted_iota(jnp.int32, (tq, tk), 1)
    reset_state()

    def sel_step(kt, carry):
        k0 = pl.multiple_of(kt * tk, tk)
        picked = _dot(sel_ref[...], blk_ref[kt])
        bias_ref[:, :tk] = jnp.where((picked > 0.5) & (lane_k + k0 <= t_k), 0.0, MASK_NEG)
        softmax_tile(ks_ref[pl.ds(k0, tk), :], vs_ref[pl.ds(k0, tk), :], tk)
        return carry

    lax.fori_loop(0, (t0 + tq + tk - 1) // tk, sel_step, 0)
    for p in range(P):
        rows = slice(p * tq, (p + 1) * tq)
        o_s = acc_ref[rows, :HD] / acc_ref[rows, HD:]
        o_ref[:, p * HD:(p + 1) * HD] = (mix_ref[rows] + gate(p, 1) * o_s).astype(o_ref.dtype)


def nsa_attention(z, zg, kv, cmp, tables, batch, seq, *, tq=256, tk=1024):
    m = z.shape[0]
    ks, vs, kw, vw = kv
    tq = _tile(seq, tq)
    tk = _tile(seq, tk)
    nt = seq // tq
    nsel = seq // NSA_SEL_BLOCK
    ncmp = seq // NSA_CMP_STRIDE
    gw = NSA_HPG * NSA_HEAD_DIM
    c_start = jnp.arange(ncmp) * NSA_CMP_STRIDE
    s_start = jnp.arange(nsel) * NSA_SEL_BLOCK
    ovl = ((c_start[None, :] <= s_start[:, None] + NSA_SEL_BLOCK - 1)
           & (c_start[None, :] + NSA_CMP_LEN - 1 >= s_start[:, None])
           & (jnp.arange(ncmp)[None, :] < ncmp - 1)).astype(BF16)
    blk_of_key = (jnp.arange(seq)[None, :] // NSA_SEL_BLOCK == jnp.arange(nsel)[:, None]).astype(BF16)
    blk_of_key = blk_of_key.reshape(nsel, seq // tk, tk).transpose(1, 0, 2)
    n_gate = 3 * NSA_HPG
    spread = (jnp.arange(LANES)[:, None] == jnp.arange(n_gate * LANES)[None, :] // LANES).astype(BF16)
    kspec =pl.BlockSpec((seq, NSA_HEAD_DIM), lambda b, g, i: (b, g))
    vspec = pl.BlockSpec((seq, 2 * NSA_HEAD_DIM), lambda b, g, i: (b, g))
    cmpspec = lambda s: pl.BlockSpec((None, None, None, ncmp, NSA_HEAD_DIM), lambda b, g, i: (s, b, g, 0, 0))
    tspec = pl.BlockSpec((tq, NSA_HEAD_DIM), lambda b, g, i: (i, 0))
    rows = NSA_HPG * tq
    width = max(tk, NSA_WINDOW + tq, ncmp)
    return pl.pallas_call(
        functools.partial(_nsa_attn_kernel, tq=tq, tk=tk, seq=seq, top_n=min(NSA_TOPK, nsel)),
        grid=(batch, NSA_KV_GROUPS, nt),
        in_specs=[
            pl.BlockSpec((tq, gw), lambda b, g, i: (b * nt + i, g)),
            pl.BlockSpec((tq, LANES), lambda b, g, i: (b * nt + i, g)),
            tspec, tspec, tspec,
            cmpspec(0), cmpspec(1),
            kspec, vspec, kspec, vspec,
            pl.BlockSpec((nsel, ncmp), lambda b, g, i: (0, 0)),
            pl.BlockSpec((seq // tk, nsel, tk), lambda b, g, i: (0, 0, 0)),
            pl.BlockSpec((LANES, n_gate * LANES), lambda b, g, i: (0, 0)),
        ],
        out_specs=pl.BlockSpec((tq, gw), lambda b, g, i: (b * nt + i, g)),
        out_shape=jax.ShapeDtypeStruct((m, NSA_Q_WIDTH), BF16),
        scratch_shapes=[
            pltpu.VMEM((rows, NSA_HEAD_DIM), BF16),
            pltpu.VMEM((rows, NSA_HEAD_DIM), BF16),
            pltpu.VMEM((rows, width), F32),
            pltpu.VMEM((rows, width), BF16),
            pltpu.VMEM((tq, width), F32),
            pltpu.VMEM((rows, 1), F32),
            pltpu.VMEM((rows, 2 * NSA_HEAD_DIM), F32),
            pltpu.VMEM((tq, ncmp), F32),
            pltpu.VMEM((tq, nsel), BF16),
            pltpu.VMEM((rows, NSA_HEAD_DIM), F32),
            pltpu.VMEM((tq, n_gate * LANES), F32),
        ],
        compiler_params=_params("parallel", "parallel", "arbitrary"),
        name="nsa_attn",
    )(z, zg, *tables, cmp, cmp, ks, vs, kw, vw, ovl, blk_of_key, spread)


def mixer_nsa(x, h, batch, seq, w_in, cmp_pos, cmp_w1, cmp_b1, cmp_w2, cmp_b2, w_out):
    d = w_in.shape[0]
    c0 = NSA_Q_WIDTH + 6 * NSA_KV_WIDTH
    per_group = 3 * NSA_HPG
    wb = w_in.astype(BF16)
    gate_cols = []
    for g in range(NSA_KV_GROUPS):
        gate_cols += [wb[:, c0 + g * per_group:c0 + (g + 1) * per_group], jnp.zeros((d, LANES - per_group), BF16)]
    z = matmul(h, wb, n_cols=c0, name="nsa_in_proj")
    zg = matmul(h, jnp.concatenate(gate_cols, axis=1), name="nsa_gate_proj")
    tables = _rope_tables(seq)
    kv = nsa_kv_prep(z, tables, batch, seq)
    cmp = nsa_compress(z, batch, seq, cmp_pos, cmp_w1, cmp_b1, cmp_w2, cmp_b2)
    o = nsa_attention(z, zg, kv, cmp, tables, batch, seq)
    return matmul(o, w_out.astype(BF16), residual=x, name="nsa_out_proj")


def kernel(x, mem, mem_norm_g,
           ffn1_norm_g, ffn1_w_gate, ffn1_w_up, ffn1_w_down,
           mix_norm_g,
           ab_w_in, gla_w_gate_up, gla_b_gate, gla_norm_g, sg_ln_g, sg_ln_b, sg_w_s, sg_b_s, ab_w_out,
           nsa_w_in, nsa_cmp_pos, nsa_cmp_w1, nsa_cmp_b1, nsa_cmp_w2, nsa_cmp_b2, nsa_w_out,
           cross_norm_g, cross_w_q, cross_w_kv, cross_w_o,
           ffn2_norm_g, ffn2_w_gate, ffn2_w_up, ffn2_w_down,
           final_norm_g):
    batch, seq, d = x.shape
    depth = ffn1_norm_g.shape[0]
    mem_n = rmsnorm(mem.reshape(-1, d), mem_norm_g)
    x = x.reshape(batch * seq, d)
    for i in range(depth):
        x = swiglu_half_step(x, ffn1_norm_g[i], ffn1_w_gate[i], ffn1_w_up[i], ffn1_w_down[i])
        h = rmsnorm(x, mix_norm_g[i])
        j = i // 2
        if i % 2 == 0:
            x = mixer_gla_sg(x, h, batch, seq, ab_w_in[j], gla_w_gate_up[j], gla_b_gate[j], gla_norm_g[j],
                             sg_ln_g[j], sg_ln_b[j], sg_w_s[j], sg_b_s[j], ab_w_out[j])
        else:
            x = mixer_nsa(x, h, batch, seq, nsa_w_in[j], nsa_cmp_pos[j], nsa_cmp_w1[j], nsa_cmp_b1[j],
                          nsa_cmp_w2[j], nsa_cmp_b2[j], nsa_w_out[j])
        kv = matmul(mem_n, cross_w_kv[i].astype(BF16), out_dtype=BF16, name="cross_kv_proj")
        x = cross_attention(x, rmsnorm(x, cross_norm_g[i]), kv, batch, seq, cross_w_q[i], cross_w_o[i])
        x = swiglu_half_step(x, ffn2_norm_g[i], ffn2_w_gate[i], ffn2_w_up[i], ffn2_w_down[i])
    return rmsnorm(x, final_norm_g, out_dtype=F32).reshape(batch, seq, d)
```

```python
import functools

import jax
import jax.numpy as jnp
from jax import lax
from jax.experimental import pallas as pl
from jax.experimental.pallas import tpu as pltpu

F32 = jnp.float32
BF16 = jnp.bfloat16

NORM_EPS = 1e-6
LN_EPS = 1e-5
ROPE_THETA = 500000.0

GLA_HEADS = 8
GLA_DK = 128
GLA_DV = 256
GLA_K_WIDTH = GLA_HEADS * GLA_DK
GLA_V_WIDTH = GLA_HEADS * GLA_DV
GLA_GATE_RANK = 16
GLA_INV_TAU = 1.0 / 16.0
GLA_CHUNK = 64
GLA_SUB = 16
SG_WIDTH = 2048
SG_GROUPS = 4
SG_GROUP_WIDTH = SG_WIDTH // SG_GROUPS
SG_CHUNK = 128

NSA_HEAD_DIM = 128
NSA_KV_GROUPS = 4
NSA_HPG = 8
NSA_Q_WIDTH = NSA_KV_GROUPS * NSA_HPG * NSA_HEAD_DIM
NSA_KV_WIDTH = NSA_KV_GROUPS * NSA_HEAD_DIM
NSA_CMP_STRIDE = 16
NSA_CMP_LEN = 32
NSA_SEL_BLOCK = 64
NSA_TOPK = 16
NSA_WINDOW = 512
NSA_FORCE_BONUS = 1e4
NSA_SCALE = NSA_HEAD_DIM ** -0.5
ROPE_DIM = NSA_HEAD_DIM // 4
ROPE_HALF = ROPE_DIM // 2
NSA_SLAB = 64
NSA_GROUP_ROWS = 256
LOG2_E = 1.4426950408889634

X_HEADS = 4
X_HEAD_DIM = 128
X_WIDTH = X_HEADS * X_HEAD_DIM
X_SCALE = X_HEAD_DIM ** -0.5

LANES = 128
MASK_NEG = -1e30
VMEM_LIMIT_BYTES = 56 * 1024 * 1024


def _params(*sem):
    return pltpu.CompilerParams(dimension_semantics=sem, vmem_limit_bytes=VMEM_LIMIT_BYTES)


def _dot(a, b):
    return jnp.dot(a, b, preferred_element_type=F32)


def _dot_nt(a, b):
    return lax.dot_general(a, b, (((1,), (1,)), ((), ())), preferred_element_type=F32)


def _dot_tn(a, b):
    return lax.dot_general(a, b, (((0,), (0,)), ((), ())), preferred_element_type=F32)


def _sigmoid(x):
    return 1.0 / (1.0 + jnp.exp(-x))


def _silu(x):
    return x * _sigmoid(x)


def _gelu_tanh(x):
    c = 0.7978845608028654
    return x * (0.5 * (1.0 + jnp.tanh(c * (x + 0.044715 * (x * x * x)))))


def _tile(n, pref):
    if n <= pref:
        return n
    t = pref
    while n % t:
        t //= 2
    return t


def _rmsnorm_kernel(x_ref, g_ref, o_ref):
    x = x_ref[...]
    ms = jnp.mean(x * x, axis=-1, keepdims=True)
    o_ref[...] = (x * lax.rsqrt(ms + NORM_EPS) * g_ref[...]).astype(o_ref.dtype)


def rmsnorm(x, g, out_dtype=BF16):
    m, d = x.shape
    tm = _tile(m, 256)
    return pl.pallas_call(
        _rmsnorm_kernel,
        grid=(m // tm,),
        in_specs=[pl.BlockSpec((tm, d), lambda i: (i, 0)), pl.BlockSpec((1, d), lambda i: (0, 0))],
        out_specs=pl.BlockSpec((tm, d), lambda i: (i, 0)),
        out_shape=jax.ShapeDtypeStruct((m, d), out_dtype),
        compiler_params=_params("parallel"),
        name="rmsnorm",
    )(x, g.reshape(1, d))


def _row_sumsq(x):
    return jnp.broadcast_to(jnp.sum(x * x, axis=-1, keepdims=True), (x.shape[0], LANES))


def _norm_prep_kernel(x_ref, g_ref, xg_ref, ss_ref):
    x = x_ref[...]
    xg_ref[...] = (x * g_ref[...]).astype(BF16)
    ss_ref[...] = _row_sumsq(x)


def norm_prep(x, g):
    m, d = x.shape
    tm = _tile(m, 256)
    return pl.pallas_call(
        _norm_prep_kernel,
        grid=(m // tm,),
        in_specs=[pl.BlockSpec((tm, d), lambda i: (i, 0)), pl.BlockSpec((1, d), lambda i: (0, 0))],
        out_specs=[pl.BlockSpec((tm, d), lambda i: (i, 0)), pl.BlockSpec((tm, LANES), lambda i: (i, 0))],
        out_shape=[jax.ShapeDtypeStruct((m, d), BF16), jax.ShapeDtypeStruct((m, LANES), F32)],
        compiler_params=_params("parallel"),
        name="norm_prep",
    )(x, g.reshape(1, d))


def _scale_rows(acc, ss_ref, d):
    rstd = lax.rsqrt(ss_ref[...] / d + NORM_EPS)
    return acc * jnp.concatenate([rstd] * (acc.shape[1] // LANES), axis=1)


def _proj_kernel(*refs, n_a, normed, residual, scale, next_norm, d_in):
    it = iter(refs)
    a_refs = [next(it) for _ in range(n_a)]
    w_refs = [next(it) for _ in range(n_a)]
    ss_ref = next(it) if normed else None
    r_ref = next(it) if residual else None
    g_ref = next(it) if next_norm else None
    o_ref = next(it)
    acc = _dot(a_refs[0][...], w_refs[0][...])
    for a_ref, w_ref in zip(a_refs[1:], w_refs[1:]):
        acc = acc + _dot(a_ref[...], w_ref[...])
    if normed:
        acc = _scale_rows(acc, ss_ref, d_in)
    if residual:
        acc = r_ref[...] + scale * acc
    o_ref[...] = acc.astype(o_ref.dtype)
    if next_norm:
        xg_ref, ss_out_ref = next(it), next(it)
        xg_ref[...] = (acc * g_ref[...]).astype(BF16)
        part = _row_sumsq(acc)

        @pl.when(pl.program_id(1) == 0)
        def _():
            ss_out_ref[...] = part

        @pl.when(pl.program_id(1) != 0)
        def _():
            ss_out_ref[...] += part


def proj(a, w, layer, *, ss=None, residual=None, scale=1.0, next_g=None, n_cols=None, out_dtype=F32,
         tm=1024, tn=1024, name="proj"):
    a_parts = a if isinstance(a, tuple) else (a,)
    m, kp = a_parts[0].shape
    n = w.shape[2] if n_cols is None else n_cols
    tm = _tile(m, tm)
    tn = _tile(n, tn)
    in_specs = [pl.BlockSpec((tm, kp), lambda i, j: (i, 0)) for _ in a_parts]
    in_specs += [pl.BlockSpec((None, kp, tn), lambda i, j, r=r: (layer, r, j)) for r in range(len(a_parts))]
    args = list(a_parts) + [w] * len(a_parts)
    if ss is not None:
        in_specs.append(pl.BlockSpec((tm, LANES), lambda i, j: (i, 0)))
        args.append(ss)
    if residual is not None:
        in_specs.append(pl.BlockSpec((tm, tn), lambda i, j: (i, j)))
        args.append(residual)
        out_dtype = F32
    out_specs = [pl.BlockSpec((tm, tn), lambda i, j: (i, j))]
    out_shape = [jax.ShapeDtypeStruct((m, n), out_dtype)]
    if next_g is not None:
        in_specs.append(pl.BlockSpec((1, tn), lambda i, j: (0, j)))
        args.append(next_g.reshape(1, n))
        out_specs += [pl.BlockSpec((tm, tn), lambda i, j: (i, j)), pl.BlockSpec((tm, LANES), lambda i, j: (i, 0))]
        out_shape += [jax.ShapeDtypeStruct((m, n), BF16), jax.ShapeDtypeStruct((m, LANES), F32)]
    out = pl.pallas_call(
        functools.partial(_proj_kernel, n_a=len(a_parts), normed=ss is not None, residual=residual is not None,
                          scale=scale, next_norm=next_g is not None, d_in=kp * len(a_parts)),
        grid=(m // tm, n // tn),
        in_specs=in_specs,
        out_specs=out_specs,
        out_shape=out_shape,
        compiler_params=_params("parallel", "arbitrary"),
        name=name,
    )(*args)
    return out if next_g is not None else out[0]


def _ffn_up_kernel(h_ref, wg_ref, wu_ref, ss_ref, o_ref, *, d_in):
    h = h_ref[...]
    g = _scale_rows(_dot(h, wg_ref[...]), ss_ref, d_in)
    u = _scale_rows(_dot(h, wu_ref[...]), ss_ref, d_in)
    o_ref[...] = (_silu(g) * u).astype(o_ref.dtype)


def ffn_up(xg, ss, wg, wu, layer, *, tm=1024, tn=512):
    m, k = xg.shape
    n = wg.shape[2]
    tm = _tile(m, tm)
    tn = _tile(n, tn)
    wspec = pl.BlockSpec((None, k, tn), lambda i, j: (layer, 0, j))
    return pl.pallas_call(
        functools.partial(_ffn_up_kernel, d_in=k),
        grid=(m // tm, n // tn),
        in_specs=[pl.BlockSpec((tm, k), lambda i, j: (i, 0)), wspec, wspec,
                  pl.BlockSpec((tm, LANES), lambda i, j: (i, 0))],
        out_specs=pl.BlockSpec((tm, tn), lambda i, j: (i, j)),
        out_shape=jax.ShapeDtypeStruct((m, n), BF16),
        compiler_params=_params("parallel", "arbitrary"),
        name="ffn_up",
    )(xg, wg, wu, ss)


def swiglu_half_step(x, xg, ss, w_gate, w_up, w_down, layer, next_g):
    act = ffn_up(xg, ss, w_gate, w_up, layer)
    out = proj(act, w_down, layer, residual=x, scale=0.5, next_g=next_g, tn=512, name="ffn_down")
    return out if next_g is not None else (out, None, None)


def _gla_kernel(q_ref, k_ref, v_ref, r_ref, a_ref, wgu_ref, bg_ref, ng_ref, o_ref, st_ref, *, rt):
    C, SB = GLA_CHUNK, GLA_SUB

    @pl.when(pl.program_id(2) == 0)
    def _():
        st_ref[...] = jnp.zeros_like(st_ref)

    zg = _dot(a_ref[...].astype(BF16), wgu_ref[...]) + bg_ref[...]
    lf = (jnp.minimum(zg, 0.0) - jnp.log(1.0 + jnp.exp(-jnp.abs(zg)))) * GLA_INV_TAU

    row = lax.broadcasted_iota(jnp.int32, (rt, GLA_DK), 0)
    rc = row & (C - 1)
    b = lf
    for s in (1, 2, 4, 8, 16, 32):
        b = b + jnp.where(rc >= s, pltpu.roll(b, s, 0), 0.0)

    q = q_ref[...] * (GLA_DK ** -0.5)
    k = k_ref[...]
    v = v_ref[...].astype(BF16)

    nb = rt // SB
    b3 = b.reshape(nb, SB, GLA_DK)
    q3 = q.reshape(nb, SB, GLA_DK)
    k3 = k.reshape(nb, SB, GLA_DK)
    rowc = lax.broadcasted_iota(jnp.int32, (rt, C), 0)
    lane = lax.broadcasted_iota(jnp.int32, (rt, C), 1)
    sub_row = rowc & (SB - 1)
    sub_base = (rowc & (C - 1)) - sub_row
    a_diag = jnp.zeros((rt, C), F32)
    for j in range(SB):
        bj = jnp.broadcast_to(b3[:, j:j + 1, :], (nb, SB, GLA_DK))
        kj = jnp.broadcast_to(k3[:, j:j + 1, :], (nb, SB, GLA_DK))
        pj = (q3 * kj * jnp.exp(jnp.minimum(b3 - bj, 0.0))).reshape(rt, GLA_DK)
        rs = jnp.sum(pj, axis=-1, keepdims=True)
        a_diag = jnp.where((lane == sub_base + j) & (sub_row >= j), rs, a_diag)

    lane_c = lax.broadcasted_iota(jnp.int32, (SB, C), 1)
    outs = []
    for c in range(rt // C):
        sl = slice(c * C, (c + 1) * C)
        bc, qc, kc, vc = b[sl], q[sl], k[sl], v[sl]
        rows = [jnp.zeros((SB, C), F32)]
        for i in range(1, C // SB):
            b0 = bc[i * SB:i * SB + 1]
            qi = qc[i * SB:(i + 1) * SB] * jnp.exp(bc[i * SB:(i + 1) * SB] - b0)
            ki = kc * jnp.exp(jnp.minimum(b0 - bc, 0.0))
            ai = _dot_nt(qi.astype(BF16), ki.astype(BF16))
            rows.append(jnp.where(lane_c < i * SB, ai, 0.0))
        a_c = a_diag[sl] + jnp.concatenate(rows, axis=0)
        o_intra = _dot(a_c.astype(BF16), vc)
        st = st_ref[...]
        o_inter = _dot_nt((qc * jnp.exp(bc)).astype(BF16), st.astype(BF16))
        bl = bc[C - 1:C]
        kd = kc * jnp.exp(bl - bc)
        st_ref[...] = st * jnp.exp(bl) + _dot_tn(vc, kd.astype(BF16))
        outs.append(o_inter + o_intra)
    o = jnp.concatenate(outs, axis=0)

    ms = jnp.mean(o * o, axis=-1, keepdims=True)
    y = o * lax.rsqrt(ms + NORM_EPS) * ng_ref[...]
    o_ref[...] = (y * _silu(r_ref[...])).astype(o_ref.dtype)


def gla_mixer(z, za, w_gate_up, b_gate, norm_g, batch, seq, *, rt=512):
    m = z.shape[0]
    rt = _tile(seq, rt)
    nt = seq // rt
    kb = GLA_K_WIDTH // GLA_DK
    vb = 2 * GLA_K_WIDTH // GLA_DV
    rb = vb + GLA_V_WIDTH // GLA_DV
    wgu = jnp.zeros((LANES, GLA_K_WIDTH), BF16).at[:GLA_GATE_RANK].set(w_gate_up.astype(BF16))
    row = lambda b, h, t: b * nt + t
    return pl.pallas_call(
        functools.partial(_gla_kernel, rt=rt),
        grid=(batch, GLA_HEADS, nt),
        in_specs=[
            pl.BlockSpec((rt, GLA_DK), lambda b, h, t: (row(b, h, t), h)),
            pl.BlockSpec((rt, GLA_DK), lambda b, h, t: (row(b, h, t), kb + h)),
            pl.BlockSpec((rt, GLA_DV), lambda b, h, t: (row(b, h, t), vb + h)),
            pl.BlockSpec((rt, GLA_DV), lambda b, h, t: (row(b, h, t), rb + h)),
            pl.BlockSpec((rt, LANES), lambda b, h, t: (row(b, h, t), 0)),
            pl.BlockSpec((LANES, GLA_DK), lambda b, h, t: (0, h)),
            pl.BlockSpec((1, GLA_DK), lambda b, h, t: (0, h)),
            pl.BlockSpec((1, GLA_DV), lambda b, h, t: (0, 0)),
        ],
        out_specs=pl.BlockSpec((rt, GLA_DV), lambda b, h, t: (row(b, h, t), h)),
        out_shape=jax.ShapeDtypeStruct((m, GLA_V_WIDTH), BF16),
        scratch_shapes=[pltpu.VMEM((GLA_DV, GLA_DK), F32)],
        compiler_params=_params("parallel", "parallel", "arbitrary"),
        name="gla",
    )(z, z, z, z, za, wgu, b_gate.reshape(1, GLA_K_WIDTH), norm_g.reshape(1, GLA_DV))


def _sg_kernel(u_ref, s_ref, lng_ref, lnb_ref, ws_ref, bs_ref, o_ref, *, rt):
    s = _gelu_tanh(s_ref[...])
    mu = jnp.mean(s, axis=-1, keepdims=True)
    sc = s - mu
    var = jnp.mean(sc * sc, axis=-1, keepdims=True)
    sn = (sc * lax.rsqrt(var + LN_EPS) * lng_ref[...] + lnb_ref[...]).astype(BF16)
    u = _gelu_tanh(u_ref[...])
    ri = lax.broadcasted_iota(jnp.int32, (SG_CHUNK, SG_CHUNK), 0)
    ci = lax.broadcasted_iota(jnp.int32, (SG_CHUNK, SG_CHUNK), 1)
    for g in range(SG_GROUPS):
        w = jnp.where(ci <= ri, ws_ref[g], 0.0).astype(BF16)
        cols = slice(g * SG_GROUP_WIDTH, (g + 1) * SG_GROUP_WIDTH)
        for c in range(rt // SG_CHUNK):
            rows = slice(c * SG_CHUNK, (c + 1) * SG_CHUNK)
            mixed = _dot(w, sn[rows, cols]) + bs_ref[g]
            o_ref[rows, cols] = (u[rows, cols] * mixed).astype(o_ref.dtype)


def sg_mixer(z, ln_g, ln_b, w_s, b_s, *, rt=512):
    m = z.shape[0]
    rt = _tile(m, rt)
    return pl.pallas_call(
        functools.partial(_sg_kernel, rt=rt),
        grid=(m // rt,),
        in_specs=[
            pl.BlockSpec((rt, SG_WIDTH), lambda i: (i, 0)),
            pl.BlockSpec((rt, SG_WIDTH), lambda i: (i, 1)),
            pl.BlockSpec((1, SG_WIDTH), lambda i: (0, 0)),
            pl.BlockSpec((1, SG_WIDTH), lambda i: (0, 0)),
            pl.BlockSpec((SG_GROUPS, SG_CHUNK, SG_CHUNK), lambda i: (0, 0, 0)),
            pl.BlockSpec((SG_GROUPS, SG_CHUNK, 1), lambda i: (0, 0, 0)),
        ],
        out_specs=pl.BlockSpec((rt, SG_WIDTH), lambda i: (i, 0)),
        out_shape=jax.ShapeDtypeStruct((m, SG_WIDTH), BF16),
        compiler_params=_params("parallel"),
        name="spatial_gate",
    )(z, z, ln_g.reshape(1, SG_WIDTH), ln_b.reshape(1, SG_WIDTH), w_s, b_s.reshape(SG_GROUPS, SG_CHUNK, 1))


def split_ab_w_in(w_in):
    layers, d, _ = w_in.shape
    a0 = 2 * GLA_K_WIDTH + 2 * GLA_V_WIDTH
    w_a = jnp.concatenate(
        [w_in[:, :, a0:a0 + GLA_GATE_RANK], jnp.zeros((layers, d, LANES - GLA_GATE_RANK), BF16)], axis=2)
    return w_in, w_a, w_in[:, :, a0 + GLA_GATE_RANK:]


def mixer_gla_sg(x, xg, ss, layer, batch, seq, w_in_parts, w_gate_up, b_gate, gla_norm_g,
                 sg_ln_g, sg_ln_b, sg_w_s, sg_b_s, w_out, next_g):
    w, w_a, w_us = w_in_parts
    z = proj(xg, w, layer, ss=ss, n_cols=2 * GLA_K_WIDTH + 2 * GLA_V_WIDTH, name="ab_in_proj")
    za = proj(xg, w_a, layer, ss=ss, name="ab_gate_proj")
    zus = proj(xg, w_us, layer, ss=ss, name="ab_sg_proj")
    y_gla = gla_mixer(z, za, w_gate_up, b_gate, gla_norm_g, batch, seq)
    y_sg = sg_mixer(zus, sg_ln_g, sg_ln_b, sg_w_s, sg_b_s)
    return proj((y_gla, y_sg), w_out, layer, residual=x, next_g=next_g, tn=512, name="ab_out_proj")


def _cross_kernel(q_ref, k_ref, v_ref, o_ref):
    for hd in range(X_HEADS):
        cols = slice(hd * X_HEAD_DIM, (hd + 1) * X_HEAD_DIM)
        s = _dot_nt(q_ref[:, cols], k_ref[:, cols]) * X_SCALE
        p = jnp.exp(s - jnp.max(s, axis=-1, keepdims=True))
        p = p / jnp.sum(p, axis=-1, keepdims=True)
        o_ref[:, cols] = _dot(p.astype(BF16), v_ref[:, cols]).astype(o_ref.dtype)


def cross_attention(x, xg, ss, kv, batch, seq, w_q, w_o, layer, next_g, *, tq=1024):
    m = xg.shape[0]
    mem = kv.shape[0] // batch
    tq = _tile(seq, tq)
    nt = seq // tq
    q = proj(xg, w_q, layer, ss=ss, out_dtype=BF16, name="cross_q_proj")
    o = pl.pallas_call(
        _cross_kernel,
        grid=(batch, nt),
        in_specs=[
            pl.BlockSpec((tq, X_WIDTH), lambda b, t: (b * nt + t, 0)),
            pl.BlockSpec((mem, X_WIDTH), lambda b, t: (b, 0)),
            pl.BlockSpec((mem, X_WIDTH), lambda b, t: (b, 1)),
        ],
        out_specs=pl.BlockSpec((tq, X_WIDTH), lambda b, t: (b * nt + t, 0)),
        out_shape=jax.ShapeDtypeStruct((m, X_WIDTH), BF16),
        compiler_params=_params("parallel", "parallel"),
        name="cross_attn",
    )(q, kv, kv)
    return proj(o, w_o, layer, residual=x, next_g=next_g, tn=512, name="cross_o_proj")


def _rope_tables(seq):
    inv_freq = jnp.power(ROPE_THETA, -jnp.arange(ROPE_HALF, dtype=F32) / ROPE_HALF)
    ang = jnp.arange(seq).astype(F32)[:, None] * inv_freq[None, :]
    cos, sin = jnp.cos(ang), jnp.sin(ang)
    rest = NSA_HEAD_DIM - ROPE_DIM
    c = jnp.concatenate([cos, cos, jnp.ones((seq, rest), F32)], axis=1)
    s_up = jnp.concatenate([jnp.zeros_like(sin), sin, jnp.zeros((seq, rest), F32)], axis=1)
    s_dn = jnp.concatenate([-sin, jnp.zeros_like(sin), jnp.zeros((seq, rest), F32)], axis=1)
    return c, s_up, s_dn


def _rope(x, c, s_up, s_dn, heads):
    width = heads * NSA_HEAD_DIM
    rep = lambda t: jnp.concatenate([t] * heads, axis=1)
    return (x * rep(c) + pltpu.roll(x, ROPE_HALF, 1) * rep(s_up)
            + pltpu.roll(x, width - ROPE_HALF, 1) * rep(s_dn))


def _nsa_kv_kernel(ks_ref, vs_ref, kw_ref, vw_ref, c_ref, su_ref, sd_ref, oks_ref, ovs_ref, okw_ref, ovw_ref):
    c, su, sd = c_ref[...], su_ref[...], sd_ref[...]
    oks_ref[...] = _rope(ks_ref[...], c, su, sd, NSA_KV_GROUPS).astype(BF16)
    okw_ref[...] = _rope(kw_ref[...], c, su, sd, NSA_KV_GROUPS).astype(BF16)
    ones = jnp.ones((ks_ref.shape[0], NSA_HEAD_DIM), BF16)
    for v_ref, o_ref in ((vs_ref, ovs_ref), (vw_ref, ovw_ref)):
        v = v_ref[...].astype(BF16)
        o_ref[...] = jnp.concatenate(
            [piece for g in range(NSA_KV_GROUPS) for piece in (v[:, g * NSA_HEAD_DIM:(g + 1) * NSA_HEAD_DIM], ones)],
            axis=1)


def nsa_kv_prep(z, tables, batch, seq, *, rt=1024):
    m = z.shape[0]
    rt = _tile(seq, rt)
    nt = seq // rt
    cb = NSA_Q_WIDTH // NSA_KV_WIDTH + 2
    zspec = lambda j: pl.BlockSpec((rt, NSA_KV_WIDTH), lambda b, t: (b * nt + t, cb + j))
    tspec = pl.BlockSpec((rt, NSA_HEAD_DIM), lambda b, t: (t, 0))
    kspec = pl.BlockSpec((rt, NSA_KV_WIDTH), lambda b, t: (b * nt + t, 0))
    vspec = pl.BlockSpec((rt, 2 * NSA_KV_WIDTH), lambda b, t: (b * nt + t, 0))
    ksds = jax.ShapeDtypeStruct((m, NSA_KV_WIDTH), BF16)
    vsds = jax.ShapeDtypeStruct((m, 2 * NSA_KV_WIDTH), BF16)
    return pl.pallas_call(
        _nsa_kv_kernel,
        grid=(batch, nt),
        in_specs=[zspec(0), zspec(1), zspec(2), zspec(3), tspec, tspec, tspec],
        out_specs=[kspec, vspec, kspec, vspec],
        out_shape=[ksds, vsds, ksds, vsds],
        compiler_params=_params("parallel", "parallel"),
        name="nsa_kv_prep",
    )(z, z, z, z, *tables)


def _nsa_cmp_kernel(x_ref, pos_ref, w1_ref, b1_ref, w2_ref, b2_ref, o_ref, *, nblk):
    half = NSA_CMP_STRIDE * NSA_HEAD_DIM
    pos_bias = _dot(pos_ref[...], w1_ref[...])[0:1] + b1_ref[...]
    row = lax.broadcasted_iota(jnp.int32, (nblk, NSA_HEAD_DIM), 0)
    for g in range(NSA_KV_GROUPS):
        xg = jnp.concatenate(
            [x_ref[:, l * NSA_KV_WIDTH + g * NSA_HEAD_DIM:l * NSA_KV_WIDTH + (g + 1) * NSA_HEAD_DIM]
             for l in range(NSA_CMP_STRIDE)], axis=1).astype(BF16)
        first = _dot(xg, w1_ref[:half])
        second = _dot(xg, w1_ref[half:])
        pre = first + pltpu.roll(second, nblk - 1, 0) + pos_bias
        out = _dot(_silu(pre).astype(BF16), w2_ref[...]) + b2_ref[...]
        o_ref[g] = jnp.where(row < nblk - 1, out, 0.0).astype(o_ref.dtype)


def nsa_compress(z, batch, seq, cmp_pos, cmp_w1, cmp_b1, cmp_w2, cmp_b2):
    nblk = seq // NSA_CMP_STRIDE
    c0 = NSA_Q_WIDTH
    xr = jnp.stack([z[:, c0:c0 + NSA_KV_WIDTH], z[:, c0 + NSA_KV_WIDTH:c0 + 2 * NSA_KV_WIDTH]])
    xr = xr.reshape(2, batch, nblk, NSA_CMP_STRIDE * NSA_KV_WIDTH)
    flat = NSA_CMP_LEN * NSA_HEAD_DIM
    pos = jnp.zeros((2, 8, flat), BF16).at[:, 0].set(cmp_pos.reshape(2, flat).astype(BF16))
    return pl.pallas_call(
        functools.partial(_nsa_cmp_kernel, nblk=nblk),
        grid=(2, batch),
        in_specs=[
            pl.BlockSpec((None, None, nblk, NSA_CMP_STRIDE * NSA_KV_WIDTH), lambda s, b: (s, b, 0, 0)),
            pl.BlockSpec((None, 8, flat), lambda s, b: (s, 0, 0)),
            pl.BlockSpec((None, flat, NSA_HEAD_DIM), lambda s, b: (s, 0, 0)),
            pl.BlockSpec((None, 1, NSA_HEAD_DIM), lambda s, b: (s, 0, 0)),
            pl.BlockSpec((None, NSA_HEAD_DIM, NSA_HEAD_DIM), lambda s, b: (s, 0, 0)),
            pl.BlockSpec((None, 1, NSA_HEAD_DIM), lambda s, b: (s, 0, 0)),
        ],
        out_specs=pl.BlockSpec((None, None, NSA_KV_GROUPS, nblk, NSA_HEAD_DIM), lambda s, b: (s, b, 0, 0, 0)),
        out_shape=jax.ShapeDtypeStruct((2, batch, NSA_KV_GROUPS, nblk, NSA_HEAD_DIM), BF16),
        compiler_params=_params("parallel", "parallel"),
        name="nsa_compress",
    )(xr, pos, cmp_w1.astype(BF16), cmp_b1.reshape(2, 1, NSA_HEAD_DIM), cmp_w2.astype(BF16),
      cmp_b2.reshape(2, 1, NSA_HEAD_DIM))


def _row_reduce(s, op, lane_op):
    acc = s[:, :LANES]
    for c in range(1, s.shape[1] // LANES):
        acc = op(acc, s[:, c * LANES:(c + 1) * LANES])
    return lane_op(acc, axis=-1, keepdims=True)


def _nsa_attn_kernel(q_ref, gate_ref, c_ref, su_ref, sd_ref, kc_ref, vc_ref, ks_ref, vs_ref, kw_ref, vw_ref,
                     ovl_ref, blk_ref, spread_ref, o_ref,
                     qp_ref, qr_ref, s_ref, p_ref, bias_ref, m_ref, acc_ref, psum_ref, sel_ref, mix_ref, gx_ref,
                     *, tq, tk, seq, top_n):
    P, HD, SB, W, SL = NSA_HPG, NSA_HEAD_DIM, NSA_SEL_BLOCK, NSA_WINDOW, NSA_SLAB
    R = P * tq
    nsel = seq // SB
    ncmp = seq // NSA_CMP_STRIDE
    t0 = pl.program_id(2) * tq

    qf = q_ref[...]
    q_scale = NSA_SCALE * LOG2_E
    qr = _rope(qf, c_ref[...], su_ref[...], sd_ref[...], P)
    for p in range(P):
        qp_ref[p * tq:(p + 1) * tq] = (qf[:, p * HD:(p + 1) * HD] * q_scale).astype(BF16)
        qr_ref[p * tq:(p + 1) * tq] = (qr[:, p * HD:(p + 1) * HD] * q_scale).astype(BF16)

    def reset_state():
        m_ref[...] = jnp.full_like(m_ref, MASK_NEG)
        acc_ref[...] = jnp.zeros_like(acc_ref)

    def softmax_tile(k_t, v_t, n):
        for g0 in range(0, R, NSA_GROUP_ROWS):
            grp = slice(g0, g0 + NSA_GROUP_ROWS)
            s_grp = _dot_nt(qr_ref[grp], k_t)
            alphas, probs = [], []
            for r0 in range(g0, g0 + NSA_GROUP_ROWS, SL):
                rows = slice(r0, r0 + SL)
                b0 = r0 % tq
                s = s_grp[r0 - g0:r0 - g0 + SL] + bias_ref[b0:b0 + SL, :n]
                m_old = m_ref[rows]
                m_new = jnp.maximum(m_old, _row_reduce(s, jnp.maximum, jnp.max))
                alphas.append(jnp.exp2(m_old - m_new))
                m_ref[rows] = m_new
                probs.append(jnp.exp2(s - m_new).astype(BF16))
            alpha = jnp.concatenate(alphas, axis=0)
            acc_ref[grp] = alpha * acc_ref[grp] + _dot(jnp.concatenate(probs, axis=0), v_t)

    t_c = t0 + lax.broadcasted_iota(jnp.int32, (tq, ncmp), 0)
    end_c = lax.broadcasted_iota(jnp.int32, (tq, ncmp), 1) * NSA_CMP_STRIDE + (NSA_CMP_LEN - 1)
    bias_ref[:, :ncmp] = jnp.where(end_c <= t_c, 0.0, MASK_NEG)
    s_ref[:, :ncmp] = _dot_nt(qp_ref[...], kc_ref[...])
    psum_ref[...] = jnp.zeros_like(psum_ref)
    for j in range(R // SL):
        rows = slice(j * SL, (j + 1) * SL)
        b0 = (j * SL) % tq
        bias = bias_ref[b0:b0 + SL, :ncmp]
        s = s_ref[rows, :ncmp] + bias
        e = jnp.where(bias == 0.0, jnp.exp2(s - _row_reduce(s, jnp.maximum, jnp.max)), 0.0)
        pn = e / jnp.maximum(_row_reduce(e, jnp.add, jnp.sum), 1e-30)
        p_ref[rows, :ncmp] = pn.astype(BF16)
        psum_ref[b0:b0 + SL] += pn
    o_c = _dot(p_ref[:, :ncmp], vc_ref[...])
    gates = _sigmoid(gate_ref[...])
    g_hi = gates.astype(BF16)
    g_lo = (gates - g_hi.astype(F32)).astype(BF16)
    gx_ref[...] = _dot(g_hi, spread_ref[...]) + _dot(g_lo, spread_ref[...])
    gate = lambda p, j: gx_ref[:, (3 * p + j) * LANES:(3 * p + j + 1) * LANES]
    for p in range(P):
        rows = slice(p * tq, (p + 1) * tq)
        mix_ref[rows] = gate(p, 0) * o_c[rows]

    p_sum = psum_ref[...]
    p_hi = p_sum.astype(BF16)
    p_lo = (p_sum - p_hi.astype(F32)).astype(BF16)
    imp_t = _dot_nt(ovl_ref[...], p_hi) + _dot_nt(ovl_ref[...], p_lo)

    blk = lax.broadcasted_iota(jnp.int32, (nsel, tq), 0)
    cur = (t0 + lax.broadcasted_iota(jnp.int32, (nsel, tq), 1)) // SB
    forced = (blk == 0) | (blk == cur) | (blk == cur - 1)
    score = jnp.where(blk <= cur, imp_t + jnp.where(forced, NSA_FORCE_BONUS, 0.0), -jnp.inf)
    rank = jnp.zeros((nsel, tq), jnp.int32)
    for sp in range(nsel):
        other = jnp.broadcast_to(score[sp:sp + 1, :], (nsel, tq))
        rank = rank + jnp.where(blk > sp, jnp.where(other >= score, 1, 0), jnp.where(other > score, 1, 0))
    sel_ref[...] = jnp.where((rank < top_n) & (blk <= cur), 1.0, 0.0).T.astype(BF16)

    wlen = W + tq
    base = pl.multiple_of(jnp.clip(t0 - W, 0, seq - wlen), tq)
    delta = (t0 + lax.broadcasted_iota(jnp.int32, (tq, wlen), 0)) - (base + lax.broadcasted_iota(jnp.int32, (tq, wlen), 1))
    bias_ref[:, :wlen] = jnp.where((delta >= 0) & (delta < W), 0.0, MASK_NEG)
    reset_state()
    softmax_tile(kw_ref[pl.ds(base, wlen), :], vw_ref[pl.ds(base, wlen), :], wlen)
    for p in range(P):
        rows = slice(p * tq, (p + 1) * tq)
        o_w = acc_ref[rows, :HD] / acc_ref[rows, HD:]
        mix_ref[rows] += gate(p, 2) * o_w

    t_k = t0 + lax.broadcasted_iota(jnp.int32, (tq, tk), 0)
    lane_k = lax.broadcasted_iota(jnp.int32, (tq, tk), 1)
    reset_state()

    def sel_step(kt, carry):
        k0 = pl.multiple_of(kt * tk, tk)
        picked = _dot(sel_ref[...], blk_ref[kt])
        bias_ref[:, :tk] = jnp.where((picked > 0.5) & (lane_k + k0 <= t_k), 0.0, MASK_NEG)
        softmax_tile(ks_ref[pl.ds(k0, tk), :], vs_ref[pl.ds(k0, tk), :], tk)
        return carry

    lax.fori_loop(0, (t0 + tq + tk - 1) // tk, sel_step, 0)
    for p in range(P):
        rows = slice(p * tq, (p + 1) * tq)
        o_s = acc_ref[rows, :HD] / acc_ref[rows, HD:]
        o_ref[:, p * HD:(p + 1) * HD] = (mix_ref[rows] + gate(p, 1) * o_s).astype(o_ref.dtype)


def nsa_attention(z, zg, kv, cmp, tables, batch, seq, *, tq=256, tk=1024):
    m = z.shape[0]
    ks, vs, kw, vw = kv
    tq = _tile(seq, tq)
    tk = _tile(seq, tk)
    nt = seq // tq
    nsel = seq // NSA_SEL_BLOCK
    ncmp = seq // NSA_CMP_STRIDE
    gw = NSA_HPG * NSA_HEAD_DIM
    c_start = jnp.arange(ncmp) * NSA_CMP_STRIDE
    s_start = jnp.arange(nsel) * NSA_SEL_BLOCK
    ovl = ((c_start[None, :] <= s_start[:, None] + NSA_SEL_BLOCK - 1)
           & (c_start[None, :] + NSA_CMP_LEN - 1 >= s_start[:, None])
           & (jnp.arange(ncmp)[None, :] < ncmp - 1)).astype(BF16)
    blk_of_key = (jnp.arange(seq)[None, :] // NSA_SEL_BLOCK == jnp.arange(nsel)[:, None]).astype(BF16)
    blk_of_key = blk_of_key.reshape(nsel, seq // tk, tk).transpose(1, 0, 2)
    n_gate = 3 * NSA_HPG
    spread = (jnp.arange(LANES)[:, None] == jnp.arange(n_gate * LANES)[None, :] // LANES).astype(BF16)
    kspec =pl.BlockSpec((seq, NSA_HEAD_DIM), lambda b, g, i: (b, g))
    vspec = pl.BlockSpec((seq, 2 * NSA_HEAD_DIM), lambda b, g, i: (b, g))
    cmpspec = lambda s: pl.BlockSpec((None, None, None, ncmp, NSA_HEAD_DIM), lambda b, g, i: (s, b, g, 0, 0))
    tspec = pl.BlockSpec((tq, NSA_HEAD_DIM), lambda b, g, i: (i, 0))
    rows = NSA_HPG * tq
    width = max(tk, NSA_WINDOW + tq, ncmp)
    return pl.pallas_call(
        functools.partial(_nsa_attn_kernel, tq=tq, tk=tk, seq=seq, top_n=min(NSA_TOPK, nsel)),
        grid=(batch, NSA_KV_GROUPS, nt),
        in_specs=[
            pl.BlockSpec((tq, gw), lambda b, g, i: (b * nt + i, g)),
            pl.BlockSpec((tq, LANES), lambda b, g, i: (b * nt + i, g)),
            tspec, tspec, tspec,
            cmpspec(0), cmpspec(1),
            kspec, vspec, kspec, vspec,
            pl.BlockSpec((nsel, ncmp), lambda b, g, i: (0, 0)),
            pl.BlockSpec((seq // tk, nsel, tk), lambda b, g, i: (0, 0, 0)),
            pl.BlockSpec((LANES, n_gate * LANES), lambda b, g, i: (0, 0)),
        ],
        out_specs=pl.BlockSpec((tq, gw), lambda b, g, i: (b * nt + i, g)),
        out_shape=jax.ShapeDtypeStruct((m, NSA_Q_WIDTH), BF16),
        scratch_shapes=[
            pltpu.VMEM((rows, NSA_HEAD_DIM), BF16),
            pltpu.VMEM((rows, NSA_HEAD_DIM), BF16),
            pltpu.VMEM((rows, width), F32),
            pltpu.VMEM((rows, width), BF16),
            pltpu.VMEM((tq, width), F32),
            pltpu.VMEM((rows, 1), F32),
            pltpu.VMEM((rows, 2 * NSA_HEAD_DIM), F32),
            pltpu.VMEM((tq, ncmp), F32),
            pltpu.VMEM((tq, nsel), BF16),
            pltpu.VMEM((rows, NSA_HEAD_DIM), F32),
            pltpu.VMEM((tq, n_gate * LANES), F32),
        ],
        compiler_params=_params("parallel", "parallel", "arbitrary"),
        name="nsa_attn",
    )(z, zg, *tables, cmp, cmp, ks, vs, kw, vw, ovl, blk_of_key, spread)


def split_nsa_w_in(w_in):
    layers, d, _ = w_in.shape
    c0 = NSA_Q_WIDTH + 6 * NSA_KV_WIDTH
    per_group = 3 * NSA_HPG
    cols = []
    for g in range(NSA_KV_GROUPS):
        cols += [w_in[:, :, c0 + g * per_group:c0 + (g + 1) * per_group],
                 jnp.zeros((layers, d, LANES - per_group), BF16)]
    return w_in, jnp.concatenate(cols, axis=2)


def mixer_nsa(x, xg, ss, layer, batch, seq, w_in_parts, cmp_pos, cmp_w1, cmp_b1, cmp_w2, cmp_b2, w_out, next_g):
    w, w_gates = w_in_parts
    z = proj(xg, w, layer, ss=ss, n_cols=NSA_Q_WIDTH + 6 * NSA_KV_WIDTH, name="nsa_in_proj")
    zg = proj(xg, w_gates, layer, ss=ss, name="nsa_gate_proj")
    tables = _rope_tables(seq)
    kv = nsa_kv_prep(z, tables, batch, seq)
    cmp = nsa_compress(z, batch, seq, cmp_pos, cmp_w1, cmp_b1, cmp_w2, cmp_b2)
    o = nsa_attention(z, zg, kv, cmp, tables, batch, seq)
    return proj(o, w_out, layer, residual=x, next_g=next_g, tn=512, name="nsa_out_proj")


def kernel(x, mem, mem_norm_g,
           ffn1_norm_g, ffn1_w_gate, ffn1_w_up, ffn1_w_down,
           mix_norm_g,
           ab_w_in, gla_w_gate_up, gla_b_gate, gla_norm_g, sg_ln_g, sg_ln_b, sg_w_s, sg_b_s, ab_w_out,
           nsa_w_in, nsa_cmp_pos, nsa_cmp_w1, nsa_cmp_b1, nsa_cmp_w2, nsa_cmp_b2, nsa_w_out,
           cross_norm_g, cross_w_q, cross_w_kv, cross_w_o,
           ffn2_norm_g, ffn2_w_gate, ffn2_w_up, ffn2_w_down,
           final_norm_g):
    batch, seq, d = x.shape
    depth = ffn1_norm_g.shape[0]
    bf = lambda w: w.astype(BF16)
    ffn1_w, ffn2_w = [tuple(map(bf, ws)) for ws in ((ffn1_w_gate, ffn1_w_up, ffn1_w_down),
                                                     (ffn2_w_gate, ffn2_w_up, ffn2_w_down))]
    ab_w_in_parts, ab_w_out = split_ab_w_in(bf(ab_w_in)), bf(ab_w_out)
    nsa_w_in_parts, nsa_w_out = split_nsa_w_in(bf(nsa_w_in)), bf(nsa_w_out)
    cross_w_q, cross_w_kv, cross_w_o = bf(cross_w_q), bf(cross_w_kv), bf(cross_w_o)

    mem_n = rmsnorm(mem.reshape(-1, d), mem_norm_g)
    x = x.reshape(batch * seq, d)
    xg, ss = norm_prep(x, ffn1_norm_g[0])
    for i in range(depth):
        x, xg, ss = swiglu_half_step(x, xg, ss, *ffn1_w, i, mix_norm_g[i])
        j = i // 2
        if i % 2 == 0:
            x, xg, ss = mixer_gla_sg(x, xg, ss, j, batch, seq, ab_w_in_parts, gla_w_gate_up[j], gla_b_gate[j],
                                     gla_norm_g[j], sg_ln_g[j], sg_ln_b[j], sg_w_s[j], sg_b_s[j], ab_w_out,
                                     cross_norm_g[i])
        else:
            x, xg, ss = mixer_nsa(x, xg, ss, j, batch, seq, nsa_w_in_parts, nsa_cmp_pos[j], nsa_cmp_w1[j],
                                  nsa_cmp_b1[j], nsa_cmp_w2[j], nsa_cmp_b2[j], nsa_w_out, cross_norm_g[i])
        kv = proj(mem_n, cross_w_kv, i, out_dtype=BF16, name="cross_kv_proj")
        x, xg, ss = cross_attention(x, xg, ss, kv, batch, seq, cross_w_q, cross_w_o, i, ffn2_norm_g[i])
        x, xg, ss = swiglu_half_step(x, xg, ss, *ffn2_w, i, ffn1_norm_g[i + 1] if i + 1 < depth else None)
    return rmsnorm(x, final_norm_g, out_dtype=F32).reshape(batch, seq, d)
```

```python
import functools

import jax
import jax.numpy as jnp
from jax import lax
from jax.experimental import pallas as pl
from jax.experimental.pallas import tpu as pltpu

F32 = jnp.float32
BF16 = jnp.bfloat16

NORM_EPS = 1e-6
LN_EPS = 1e-5
ROPE_THETA = 500000.0

GLA_HEADS = 8
GLA_DK = 128
GLA_DV = 256
GLA_K_WIDTH = GLA_HEADS * GLA_DK
GLA_V_WIDTH = GLA_HEADS * GLA_DV
GLA_GATE_RANK = 16
GLA_INV_TAU = 1.0 / 16.0
GLA_CHUNK = 64
GLA_SUB = 16
SG_WIDTH = 2048
SG_GROUPS = 4
SG_GROUP_WIDTH = SG_WIDTH // SG_GROUPS
SG_CHUNK = 128

NSA_HEAD_DIM = 128
NSA_KV_GROUPS = 4
NSA_HPG = 8
NSA_Q_WIDTH = NSA_KV_GROUPS * NSA_HPG * NSA_HEAD_DIM
NSA_KV_WIDTH = NSA_KV_GROUPS * NSA_HEAD_DIM
NSA_CMP_STRIDE = 16
NSA_CMP_LEN = 32
NSA_SEL_BLOCK = 64
NSA_TOPK = 16
NSA_WINDOW = 512
NSA_FORCE_BONUS = 1e4
NSA_SCALE = NSA_HEAD_DIM ** -0.5
ROPE_DIM = NSA_HEAD_DIM // 4
ROPE_HALF = ROPE_DIM // 2
NSA_SLAB = 64
NSA_GROUP_ROWS = 256
LOG2_E = 1.4426950408889634

X_HEADS = 4
X_HEAD_DIM = 128
X_WIDTH = X_HEADS * X_HEAD_DIM
X_SCALE = X_HEAD_DIM ** -0.5

LANES = 128
MASK_NEG = -1e30
VMEM_LIMIT_BYTES = 56 * 1024 * 1024


def _params(*sem):
    return pltpu.CompilerParams(dimension_semantics=sem, vmem_limit_bytes=VMEM_LIMIT_BYTES)


def _dot(a, b):
    return jnp.dot(a, b, preferred_element_type=F32)


def _dot_nt(a, b):
    return lax.dot_general(a, b, (((1,), (1,)), ((), ())), preferred_element_type=F32)


def _dot_tn(a, b):
    return lax.dot_general(a, b, (((0,), (0,)), ((), ())), preferred_element_type=F32)


def _sigmoid(x):
    return 1.0 / (1.0 + jnp.exp(-x))


def _silu(x):
    return x * _sigmoid(x)


def _gelu_tanh(x):
    c = 0.7978845608028654
    return x * (0.5 * (1.0 + jnp.tanh(c * (x + 0.044715 * (x * x * x)))))


def _tile(n, pref):
    if n <= pref:
        return n
    t = pref
    while n % t:
        t //= 2
    return t


def _rmsnorm_kernel(x_ref, g_ref, o_ref):
    x = x_ref[...]
    ms = jnp.mean(x * x, axis=-1, keepdims=True)
    o_ref[...] = (x * lax.rsqrt(ms + NORM_EPS) * g_ref[...]).astype(o_ref.dtype)


def rmsnorm(x, g, out_dtype=BF16):
    m, d = x.shape
    tm = _tile(m, 256)
    return pl.pallas_call(
        _rmsnorm_kernel,
        grid=(m // tm,),
        in_specs=[pl.BlockSpec((tm, d), lambda i: (i, 0)), pl.BlockSpec((1, d), lambda i: (0, 0))],
        out_specs=pl.BlockSpec((tm, d), lambda i: (i, 0)),
        out_shape=jax.ShapeDtypeStruct((m, d), out_dtype),
        compiler_params=_params("parallel"),
        name="rmsnorm",
    )(x, g.reshape(1, d))


def _row_sumsq(x):
    return jnp.broadcast_to(jnp.sum(x * x, axis=-1, keepdims=True), (x.shape[0], LANES))


def _norm_prep_kernel(x_ref, g_ref, xg_ref, ss_ref):
    x = x_ref[...]
    xg_ref[...] = (x * g_ref[...]).astype(BF16)
    ss_ref[...] = _row_sumsq(x)


def norm_prep(x, g):
    m, d = x.shape
    tm = _tile(m, 256)
    return pl.pallas_call(
        _norm_prep_kernel,
        grid=(m // tm,),
        in_specs=[pl.BlockSpec((tm, d), lambda i: (i, 0)), pl.BlockSpec((1, d), lambda i: (0, 0))],
        out_specs=[pl.BlockSpec((tm, d), lambda i: (i, 0)), pl.BlockSpec((tm, LANES), lambda i: (i, 0))],
        out_shape=[jax.ShapeDtypeStruct((m, d), BF16), jax.ShapeDtypeStruct((m, LANES), F32)],
        compiler_params=_params("parallel"),
        name="norm_prep",
    )(x, g.reshape(1, d))


def _scale_rows(acc, ss_ref, d):
    rstd = lax.rsqrt(ss_ref[...] / d + NORM_EPS)
    return acc * jnp.concatenate([rstd] * (acc.shape[1] // LANES), axis=1)


def _proj_kernel(*refs, n_a, normed, residual, scale, next_norm, n_cast, d_in):
    it = iter(refs)
    a_refs = [next(it) for _ in range(n_a)]
    w_refs = [next(it) for _ in range(n_a)]
    ss_ref = next(it) if normed else None
    r_ref = next(it) if residual else None
    g_ref = next(it) if next_norm else None
    cast_in = [next(it) for _ in range(n_cast)]
    o_ref = next(it)
    acc = _dot(a_refs[0][...], w_refs[0][...])
    for a_ref, w_ref in zip(a_refs[1:], w_refs[1:]):
        acc = acc + _dot(a_ref[...], w_ref[...])
    if normed:
        acc = _scale_rows(acc, ss_ref, d_in)
    if residual:
        acc = r_ref[...] + scale * acc
    o_ref[...] = acc.astype(o_ref.dtype)
    if next_norm:
        xg_ref, ss_out_ref = next(it), next(it)
        xg_ref[...] = (acc * g_ref[...]).astype(BF16)
        part = _row_sumsq(acc)

        @pl.when(pl.program_id(1) == 0)
        def _():
            ss_out_ref[...] = part

        @pl.when(pl.program_id(1) != 0)
        def _():
            ss_out_ref[...] += part
    for src_ref in cast_in:
        next(it)[...] = src_ref[...].astype(BF16)


def _cast_specs(w32, layer, n_steps, nj):
    _, k, n = w32.shape
    nblk = n_steps
    while k % nblk or (k // nblk) % 16:
        nblk -= 1
    rb = k // nblk
    blk = lambda i, j: jnp.minimum(i * nj + j, nblk - 1)
    return (pl.BlockSpec((None, rb, n), lambda i, j: (layer, blk(i, j), 0)),
            pl.BlockSpec((rb, n), lambda i, j: (blk(i, j), 0)),
            jax.ShapeDtypeStruct((k, n), BF16))


def proj(a, w, layer, *, ss=None, residual=None, scale=1.0, next_g=None, n_cols=None, out_dtype=F32,
         cast=(), tm=1024, tn=1024, name="proj"):
    a_parts = a if isinstance(a, tuple) else (a,)
    m, kp = a_parts[0].shape
    n = w.shape[2] if n_cols is None else n_cols
    tm = _tile(m, tm)
    tn = _tile(n, tn)
    in_specs = [pl.BlockSpec((tm, kp), lambda i, j: (i, 0)) for _ in a_parts]
    in_specs += [pl.BlockSpec((None, kp, tn), lambda i, j, r=r: (layer, r, j)) for r in range(len(a_parts))]
    args = list(a_parts) + [w] * len(a_parts)
    if ss is not None:
        in_specs.append(pl.BlockSpec((tm, LANES), lambda i, j: (i, 0)))
        args.append(ss)
    if residual is not None:
        in_specs.append(pl.BlockSpec((tm, tn), lambda i, j: (i, j)))
        args.append(residual)
        out_dtype = F32
    out_specs = [pl.BlockSpec((tm, tn), lambda i, j: (i, j))]
    out_shape = [jax.ShapeDtypeStruct((m, n), out_dtype)]
    if next_g is not None:
        in_specs.append(pl.BlockSpec((1, tn), lambda i, j: (0, j)))
        args.append(next_g.reshape(1, n))
        out_specs += [pl.BlockSpec((tm, tn), lambda i, j: (i, j)), pl.BlockSpec((tm, LANES), lambda i, j: (i, 0))]
        out_shape += [jax.ShapeDtypeStruct((m, n), BF16), jax.ShapeDtypeStruct((m, LANES), F32)]
    for w32, cast_layer in cast:
        in_spec, out_spec, sds = _cast_specs(w32, cast_layer, (m // tm) * (n // tn), n // tn)
        in_specs.append(in_spec)
        args.append(w32)
        out_specs.append(out_spec)
        out_shape.append(sds)
    out = pl.pallas_call(
        functools.partial(_proj_kernel, n_a=len(a_parts), normed=ss is not None, residual=residual is not None,
                          scale=scale, next_norm=next_g is not None, n_cast=len(cast), d_in=kp * len(a_parts)),
        grid=(m // tm, n // tn),
        in_specs=in_specs,
        out_specs=out_specs,
        out_shape=out_shape,
        compiler_params=_params("parallel", "arbitrary"),
        name=name,
    )(*args)
    return out if next_g is not None or cast else out[0]


def _ffn_up_kernel(*refs, n_cast, d_in):
    h_ref, wg_ref, wu_ref, ss_ref = refs[:4]
    cast_in = refs[4:4 + n_cast]
    o_ref = refs[4 + n_cast]
    cast_out = refs[5 + n_cast:]
    h = h_ref[...]
    g = _scale_rows(_dot(h, wg_ref[...]), ss_ref, d_in)
    u = _scale_rows(_dot(h, wu_ref[...]), ss_ref, d_in)
    o_ref[...] = (_silu(g) * u).astype(o_ref.dtype)
    for src_ref, dst_ref in zip(cast_in, cast_out):
        dst_ref[...] = src_ref[...].astype(BF16)


def ffn_up(xg, ss, wg, wu, *, cast=(), tm=1024, tn=512):
    m, k = xg.shape
    n = wg.shape[1]
    tm = _tile(m, tm)
    tn = _tile(n, tn)
    wspec = pl.BlockSpec((k, tn), lambda i, j: (0, j))
    in_specs = [pl.BlockSpec((tm, k), lambda i, j: (i, 0)), wspec, wspec, pl.BlockSpec((tm, LANES), lambda i, j: (i, 0))]
    out_specs = [pl.BlockSpec((tm, tn), lambda i, j: (i, j))]
    out_shape = [jax.ShapeDtypeStruct((m, n), BF16)]
    args = [xg, wg, wu, ss]
    for w32, cast_layer in cast:
        in_spec, out_spec, sds = _cast_specs(w32, cast_layer, (m // tm) * (n // tn), n // tn)
        in_specs.append(in_spec)
        args.append(w32)
        out_specs.append(out_spec)
        out_shape.append(sds)
    return pl.pallas_call(
        functools.partial(_ffn_up_kernel, n_cast=len(cast), d_in=k),
        grid=(m // tm, n // tn),
        in_specs=in_specs,
        out_specs=out_specs,
        out_shape=out_shape,
        compiler_params=_params("parallel", "arbitrary"),
        name="ffn_up",
    )(*args)


def swiglu_half_step(x, xg, ss, weights, next_g, next_weights32):
    w_gate, w_up, w_down = weights
    cast_up, cast_down = (), ()
    if next_weights32 is not None:
        (g32, u32, d32), nl = next_weights32
        cast_up, cast_down = ((g32, nl), (u32, nl)), ((d32, nl),)
    act, *next_gu = ffn_up(xg, ss, w_gate, w_up, cast=cast_up)
    out = proj(act, w_down[None], 0, residual=x, scale=0.5, next_g=next_g, cast=cast_down, tn=512, name="ffn_down")
    out = list(out) if isinstance(out, (list, tuple)) else [out]
    x_new = out.pop(0)
    xg_new, ss_new = (out.pop(0), out.pop(0)) if next_g is not None else (None, None)
    return x_new, xg_new, ss_new, tuple(next_gu) + tuple(out)


def _gla_kernel(q_ref, k_ref, v_ref, r_ref, a_ref, wgu_ref, bg_ref, ng_ref, o_ref, st_ref, *, rt):
    C, SB = GLA_CHUNK, GLA_SUB

    @pl.when(pl.program_id(2) == 0)
    def _():
        st_ref[...] = jnp.zeros_like(st_ref)

    zg = _dot(a_ref[...].astype(BF16), wgu_ref[...]) + bg_ref[...]
    lf = (jnp.minimum(zg, 0.0) - jnp.log(1.0 + jnp.exp(-jnp.abs(zg)))) * GLA_INV_TAU

    row = lax.broadcasted_iota(jnp.int32, (rt, GLA_DK), 0)
    rc = row & (C - 1)
    b = lf
    for s in (1, 2, 4, 8, 16, 32):
        b = b + jnp.where(rc >= s, pltpu.roll(b, s, 0), 0.0)

    q = q_ref[...] * (GLA_DK ** -0.5)
    k = k_ref[...]
    v = v_ref[...].astype(BF16)

    nb = rt // SB
    b3 = b.reshape(nb, SB, GLA_DK)
    q3 = q.reshape(nb, SB, GLA_DK)
    k3 = k.reshape(nb, SB, GLA_DK)
    rowc = lax.broadcasted_iota(jnp.int32, (rt, C), 0)
    lane = lax.broadcasted_iota(jnp.int32, (rt, C), 1)
    sub_row = rowc & (SB - 1)
    sub_base = (rowc & (C - 1)) - sub_row
    a_diag = jnp.zeros((rt, C), F32)
    for j in range(SB):
        bj = jnp.broadcast_to(b3[:, j:j + 1, :], (nb, SB, GLA_DK))
        kj = jnp.broadcast_to(k3[:, j:j + 1, :], (nb, SB, GLA_DK))
        pj = (q3 * kj * jnp.exp(jnp.minimum(b3 - bj, 0.0))).reshape(rt, GLA_DK)
        rs = jnp.sum(pj, axis=-1, keepdims=True)
        a_diag = jnp.where((lane == sub_base + j) & (sub_row >= j), rs, a_diag)

    lane_c = lax.broadcasted_iota(jnp.int32, (SB, C), 1)
    outs = []
    for c in range(rt // C):
        sl = slice(c * C, (c + 1) * C)
        bc, qc, kc, vc = b[sl], q[sl], k[sl], v[sl]
        rows = [jnp.zeros((SB, C), F32)]
        for i in range(1, C // SB):
            b0 = bc[i * SB:i * SB + 1]
            qi = qc[i * SB:(i + 1) * SB] * jnp.exp(bc[i * SB:(i + 1) * SB] - b0)
            ki = kc * jnp.exp(jnp.minimum(b0 - bc, 0.0))
            ai = _dot_nt(qi.astype(BF16), ki.astype(BF16))
            rows.append(jnp.where(lane_c < i * SB, ai, 0.0))
        a_c = a_diag[sl] + jnp.concatenate(rows, axis=0)
        o_intra = _dot(a_c.astype(BF16), vc)
        st = st_ref[...]
        o_inter = _dot_nt((qc * jnp.exp(bc)).astype(BF16), st.astype(BF16))
        bl = bc[C - 1:C]
        kd = kc * jnp.exp(bl - bc)
        st_ref[...] = st * jnp.exp(bl) + _dot_tn(vc, kd.astype(BF16))
        outs.append(o_inter + o_intra)
    o = jnp.concatenate(outs, axis=0)

    ms = jnp.mean(o * o, axis=-1, keepdims=True)
    y = o * lax.rsqrt(ms + NORM_EPS) * ng_ref[...]
    o_ref[...] = (y * _silu(r_ref[...])).astype(o_ref.dtype)


def gla_mixer(z, za, w_gate_up, b_gate, norm_g, batch, seq, *, rt=512):
    m = z.shape[0]
    rt = _tile(seq, rt)
    nt = seq // rt
    kb = GLA_K_WIDTH // GLA_DK
    vb = 2 * GLA_K_WIDTH // GLA_DV
    rb = vb + GLA_V_WIDTH // GLA_DV
    wgu = jnp.zeros((LANES, GLA_K_WIDTH), BF16).at[:GLA_GATE_RANK].set(w_gate_up.astype(BF16))
    row = lambda b, h, t: b * nt + t
    return pl.pallas_call(
        functools.partial(_gla_kernel, rt=rt),
        grid=(batch, GLA_HEADS, nt),
        in_specs=[
            pl.BlockSpec((rt, GLA_DK), lambda b, h, t: (row(b, h, t), h)),
            pl.BlockSpec((rt, GLA_DK), lambda b, h, t: (row(b, h, t), kb + h)),
            pl.BlockSpec((rt, GLA_DV), lambda b, h, t: (row(b, h, t), vb + h)),
            pl.BlockSpec((rt, GLA_DV), lambda b, h, t: (row(b, h, t), rb + h)),
            pl.BlockSpec((rt, LANES), lambda b, h, t: (row(b, h, t), 0)),
            pl.BlockSpec((LANES, GLA_DK), lambda b, h, t: (0, h)),
            pl.BlockSpec((1, GLA_DK), lambda b, h, t: (0, h)),
            pl.BlockSpec((1, GLA_DV), lambda b, h, t: (0, 0)),
        ],
        out_specs=pl.BlockSpec((rt, GLA_DV), lambda b, h, t: (row(b, h, t), h)),
        out_shape=jax.ShapeDtypeStruct((m, GLA_V_WIDTH), BF16),
        scratch_shapes=[pltpu.VMEM((GLA_DV, GLA_DK), F32)],
        compiler_params=_params("parallel", "parallel", "arbitrary"),
        name="gla",
    )(z, z, z, z, za, wgu, b_gate.reshape(1, GLA_K_WIDTH), norm_g.reshape(1, GLA_DV))


def _sg_kernel(u_ref, s_ref, lng_ref, lnb_ref, ws_ref, bs_ref, o_ref, *, rt):
    s = _gelu_tanh(s_ref[...])
    mu = jnp.mean(s, axis=-1, keepdims=True)
    sc = s - mu
    var = jnp.mean(sc * sc, axis=-1, keepdims=True)
    sn = (sc * lax.rsqrt(var + LN_EPS) * lng_ref[...] + lnb_ref[...]).astype(BF16)
    u = _gelu_tanh(u_ref[...])
    ri = lax.broadcasted_iota(jnp.int32, (SG_CHUNK, SG_CHUNK), 0)
    ci = lax.broadcasted_iota(jnp.int32, (SG_CHUNK, SG_CHUNK), 1)
    for g in range(SG_GROUPS):
        w = jnp.where(ci <= ri, ws_ref[g], 0.0).astype(BF16)
        cols = slice(g * SG_GROUP_WIDTH, (g + 1) * SG_GROUP_WIDTH)
        for c in range(rt // SG_CHUNK):
            rows = slice(c * SG_CHUNK, (c + 1) * SG_CHUNK)
            mixed = _dot(w, sn[rows, cols]) + bs_ref[g]
            o_ref[rows, cols] = (u[rows, cols] * mixed).astype(o_ref.dtype)


def sg_mixer(z, ln_g, ln_b, w_s, b_s, *, rt=512):
    m = z.shape[0]
    rt = _tile(m, rt)
    return pl.pallas_call(
        functools.partial(_sg_kernel, rt=rt),
        grid=(m // rt,),
        in_specs=[
            pl.BlockSpec((rt, SG_WIDTH), lambda i: (i, 0)),
            pl.BlockSpec((rt, SG_WIDTH), lambda i: (i, 1)),
            pl.BlockSpec((1, SG_WIDTH), lambda i: (0, 0)),
            pl.BlockSpec((1, SG_WIDTH), lambda i: (0, 0)),
            pl.BlockSpec((SG_GROUPS, SG_CHUNK, SG_CHUNK), lambda i: (0, 0, 0)),
            pl.BlockSpec((SG_GROUPS, SG_CHUNK, 1), lambda i: (0, 0, 0)),
        ],
        out_specs=pl.BlockSpec((rt, SG_WIDTH), lambda i: (i, 0)),
        out_shape=jax.ShapeDtypeStruct((m, SG_WIDTH), BF16),
        compiler_params=_params("parallel"),
        name="spatial_gate",
    )(z, z, ln_g.reshape(1, SG_WIDTH), ln_b.reshape(1, SG_WIDTH), w_s, b_s.reshape(SG_GROUPS, SG_CHUNK, 1))


def split_ab_w_in(w_in):
    layers, d, _ = w_in.shape
    a0 = 2 * GLA_K_WIDTH + 2 * GLA_V_WIDTH
    w_a = jnp.concatenate(
        [w_in[:, :, a0:a0 + GLA_GATE_RANK], jnp.zeros((layers, d, LANES - GLA_GATE_RANK), BF16)], axis=2)
    return w_in, w_a, w_in[:, :, a0 + GLA_GATE_RANK:]


def mixer_gla_sg(x, xg, ss, layer, batch, seq, w_in_parts, w_gate_up, b_gate, gla_norm_g,
                 sg_ln_g, sg_ln_b, sg_w_s, sg_b_s, w_out, next_g):
    w, w_a, w_us = w_in_parts
    z = proj(xg, w, layer, ss=ss, n_cols=2 * GLA_K_WIDTH + 2 * GLA_V_WIDTH, name="ab_in_proj")
    za = proj(xg, w_a, layer, ss=ss, name="ab_gate_proj")
    zus = proj(xg, w_us, layer, ss=ss, name="ab_sg_proj")
    y_gla = gla_mixer(z, za, w_gate_up, b_gate, gla_norm_g, batch, seq)
    y_sg = sg_mixer(zus, sg_ln_g, sg_ln_b, sg_w_s, sg_b_s)
    return proj((y_gla, y_sg), w_out, layer, residual=x, next_g=next_g, tn=512, name="ab_out_proj")


def _cross_kernel(q_ref, k_ref, v_ref, o_ref):
    for hd in range(X_HEADS):
        cols = slice(hd * X_HEAD_DIM, (hd + 1) * X_HEAD_DIM)
        s = _dot_nt(q_ref[:, cols], k_ref[:, cols]) * X_SCALE
        p = jnp.exp(s - jnp.max(s, axis=-1, keepdims=True))
        p = p / jnp.sum(p, axis=-1, keepdims=True)
        o_ref[:, cols] = _dot(p.astype(BF16), v_ref[:, cols]).astype(o_ref.dtype)


def cross_attention(x, xg, ss, kv, batch, seq, w_q, w_o, layer, next_g, *, tq=1024):
    m = xg.shape[0]
    mem = kv.shape[0] // batch
    tq = _tile(seq, tq)
    nt = seq // tq
    q = proj(xg, w_q, layer, ss=ss, out_dtype=BF16, name="cross_q_proj")
    o = pl.pallas_call(
        _cross_kernel,
        grid=(batch, nt),
        in_specs=[
            pl.BlockSpec((tq, X_WIDTH), lambda b, t: (b * nt + t, 0)),
            pl.BlockSpec((mem, X_WIDTH), lambda b, t: (b, 0)),
            pl.BlockSpec((mem, X_WIDTH), lambda b, t: (b, 1)),
        ],
        out_specs=pl.BlockSpec((tq, X_WIDTH), lambda b, t: (b * nt + t, 0)),
        out_shape=jax.ShapeDtypeStruct((m, X_WIDTH), BF16),
        compiler_params=_params("parallel", "parallel"),
        name="cross_attn",
    )(q, kv, kv)
    return proj(o, w_o, layer, residual=x, next_g=next_g, tn=512, name="cross_o_proj")


def _rope_tables(seq):
    inv_freq = jnp.power(ROPE_THETA, -jnp.arange(ROPE_HALF, dtype=F32) / ROPE_HALF)
    ang = jnp.arange(seq).astype(F32)[:, None] * inv_freq[None, :]
    cos, sin = jnp.cos(ang), jnp.sin(ang)
    rest = NSA_HEAD_DIM - ROPE_DIM
    c = jnp.concatenate([cos, cos, jnp.ones((seq, rest), F32)], axis=1)
    s_up = jnp.concatenate([jnp.zeros_like(sin), sin, jnp.zeros((seq, rest), F32)], axis=1)
    s_dn = jnp.concatenate([-sin, jnp.zeros_like(sin), jnp.zeros((seq, rest), F32)], axis=1)
    return c, s_up, s_dn


def _rope(x, c, s_up, s_dn, heads):
    width = heads * NSA_HEAD_DIM
    rep = lambda t: jnp.concatenate([t] * heads, axis=1)
    return (x * rep(c) + pltpu.roll(x, ROPE_HALF, 1) * rep(s_up)
            + pltpu.roll(x, width - ROPE_HALF, 1) * rep(s_dn))


def _nsa_kv_kernel(ks_ref, vs_ref, kw_ref, vw_ref, c_ref, su_ref, sd_ref, oks_ref, ovs_ref, okw_ref, ovw_ref):
    c, su, sd = c_ref[...], su_ref[...], sd_ref[...]
    oks_ref[...] = _rope(ks_ref[...], c, su, sd, NSA_KV_GROUPS).astype(BF16)
    okw_ref[...] = _rope(kw_ref[...], c, su, sd, NSA_KV_GROUPS).astype(BF16)
    ones = jnp.ones((ks_ref.shape[0], NSA_HEAD_DIM), BF16)
    for v_ref, o_ref in ((vs_ref, ovs_ref), (vw_ref, ovw_ref)):
        v = v_ref[...].astype(BF16)
        o_ref[...] = jnp.concatenate(
            [piece for g in range(NSA_KV_GROUPS) for piece in (v[:, g * NSA_HEAD_DIM:(g + 1) * NSA_HEAD_DIM], ones)],
            axis=1)


def nsa_kv_prep(z, tables, batch, seq, *, rt=1024):
    m = z.shape[0]
    rt = _tile(seq, rt)
    nt = seq // rt
    cb = NSA_Q_WIDTH // NSA_KV_WIDTH + 2
    zspec = lambda j: pl.BlockSpec((rt, NSA_KV_WIDTH), lambda b, t: (b * nt + t, cb + j))
    tspec = pl.BlockSpec((rt, NSA_HEAD_DIM), lambda b, t: (t, 0))
    kspec = pl.BlockSpec((rt, NSA_KV_WIDTH), lambda b, t: (b * nt + t, 0))
    vspec = pl.BlockSpec((rt, 2 * NSA_KV_WIDTH), lambda b, t: (b * nt + t, 0))
    ksds = jax.ShapeDtypeStruct((m, NSA_KV_WIDTH), BF16)
    vsds = jax.ShapeDtypeStruct((m, 2 * NSA_KV_WIDTH), BF16)
    return pl.pallas_call(
        _nsa_kv_kernel,
        grid=(batch, nt),
        in_specs=[zspec(0), zspec(1), zspec(2), zspec(3), tspec, tspec, tspec],
        out_specs=[kspec, vspec, kspec, vspec],
        out_shape=[ksds, vsds, ksds, vsds],
        compiler_params=_params("parallel", "parallel"),
        name="nsa_kv_prep",
    )(z, z, z, z, *tables)


def _nsa_cmp_kernel(x_ref, pos_ref, w1_ref, b1_ref, w2_ref, b2_ref, o_ref, *, nblk):
    half = NSA_CMP_STRIDE * NSA_HEAD_DIM
    pos_bias = _dot(pos_ref[...], w1_ref[...])[0:1] + b1_ref[...]
    row = lax.broadcasted_iota(jnp.int32, (nblk, NSA_HEAD_DIM), 0)
    for g in range(NSA_KV_GROUPS):
        xg = jnp.concatenate(
            [x_ref[:, l * NSA_KV_WIDTH + g * NSA_HEAD_DIM:l * NSA_KV_WIDTH + (g + 1) * NSA_HEAD_DIM]
             for l in range(NSA_CMP_STRIDE)], axis=1).astype(BF16)
        first = _dot(xg, w1_ref[:half])
        second = _dot(xg, w1_ref[half:])
        pre = first + pltpu.roll(second, nblk - 1, 0) + pos_bias
        out = _dot(_silu(pre).astype(BF16), w2_ref[...]) + b2_ref[...]
        o_ref[g] = jnp.where(row < nblk - 1, out, 0.0).astype(o_ref.dtype)


def nsa_compress(z, batch, seq, cmp_pos, cmp_w1, cmp_b1, cmp_w2, cmp_b2):
    nblk = seq // NSA_CMP_STRIDE
    c0 = NSA_Q_WIDTH
    xr = jnp.stack([z[:, c0:c0 + NSA_KV_WIDTH], z[:, c0 + NSA_KV_WIDTH:c0 + 2 * NSA_KV_WIDTH]])
    xr = xr.reshape(2, batch, nblk, NSA_CMP_STRIDE * NSA_KV_WIDTH)
    flat = NSA_CMP_LEN * NSA_HEAD_DIM
    pos = jnp.zeros((2, 8, flat), BF16).at[:, 0].set(cmp_pos.reshape(2, flat).astype(BF16))
    return pl.pallas_call(
        functools.partial(_nsa_cmp_kernel, nblk=nblk),
        grid=(2, batch),
        in_specs=[
            pl.BlockSpec((None, None, nblk, NSA_CMP_STRIDE * NSA_KV_WIDTH), lambda s, b: (s, b, 0, 0)),
            pl.BlockSpec((None, 8, flat), lambda s, b: (s, 0, 0)),
            pl.BlockSpec((None, flat, NSA_HEAD_DIM), lambda s, b: (s, 0, 0)),
            pl.BlockSpec((None, 1, NSA_HEAD_DIM), lambda s, b: (s, 0, 0)),
            pl.BlockSpec((None, NSA_HEAD_DIM, NSA_HEAD_DIM), lambda s, b: (s, 0, 0)),
            pl.BlockSpec((None, 1, NSA_HEAD_DIM), lambda s, b: (s, 0, 0)),
        ],
        out_specs=pl.BlockSpec((None, None, NSA_KV_GROUPS, nblk, NSA_HEAD_DIM), lambda s, b: (s, b, 0, 0, 0)),
        out_shape=jax.ShapeDtypeStruct((2, batch, NSA_KV_GROUPS, nblk, NSA_HEAD_DIM), BF16),
        compiler_params=_params("parallel", "parallel"),
        name="nsa_compress",
    )(xr, pos, cmp_w1.astype(BF16), cmp_b1.reshape(2, 1, NSA_HEAD_DIM), cmp_w2.astype(BF16),
      cmp_b2.reshape(2, 1, NSA_HEAD_DIM))


def _row_reduce(s, op, lane_op):
    acc = s[:, :LANES]
    for c in range(1, s.shape[1] // LANES):
        acc = op(acc, s[:, c * LANES:(c + 1) * LANES])
    return lane_op(acc, axis=-1, keepdims=True)


def _nsa_attn_kernel(q_ref, gate_ref, c_ref, su_ref, sd_ref, kc_ref, vc_ref, ks_ref, vs_ref, kw_ref, vw_ref,
                     ovl_ref, blk_ref, spread_ref, o_ref,
                     qp_ref, qr_ref, s_ref, p_ref, bias_ref, m_ref, acc_ref, psum_ref, sel_ref, mix_ref, gx_ref,
                     *, tq, tk, seq, top_n):
    P, HD, SB, W, SL = NSA_HPG, NSA_HEAD_DIM, NSA_SEL_BLOCK, NSA_WINDOW, NSA_SLAB
    R = P * tq
    nsel = seq // SB
    ncmp = seq // NSA_CMP_STRIDE
    t0 = pl.program_id(2) * tq

    qf = q_ref[...]
    q_scale = NSA_SCALE * LOG2_E
    qr = _rope(qf, c_ref[...], su_ref[...], sd_ref[...], P)
    for p in range(P):
        qp_ref[p * tq:(p + 1) * tq] = (qf[:, p * HD:(p + 1) * HD] * q_scale).astype(BF16)
        qr_ref[p * tq:(p + 1) * tq] = (qr[:, p * HD:(p + 1) * HD] * q_scale).astype(BF16)

    def reset_state():
        m_ref[...] = jnp.full_like(m_ref, MASK_NEG)
        acc_ref[...] = jnp.zeros_like(acc_ref)

    def softmax_tile(k_t, v_t, n):
        for g0 in range(0, R, NSA_GROUP_ROWS):
            grp = slice(g0, g0 + NSA_GROUP_ROWS)
            s_grp = _dot_nt(qr_ref[grp], k_t)
            alphas, probs = [], []
            for r0 in range(g0, g0 + NSA_GROUP_ROWS, SL):
                rows = slice(r0, r0 + SL)
                b0 = r0 % tq
                s = s_grp[r0 - g0:r0 - g0 + SL] + bias_ref[b0:b0 + SL, :n]
                m_old = m_ref[rows]
                m_new = jnp.maximum(m_old, _row_reduce(s, jnp.maximum, jnp.max))
                alphas.append(jnp.exp2(m_old - m_new))
                m_ref[rows] = m_new
                probs.append(jnp.exp2(s - m_new).astype(BF16))
            alpha = jnp.concatenate(alphas, axis=0)
            acc_ref[grp] = alpha * acc_ref[grp] + _dot(jnp.concatenate(probs, axis=0), v_t)

    t_c = t0 + lax.broadcasted_iota(jnp.int32, (tq, ncmp), 0)
    end_c = lax.broadcasted_iota(jnp.int32, (tq, ncmp), 1) * NSA_CMP_STRIDE + (NSA_CMP_LEN - 1)
    bias_ref[:, :ncmp] = jnp.where(end_c <= t_c, 0.0, MASK_NEG)
    s_ref[:, :ncmp] = _dot_nt(qp_ref[...], kc_ref[...])
    psum_ref[...] = jnp.zeros_like(psum_ref)
    for j in range(R // SL):
        rows = slice(j * SL, (j + 1) * SL)
        b0 = (j * SL) % tq
        bias = bias_ref[b0:b0 + SL, :ncmp]
        s = s_ref[rows, :ncmp] + bias
        e = jnp.where(bias == 0.0, jnp.exp2(s - _row_reduce(s, jnp.maximum, jnp.max)), 0.0)
        pn = e / jnp.maximum(_row_reduce(e, jnp.add, jnp.sum), 1e-30)
        p_ref[rows, :ncmp] = pn.astype(BF16)
        psum_ref[b0:b0 + SL] += pn
    o_c = _dot(p_ref[:, :ncmp], vc_ref[...])
    gates = _sigmoid(gate_ref[...])
    g_hi = gates.astype(BF16)
    g_lo = (gates - g_hi.astype(F32)).astype(BF16)
    gx_ref[...] = _dot(g_hi, spread_ref[...]) + _dot(g_lo, spread_ref[...])
    gate = lambda p, j: gx_ref[:, (3 * p + j) * LANES:(3 * p + j + 1) * LANES]
    for p in range(P):
        rows = slice(p * tq, (p + 1) * tq)
        mix_ref[rows] = gate(p, 0) * o_c[rows]

    p_sum = psum_ref[...]
    p_hi = p_sum.astype(BF16)
    p_lo = (p_sum - p_hi.astype(F32)).astype(BF16)
    imp_t = _dot_nt(ovl_ref[...], p_hi) + _dot_nt(ovl_ref[...], p_lo)

    blk = lax.broadcasted_iota(jnp.int32, (nsel, tq), 0)
    cur = (t0 + lax.broadcasted_iota(jnp.int32, (nsel, tq), 1)) // SB
    forced = (blk == 0) | (blk == cur) | (blk == cur - 1)
    score = jnp.where(blk <= cur, imp_t + jnp.where(forced, NSA_FORCE_BONUS, 0.0), -jnp.inf)
    rank = jnp.zeros((nsel, tq), jnp.int32)
    for sp in range(nsel):
        other = jnp.broadcast_to(score[sp:sp + 1, :], (nsel, tq))
        rank = rank + jnp.where(blk > sp, jnp.where(other >= score, 1, 0), jnp.where(other > score, 1, 0))
    sel_ref[...] = jnp.where((rank < top_n) & (blk <= cur), 1.0, 0.0).T.astype(BF16)

    wlen = W + tq
    base = pl.multiple_of(jnp.clip(t0 - W, 0, seq - wlen), tq)
    delta = (t0 + lax.broadcasted_iota(jnp.int32, (tq, wlen), 0)) - (base + lax.broadcasted_iota(jnp.int32, (tq, wlen), 1))
    bias_ref[:, :wlen] = jnp.where((delta >= 0) & (delta < W), 0.0, MASK_NEG)
    reset_state()
    softmax_tile(kw_ref[pl.ds(base, wlen), :], vw_ref[pl.ds(base, wlen), :], wlen)
    for p in range(P):
        rows = slice(p * tq, (p + 1) * tq)
        o_w = acc_ref[rows, :HD] / acc_ref[rows, HD:]
        mix_ref[rows] += gate(p, 2) * o_w

    t_k = t0 + lax.broadcasted_iota(jnp.int32, (tq, tk), 0)
    lane_k = lax.broadcasted_iota(jnp.int32, (tq, tk), 1)
    reset_state()

    def sel_step(kt, carry):
        k0 = pl.multiple_of(kt * tk, tk)
        picked = _dot(sel_ref[...], blk_ref[kt])
        bias_ref[:, :tk] = jnp.where((picked > 0.5) & (lane_k + k0 <= t_k), 0.0, MASK_NEG)
        softmax_tile(ks_ref[pl.ds(k0, tk), :], vs_ref[pl.ds(k0, tk), :], tk)
        return carry

    lax.fori_loop(0, (t0 + tq + tk - 1) // tk, sel_step, 0)
    for p in range(P):
        rows = slice(p * tq, (p + 1) * tq)
        o_s = acc_ref[rows, :HD] / acc_ref[rows, HD:]
        o_ref[:, p * HD:(p + 1) * HD] = (mix_ref[rows] + gate(p, 1) * o_s).astype(o_ref.dtype)


def nsa_attention(z, zg, kv, cmp, tables, batch, seq, *, tq=256, tk=1024):
    m = z.shape[0]
    ks, vs, kw, vw = kv
    tq = _tile(seq, tq)
    tk = _tile(seq, tk)
    nt = seq // tq
    nsel = seq // NSA_SEL_BLOCK
    ncmp = seq // NSA_CMP_STRIDE
    gw = NSA_HPG * NSA_HEAD_DIM
    c_start = jnp.arange(ncmp) * NSA_CMP_STRIDE
    s_start = jnp.arange(nsel) * NSA_SEL_BLOCK
    ovl = ((c_start[None, :] <= s_start[:, None] + NSA_SEL_BLOCK - 1)
           & (c_start[None, :] + NSA_CMP_LEN - 1 >= s_start[:, None])
           & (jnp.arange(ncmp)[None, :] < ncmp - 1)).astype(BF16)
    blk_of_key = (jnp.arange(seq)[None, :] // NSA_SEL_BLOCK == jnp.arange(nsel)[:, None]).astype(BF16)
    blk_of_key = blk_of_key.reshape(nsel, seq // tk, tk).transpose(1, 0, 2)
    n_gate = 3 * NSA_HPG
    spread = (jnp.arange(LANES)[:, None] == jnp.arange(n_gate * LANES)[None, :] // LANES).astype(BF16)
    kspec =pl.BlockSpec((seq, NSA_HEAD_DIM), lambda b, g, i: (b, g))
    vspec = pl.BlockSpec((seq, 2 * NSA_HEAD_DIM), lambda b, g, i: (b, g))
    cmpspec = lambda s: pl.BlockSpec((None, None, None, ncmp, NSA_HEAD_DIM), lambda b, g, i: (s, b, g, 0, 0))
    tspec = pl.BlockSpec((tq, NSA_HEAD_DIM), lambda b, g, i: (i, 0))
    rows = NSA_HPG * tq
    width = max(tk, NSA_WINDOW + tq, ncmp)
    return pl.pallas_call(
        functools.partial(_nsa_attn_kernel, tq=tq, tk=tk, seq=seq, top_n=min(NSA_TOPK, nsel)),
        grid=(batch, NSA_KV_GROUPS, nt),
        in_specs=[
            pl.BlockSpec((tq, gw), lambda b, g, i: (b * nt + i, g)),
            pl.BlockSpec((tq, LANES), lambda b, g, i: (b * nt + i, g)),
            tspec, tspec, tspec,
            cmpspec(0), cmpspec(1),
            kspec, vspec, kspec, vspec,
            pl.BlockSpec((nsel, ncmp), lambda b, g, i: (0, 0)),
            pl.BlockSpec((seq // tk, nsel, tk), lambda b, g, i: (0, 0, 0)),
            pl.BlockSpec((LANES, n_gate * LANES), lambda b, g, i: (0, 0)),
        ],
        out_specs=pl.BlockSpec((tq, gw), lambda b, g, i: (b * nt + i, g)),
        out_shape=jax.ShapeDtypeStruct((m, NSA_Q_WIDTH), BF16),
        scratch_shapes=[
            pltpu.VMEM((rows, NSA_HEAD_DIM), BF16),
            pltpu.VMEM((rows, NSA_HEAD_DIM), BF16),
            pltpu.VMEM((rows, width), F32),
            pltpu.VMEM((rows, width), BF16),
            pltpu.VMEM((tq, width), F32),
            pltpu.VMEM((rows, 1), F32),
            pltpu.VMEM((rows, 2 * NSA_HEAD_DIM), F32),
            pltpu.VMEM((tq, ncmp), F32),
            pltpu.VMEM((tq, nsel), BF16),
            pltpu.VMEM((rows, NSA_HEAD_DIM), F32),
            pltpu.VMEM((tq, n_gate * LANES), F32),
        ],
        compiler_params=_params("parallel", "parallel", "arbitrary"),
        name="nsa_attn",
    )(z, zg, *tables, cmp, cmp, ks, vs, kw, vw, ovl, blk_of_key, spread)


def split_nsa_w_in(w_in):
    layers, d, _ = w_in.shape
    c0 = NSA_Q_WIDTH + 6 * NSA_KV_WIDTH
    per_group = 3 * NSA_HPG
    cols = []
    for g in range(NSA_KV_GROUPS):
        cols += [w_in[:, :, c0 + g * per_group:c0 + (g + 1) * per_group],
                 jnp.zeros((layers, d, LANES - per_group), BF16)]
    return w_in, jnp.concatenate(cols, axis=2)


def mixer_nsa(x, xg, ss, layer, batch, seq, w_in_parts, cmp_pos, cmp_w1, cmp_b1, cmp_w2, cmp_b2, w_out, next_g):
    w, w_gates = w_in_parts
    z = proj(xg, w, layer, ss=ss, n_cols=NSA_Q_WIDTH + 6 * NSA_KV_WIDTH, name="nsa_in_proj")
    zg = proj(xg, w_gates, layer, ss=ss, name="nsa_gate_proj")
    tables = _rope_tables(seq)
    kv = nsa_kv_prep(z, tables, batch, seq)
    cmp = nsa_compress(z, batch, seq, cmp_pos, cmp_w1, cmp_b1, cmp_w2, cmp_b2)
    o = nsa_attention(z, zg, kv, cmp, tables, batch, seq)
    return proj(o, w_out, layer, residual=x, next_g=next_g, tn=512, name="nsa_out_proj")


def kernel(x, mem, mem_norm_g,
           ffn1_norm_g, ffn1_w_gate, ffn1_w_up, ffn1_w_down,
           mix_norm_g,
           ab_w_in, gla_w_gate_up, gla_b_gate, gla_norm_g, sg_ln_g, sg_ln_b, sg_w_s, sg_b_s, ab_w_out,
           nsa_w_in, nsa_cmp_pos, nsa_cmp_w1, nsa_cmp_b1, nsa_cmp_w2, nsa_cmp_b2, nsa_w_out,
           cross_norm_g, cross_w_q, cross_w_kv, cross_w_o,
           ffn2_norm_g, ffn2_w_gate, ffn2_w_up, ffn2_w_down,
           final_norm_g):
    batch, seq, d = x.shape
    depth = ffn1_norm_g.shape[0]
    bf = lambda w: w.astype(BF16)
    ffn1_w32 = (ffn1_w_gate, ffn1_w_up, ffn1_w_down)
    ffn2_w32 = (ffn2_w_gate, ffn2_w_up, ffn2_w_down)
    ffn_w = tuple(bf(w[0]) for w in ffn1_w32)
    ab_w_in_parts, ab_w_out = split_ab_w_in(bf(ab_w_in)), bf(ab_w_out)
    nsa_w_in_parts, nsa_w_out = split_nsa_w_in(bf(nsa_w_in)), bf(nsa_w_out)
    cross_w_q, cross_w_kv, cross_w_o = bf(cross_w_q), bf(cross_w_kv), bf(cross_w_o)

    mem_n = rmsnorm(mem.reshape(-1, d), mem_norm_g)
    x = x.reshape(batch * seq, d)
    xg, ss = norm_prep(x, ffn1_norm_g[0])
    for i in range(depth):
        x, xg, ss, ffn_w = swiglu_half_step(x, xg, ss, ffn_w, mix_norm_g[i], (ffn2_w32, i))
        j = i // 2
        if i % 2 == 0:
            x, xg, ss = mixer_gla_sg(x, xg, ss, j, batch, seq, ab_w_in_parts, gla_w_gate_up[j], gla_b_gate[j],
                                     gla_norm_g[j], sg_ln_g[j], sg_ln_b[j], sg_w_s[j], sg_b_s[j], ab_w_out,
                                     cross_norm_g[i])
        else:
            x, xg, ss = mixer_nsa(x, xg, ss, j, batch, seq, nsa_w_in_parts, nsa_cmp_pos[j], nsa_cmp_w1[j],
                                  nsa_cmp_b1[j], nsa_cmp_w2[j], nsa_cmp_b2[j], nsa_w_out, cross_norm_g[i])
        kv = proj(mem_n, cross_w_kv, i, out_dtype=BF16, name="cross_kv_proj")
        x, xg, ss = cross_attention(x, xg, ss, kv, batch, seq, cross_w_q, cross_w_o, i, ffn2_norm_g[i])
        last = i + 1 == depth
        x, xg, ss, ffn_w = swiglu_half_step(x, xg, ss, ffn_w, None if last else ffn1_norm_g[i + 1],
                                            None if last else (ffn1_w32, i + 1))
    return rmsnorm(x, final_norm_g, out_dtype=F32).reshape(batch, seq, d)
```

```python
import functools

import jax
import jax.numpy as jnp
from jax import lax
from jax.experimental import pallas as pl
from jax.experimental.pallas import tpu as pltpu

F32 = jnp.float32
BF16 = jnp.bfloat16

NORM_EPS = 1e-6
LN_EPS = 1e-5
ROPE_THETA = 500000.0

GLA_HEADS = 8
GLA_DK = 128
GLA_DV = 256
GLA_K_WIDTH = GLA_HEADS * GLA_DK
GLA_V_WIDTH = GLA_HEADS * GLA_DV
GLA_GATE_RANK = 16
GLA_INV_TAU = 1.0 / 16.0
GLA_CHUNK = 64
GLA_SUB = 16
SG_WIDTH = 2048
SG_GROUPS = 4
SG_GROUP_WIDTH = SG_WIDTH // SG_GROUPS
SG_CHUNK = 128

NSA_HEAD_DIM = 128
NSA_KV_GROUPS = 4
NSA_HPG = 8
NSA_Q_WIDTH = NSA_KV_GROUPS * NSA_HPG * NSA_HEAD_DIM
NSA_KV_WIDTH = NSA_KV_GROUPS * NSA_HEAD_DIM
NSA_CMP_STRIDE = 16
NSA_CMP_LEN = 32
NSA_SEL_BLOCK = 64
NSA_TOPK = 16
NSA_WINDOW = 512
NSA_FORCE_BONUS = 1e4
NSA_SCALE = NSA_HEAD_DIM ** -0.5
ROPE_DIM = NSA_HEAD_DIM // 4
ROPE_HALF = ROPE_DIM // 2
NSA_SLAB = 64
NSA_GROUP_ROWS = 256
LOG2_E = 1.4426950408889634

X_HEADS = 4
X_HEAD_DIM = 128
X_WIDTH = X_HEADS * X_HEAD_DIM
X_SCALE = X_HEAD_DIM ** -0.5

LANES = 128
MASK_NEG = -1e30
VMEM_LIMIT_BYTES = 56 * 1024 * 1024


def _params(*sem):
    return pltpu.CompilerParams(dimension_semantics=sem, vmem_limit_bytes=VMEM_LIMIT_BYTES)


def _dot(a, b):
    return jnp.dot(a, b, preferred_element_type=F32)


def _dot_nt(a, b):
    return lax.dot_general(a, b, (((1,), (1,)), ((), ())), preferred_element_type=F32)


def _dot_tn(a, b):
    return lax.dot_general(a, b, (((0,), (0,)), ((), ())), preferred_element_type=F32)


def _sigmoid(x):
    return 1.0 / (1.0 + jnp.exp(-x))


def _silu(x):
    return x * _sigmoid(x)


def _gelu_tanh(x):
    c = 0.7978845608028654
    return x * (0.5 * (1.0 + jnp.tanh(c * (x + 0.044715 * (x * x * x)))))


def _tile(n, pref):
    if n <= pref:
        return n
    t = pref
    while n % t:
        t //= 2
    return t


def _rmsnorm_kernel(x_ref, g_ref, o_ref):
    x = x_ref[...]
    ms = jnp.mean(x * x, axis=-1, keepdims=True)
    o_ref[...] = (x * lax.rsqrt(ms + NORM_EPS) * g_ref[...]).astype(o_ref.dtype)


def rmsnorm(x, g, out_dtype=BF16):
    m, d = x.shape
    tm = _tile(m, 256)
    return pl.pallas_call(
        _rmsnorm_kernel,
        grid=(m // tm,),
        in_specs=[pl.BlockSpec((tm, d), lambda i: (i, 0)), pl.BlockSpec((1, d), lambda i: (0, 0))],
        out_specs=pl.BlockSpec((tm, d), lambda i: (i, 0)),
        out_shape=jax.ShapeDtypeStruct((m, d), out_dtype),
        compiler_params=_params("parallel"),
        name="rmsnorm",
    )(x, g.reshape(1, d))


def _row_sumsq(x):
    return jnp.broadcast_to(jnp.sum(x * x, axis=-1, keepdims=True), (x.shape[0], LANES))


def _norm_prep_kernel(x_ref, g_ref, xg_ref, ss_ref):
    x = x_ref[...]
    xg_ref[...] = (x * g_ref[...]).astype(BF16)
    ss_ref[...] = _row_sumsq(x)


def norm_prep(x, g):
    m, d = x.shape
    tm = _tile(m, 256)
    return pl.pallas_call(
        _norm_prep_kernel,
        grid=(m // tm,),
        in_specs=[pl.BlockSpec((tm, d), lambda i: (i, 0)), pl.BlockSpec((1, d), lambda i: (0, 0))],
        out_specs=[pl.BlockSpec((tm, d), lambda i: (i, 0)), pl.BlockSpec((tm, LANES), lambda i: (i, 0))],
        out_shape=[jax.ShapeDtypeStruct((m, d), BF16), jax.ShapeDtypeStruct((m, LANES), F32)],
        compiler_params=_params("parallel"),
        name="norm_prep",
    )(x, g.reshape(1, d))


def _scale_rows(acc, ss_ref, d):
    rstd = lax.rsqrt(ss_ref[...] / d + NORM_EPS)
    return acc * jnp.concatenate([rstd] * (acc.shape[1] // LANES), axis=1)


def _proj_kernel(*refs, n_a, normed, residual, scale, next_norm, n_cast, d_in):
    it = iter(refs)
    a_refs = [next(it) for _ in range(n_a)]
    w_refs = [next(it) for _ in range(n_a)]
    ss_ref = next(it) if normed else None
    r_ref = next(it) if residual else None
    g_ref = next(it) if next_norm else None
    cast_in = [next(it) for _ in range(n_cast)]
    o_ref = next(it)
    acc = _dot(a_refs[0][...], w_refs[0][...])
    for a_ref, w_ref in zip(a_refs[1:], w_refs[1:]):
        acc = acc + _dot(a_ref[...], w_ref[...])
    if normed:
        acc = _scale_rows(acc, ss_ref, d_in)
    if residual:
        acc = r_ref[...] + scale * acc
    o_ref[...] = acc.astype(o_ref.dtype)
    if next_norm:
        xg_ref, ss_out_ref = next(it), next(it)
        xg_ref[...] = (acc * g_ref[...]).astype(BF16)
        part = _row_sumsq(acc)

        @pl.when(pl.program_id(1) == 0)
        def _():
            ss_out_ref[...] = part

        @pl.when(pl.program_id(1) != 0)
        def _():
            ss_out_ref[...] += part
    for src_ref in cast_in:
        next(it)[...] = src_ref[...].astype(BF16)


def _cast_specs(w32, layer, n_steps, nj):
    _, k, n = w32.shape
    nblk = n_steps
    while k % nblk or (k // nblk) % 16:
        nblk -= 1
    rb = k // nblk
    blk = lambda i, j: jnp.minimum(i * nj + j, nblk - 1)
    return (pl.BlockSpec((None, rb, n), lambda i, j: (layer, blk(i, j), 0)),
            pl.BlockSpec((rb, n), lambda i, j: (blk(i, j), 0)),
            jax.ShapeDtypeStruct((k, n), BF16))


def proj(a, w, layer, *, ss=None, residual=None, scale=1.0, next_g=None, n_cols=None, out_dtype=F32,
         cast=(), tm=1024, tn=1024, name="proj"):
    a_parts = a if isinstance(a, tuple) else (a,)
    m, kp = a_parts[0].shape
    n = w.shape[2] if n_cols is None else n_cols
    tm = _tile(m, tm)
    tn = _tile(n, tn)
    in_specs = [pl.BlockSpec((tm, kp), lambda i, j: (i, 0)) for _ in a_parts]
    in_specs += [pl.BlockSpec((None, kp, tn), lambda i, j, r=r: (layer, r, j)) for r in range(len(a_parts))]
    args = list(a_parts) + [w] * len(a_parts)
    if ss is not None:
        in_specs.append(pl.BlockSpec((tm, LANES), lambda i, j: (i, 0)))
        args.append(ss)
    if residual is not None:
        in_specs.append(pl.BlockSpec((tm, tn), lambda i, j: (i, j)))
        args.append(residual)
        out_dtype = F32
    out_specs = [pl.BlockSpec((tm, tn), lambda i, j: (i, j))]
    out_shape = [jax.ShapeDtypeStruct((m, n), out_dtype)]
    if next_g is not None:
        in_specs.append(pl.BlockSpec((1, tn), lambda i, j: (0, j)))
        args.append(next_g.reshape(1, n))
        out_specs += [pl.BlockSpec((tm, tn), lambda i, j: (i, j)), pl.BlockSpec((tm, LANES), lambda i, j: (i, 0))]
        out_shape += [jax.ShapeDtypeStruct((m, n), BF16), jax.ShapeDtypeStruct((m, LANES), F32)]
    for w32, cast_layer in cast:
        in_spec, out_spec, sds = _cast_specs(w32, cast_layer, (m // tm) * (n // tn), n // tn)
        in_specs.append(in_spec)
        args.append(w32)
        out_specs.append(out_spec)
        out_shape.append(sds)
    out = pl.pallas_call(
        functools.partial(_proj_kernel, n_a=len(a_parts), normed=ss is not None, residual=residual is not None,
                          scale=scale, next_norm=next_g is not None, n_cast=len(cast), d_in=kp * len(a_parts)),
        grid=(m // tm, n // tn),
        in_specs=in_specs,
        out_specs=out_specs,
        out_shape=out_shape,
        compiler_params=_params("parallel", "arbitrary"),
        name=name,
    )(*args)
    return out if next_g is not None or cast else out[0]


def _ffn_up_kernel(*refs, n_cast, d_in):
    h_ref, wg_ref, wu_ref, ss_ref = refs[:4]
    cast_in = refs[4:4 + n_cast]
    o_ref = refs[4 + n_cast]
    cast_out = refs[5 + n_cast:]
    h = h_ref[...]
    g = _scale_rows(_dot(h, wg_ref[...]), ss_ref, d_in)
    u = _scale_rows(_dot(h, wu_ref[...]), ss_ref, d_in)
    o_ref[...] = (_silu(g) * u).astype(o_ref.dtype)
    for src_ref, dst_ref in zip(cast_in, cast_out):
        dst_ref[...] = src_ref[...].astype(BF16)


def ffn_up(xg, ss, wg, wu, *, cast=(), tm=1024, tn=512):
    m, k = xg.shape
    n = wg.shape[1]
    tm = _tile(m, tm)
    tn = _tile(n, tn)
    wspec = pl.BlockSpec((k, tn), lambda i, j: (0, j))
    in_specs = [pl.BlockSpec((tm, k), lambda i, j: (i, 0)), wspec, wspec, pl.BlockSpec((tm, LANES), lambda i, j: (i, 0))]
    out_specs = [pl.BlockSpec((tm, tn), lambda i, j: (i, j))]
    out_shape = [jax.ShapeDtypeStruct((m, n), BF16)]
    args = [xg, wg, wu, ss]
    for w32, cast_layer in cast:
        in_spec, out_spec, sds = _cast_specs(w32, cast_layer, (m // tm) * (n // tn), n // tn)
        in_specs.append(in_spec)
        args.append(w32)
        out_specs.append(out_spec)
        out_shape.append(sds)
    return pl.pallas_call(
        functools.partial(_ffn_up_kernel, n_cast=len(cast), d_in=k),
        grid=(m // tm, n // tn),
        in_specs=in_specs,
        out_specs=out_specs,
        out_shape=out_shape,
        compiler_params=_params("parallel", "arbitrary"),
        name="ffn_up",
    )(*args)


def swiglu_half_step(x, xg, ss, weights, next_g, next_weights32):
    w_gate, w_up, w_down = weights
    cast_up, cast_down = (), ()
    if next_weights32 is not None:
        (g32, u32, d32), nl = next_weights32
        cast_up, cast_down = ((g32, nl), (u32, nl)), ((d32, nl),)
    act, *next_gu = ffn_up(xg, ss, w_gate, w_up, cast=cast_up)
    out = proj(act, w_down[None], 0, residual=x, scale=0.5, next_g=next_g, cast=cast_down, tn=512, name="ffn_down")
    out = list(out) if isinstance(out, (list, tuple)) else [out]
    x_new = out.pop(0)
    xg_new, ss_new = (out.pop(0), out.pop(0)) if next_g is not None else (None, None)
    return x_new, xg_new, ss_new, tuple(next_gu) + tuple(out)


def _gla_kernel(q_ref, k_ref, v_ref, r_ref, a_ref, wgu_ref, bg_ref, ng_ref, o_ref, st_ref, *, rt):
    C, SB = GLA_CHUNK, GLA_SUB

    @pl.when(pl.program_id(2) == 0)
    def _():
        st_ref[...] = jnp.zeros_like(st_ref)

    zg = _dot(a_ref[...].astype(BF16), wgu_ref[...]) + bg_ref[...]
    lf = (jnp.minimum(zg, 0.0) - jnp.log(1.0 + jnp.exp(-jnp.abs(zg)))) * (GLA_INV_TAU * LOG2_E)

    row = lax.broadcasted_iota(jnp.int32, (rt, GLA_DK), 0)
    rc = row & (C - 1)
    b = lf
    for s in (1, 2, 4, 8, 16, 32):
        b = b + jnp.where(rc >= s, pltpu.roll(b, s, 0), 0.0)

    q = q_ref[...] * (GLA_DK ** -0.5)
    k = k_ref[...]
    v = v_ref[...].astype(BF16)

    nb = rt // SB
    b3 = b.reshape(nb, SB, GLA_DK)
    q3 = q.reshape(nb, SB, GLA_DK)
    k3 = k.reshape(nb, SB, GLA_DK)
    rowc = lax.broadcasted_iota(jnp.int32, (rt, C), 0)
    lane = lax.broadcasted_iota(jnp.int32, (rt, C), 1)
    sub_row = rowc & (SB - 1)
    col_in_sub = lane - ((rowc & (C - 1)) - sub_row)
    causal_col = jnp.where(sub_row >= col_in_sub, col_in_sub, -1)
    a_diag = jnp.zeros((rt, C), F32)
    for j in range(SB):
        bj = jnp.broadcast_to(b3[:, j:j + 1, :], (nb, SB, GLA_DK))
        kj = jnp.broadcast_to(k3[:, j:j + 1, :], (nb, SB, GLA_DK))
        pj = (q3 * kj * jnp.exp2(b3 - bj)).reshape(rt, GLA_DK)
        rs = jnp.sum(pj, axis=-1, keepdims=True)
        a_diag = jnp.where(causal_col == j, rs, a_diag)

    lane_c = lax.broadcasted_iota(jnp.int32, (SB, C), 1)
    outs = []
    for c in range(rt // C):
        sl = slice(c * C, (c + 1) * C)
        bc, qc, kc, vc = b[sl], q[sl], k[sl], v[sl]
        rows = [jnp.zeros((SB, C), F32)]
        for i in range(1, C // SB):
            b0 = bc[i * SB:i * SB + 1]
            qi = qc[i * SB:(i + 1) * SB] * jnp.exp2(bc[i * SB:(i + 1) * SB] - b0)
            ki = kc * jnp.exp2(jnp.minimum(b0 - bc, 0.0))
            ai = _dot_nt(qi.astype(BF16), ki.astype(BF16))
            rows.append(jnp.where(lane_c < i * SB, ai, 0.0))
        a_c = a_diag[sl] + jnp.concatenate(rows, axis=0)
        o_intra = _dot(a_c.astype(BF16), vc)
        st = st_ref[...]
        o_inter = _dot_nt((qc * jnp.exp2(bc)).astype(BF16), st.astype(BF16))
        bl = bc[C - 1:C]
        kd = kc * jnp.exp2(bl - bc)
        st_ref[...] = st * jnp.exp2(bl) + _dot_tn(vc, kd.astype(BF16))
        outs.append(o_inter + o_intra)
    o = jnp.concatenate(outs, axis=0)

    ms = jnp.mean(o * o, axis=-1, keepdims=True)
    y = o * lax.rsqrt(ms + NORM_EPS) * ng_ref[...]
    o_ref[...] = (y * _silu(r_ref[...])).astype(o_ref.dtype)


def gla_mixer(z, za, w_gate_up, b_gate, norm_g, batch, seq, *, rt=512):
    m = z.shape[0]
    rt = _tile(seq, rt)
    nt = seq // rt
    kb = GLA_K_WIDTH // GLA_DK
    vb = 2 * GLA_K_WIDTH // GLA_DV
    rb = vb + GLA_V_WIDTH // GLA_DV
    wgu = jnp.zeros((LANES, GLA_K_WIDTH), BF16).at[:GLA_GATE_RANK].set(w_gate_up.astype(BF16))
    row = lambda b, h, t: b * nt + t
    return pl.pallas_call(
        functools.partial(_gla_kernel, rt=rt),
        grid=(batch, GLA_HEADS, nt),
        in_specs=[
            pl.BlockSpec((rt, GLA_DK), lambda b, h, t: (row(b, h, t), h)),
            pl.BlockSpec((rt, GLA_DK), lambda b, h, t: (row(b, h, t), kb + h)),
            pl.BlockSpec((rt, GLA_DV), lambda b, h, t: (row(b, h, t), vb + h)),
            pl.BlockSpec((rt, GLA_DV), lambda b, h, t: (row(b, h, t), rb + h)),
            pl.BlockSpec((rt, LANES), lambda b, h, t: (row(b, h, t), 0)),
            pl.BlockSpec((LANES, GLA_DK), lambda b, h, t: (0, h)),
            pl.BlockSpec((1, GLA_DK), lambda b, h, t: (0, h)),
            pl.BlockSpec((1, GLA_DV), lambda b, h, t: (0, 0)),
        ],
        out_specs=pl.BlockSpec((rt, GLA_DV), lambda b, h, t: (row(b, h, t), h)),
        out_shape=jax.ShapeDtypeStruct((m, GLA_V_WIDTH), BF16),
        scratch_shapes=[pltpu.VMEM((GLA_DV, GLA_DK), F32)],
        compiler_params=_params("parallel", "parallel", "arbitrary"),
        name="gla",
    )(z, z, z, z, za, wgu, b_gate.reshape(1, GLA_K_WIDTH), norm_g.reshape(1, GLA_DV))


def _sg_kernel(u_ref, s_ref, lng_ref, lnb_ref, ws_ref, bs_ref, o_ref, *, rt):
    s = _gelu_tanh(s_ref[...])
    mu = jnp.mean(s, axis=-1, keepdims=True)
    sc = s - mu
    var = jnp.mean(sc * sc, axis=-1, keepdims=True)
    sn = (sc * lax.rsqrt(var + LN_EPS) * lng_ref[...] + lnb_ref[...]).astype(BF16)
    u = _gelu_tanh(u_ref[...])
    ri = lax.broadcasted_iota(jnp.int32, (SG_CHUNK, SG_CHUNK), 0)
    ci = lax.broadcasted_iota(jnp.int32, (SG_CHUNK, SG_CHUNK), 1)
    for g in range(SG_GROUPS):
        w = jnp.where(ci <= ri, ws_ref[g], 0.0).astype(BF16)
        cols = slice(g * SG_GROUP_WIDTH, (g + 1) * SG_GROUP_WIDTH)
        for c in range(rt // SG_CHUNK):
            rows = slice(c * SG_CHUNK, (c + 1) * SG_CHUNK)
            mixed = _dot(w, sn[rows, cols]) + bs_ref[g]
            o_ref[rows, cols] = (u[rows, cols] * mixed).astype(o_ref.dtype)


def sg_mixer(z, ln_g, ln_b, w_s, b_s, *, rt=512):
    m = z.shape[0]
    rt = _tile(m, rt)
    return pl.pallas_call(
        functools.partial(_sg_kernel, rt=rt),
        grid=(m // rt,),
        in_specs=[
            pl.BlockSpec((rt, SG_WIDTH), lambda i: (i, 0)),
            pl.BlockSpec((rt, SG_WIDTH), lambda i: (i, 1)),
            pl.BlockSpec((1, SG_WIDTH), lambda i: (0, 0)),
            pl.BlockSpec((1, SG_WIDTH), lambda i: (0, 0)),
            pl.BlockSpec((SG_GROUPS, SG_CHUNK, SG_CHUNK), lambda i: (0, 0, 0)),
            pl.BlockSpec((SG_GROUPS, SG_CHUNK, 1), lambda i: (0, 0, 0)),
        ],
        out_specs=pl.BlockSpec((rt, SG_WIDTH), lambda i: (i, 0)),
        out_shape=jax.ShapeDtypeStruct((m, SG_WIDTH), BF16),
        compiler_params=_params("parallel"),
        name="spatial_gate",
    )(z, z, ln_g.reshape(1, SG_WIDTH), ln_b.reshape(1, SG_WIDTH), w_s, b_s.reshape(SG_GROUPS, SG_CHUNK, 1))


def split_ab_w_in(w_in):
    layers, d, _ = w_in.shape
    a0 = 2 * GLA_K_WIDTH + 2 * GLA_V_WIDTH
    w_a = jnp.concatenate(
        [w_in[:, :, a0:a0 + GLA_GATE_RANK], jnp.zeros((layers, d, LANES - GLA_GATE_RANK), BF16)], axis=2)
    return w_in, w_a, w_in[:, :, a0 + GLA_GATE_RANK:]


def mixer_gla_sg(x, xg, ss, layer, batch, seq, w_in_parts, w_gate_up, b_gate, gla_norm_g,
                 sg_ln_g, sg_ln_b, sg_w_s, sg_b_s, w_out, next_g):
    w, w_a, w_us = w_in_parts
    z = proj(xg, w, layer, ss=ss, n_cols=2 * GLA_K_WIDTH + 2 * GLA_V_WIDTH, name="ab_in_proj")
    za = proj(xg, w_a, layer, ss=ss, name="ab_gate_proj")
    zus = proj(xg, w_us, layer, ss=ss, name="ab_sg_proj")
    y_gla = gla_mixer(z, za, w_gate_up, b_gate, gla_norm_g, batch, seq)
    y_sg = sg_mixer(zus, sg_ln_g, sg_ln_b, sg_w_s, sg_b_s)
    return proj((y_gla, y_sg), w_out, layer, residual=x, next_g=next_g, tn=512, name="ab_out_proj")


def _cross_kernel(xg_ref, ss_ref, x_ref, wq_ref, wo_ref, k_ref, v_ref, g_ref, o_ref, xg_out_ref, ss_out_ref, *, d_in):
    q = _scale_rows(_dot(xg_ref[...], wq_ref[...]), ss_ref, d_in).astype(BF16)
    heads = []
    for hd in range(X_HEADS):
        cols = slice(hd * X_HEAD_DIM, (hd + 1) * X_HEAD_DIM)
        s = _dot_nt(q[:, cols], k_ref[:, cols]) * X_SCALE
        p = jnp.exp(s - jnp.max(s, axis=-1, keepdims=True))
        p = p / jnp.sum(p, axis=-1, keepdims=True)
        heads.append(_dot(p.astype(BF16), v_ref[:, cols]).astype(BF16))
    x_new = x_ref[...] + _dot(jnp.concatenate(heads, axis=1), wo_ref[...])
    o_ref[...] = x_new
    xg_out_ref[...] = (x_new * g_ref[...]).astype(BF16)
    ss_out_ref[...] = _row_sumsq(x_new)


def cross_attention(x, xg, ss, kv, batch, seq, w_q, w_o, layer, next_g, *, tm=256):
    m, d = xg.shape
    mem = kv.shape[0] // batch
    tm = _tile(seq, tm)
    nt = seq // tm
    rows = lambda b, t: (b * nt + t, 0)
    return pl.pallas_call(
        functools.partial(_cross_kernel, d_in=d),
        grid=(batch, nt),
        in_specs=[
            pl.BlockSpec((tm, d), rows),
            pl.BlockSpec((tm, LANES), rows),
            pl.BlockSpec((tm, d), rows),
            pl.BlockSpec((None, d, X_WIDTH), lambda b, t: (layer, 0, 0)),
            pl.BlockSpec((None, X_WIDTH, d), lambda b, t: (layer, 0, 0)),
            pl.BlockSpec((mem, X_WIDTH), lambda b, t: (b, 0)),
            pl.BlockSpec((mem, X_WIDTH), lambda b, t: (b, 1)),
            pl.BlockSpec((1, d), lambda b, t: (0, 0)),
        ],
        out_specs=[pl.BlockSpec((tm, d), rows), pl.BlockSpec((tm, d), rows), pl.BlockSpec((tm, LANES), rows)],
        out_shape=[jax.ShapeDtypeStruct((m, d), F32), jax.ShapeDtypeStruct((m, d), BF16),
                   jax.ShapeDtypeStruct((m, LANES), F32)],
        compiler_params=_params("parallel", "parallel"),
        name="cross_attn",
    )(xg, ss, x, w_q, w_o, kv, kv, next_g.reshape(1, d))


def _rope_tables(seq):
    inv_freq = jnp.power(ROPE_THETA, -jnp.arange(ROPE_HALF, dtype=F32) / ROPE_HALF)
    ang = jnp.arange(seq).astype(F32)[:, None] * inv_freq[None, :]
    cos, sin = jnp.cos(ang), jnp.sin(ang)
    rest = NSA_HEAD_DIM - ROPE_DIM
    c = jnp.concatenate([cos, cos, jnp.ones((seq, rest), F32)], axis=1)
    s_up = jnp.concatenate([jnp.zeros_like(sin), sin, jnp.zeros((seq, rest), F32)], axis=1)
    s_dn = jnp.concatenate([-sin, jnp.zeros_like(sin), jnp.zeros((seq, rest), F32)], axis=1)
    return c, s_up, s_dn


def _rope(x, c, s_up, s_dn, heads):
    width = heads * NSA_HEAD_DIM
    rep = lambda t: jnp.concatenate([t] * heads, axis=1)
    return (x * rep(c) + pltpu.roll(x, ROPE_HALF, 1) * rep(s_up)
            + pltpu.roll(x, width - ROPE_HALF, 1) * rep(s_dn))


def _nsa_kv_kernel(ks_ref, vs_ref, kw_ref, vw_ref, c_ref, su_ref, sd_ref, oks_ref, ovs_ref, okw_ref, ovw_ref):
    c, su, sd = c_ref[...], su_ref[...], sd_ref[...]
    oks_ref[...] = _rope(ks_ref[...], c, su, sd, NSA_KV_GROUPS).astype(BF16)
    okw_ref[...] = _rope(kw_ref[...], c, su, sd, NSA_KV_GROUPS).astype(BF16)
    ones = jnp.ones((ks_ref.shape[0], NSA_HEAD_DIM), BF16)
    for v_ref, o_ref in ((vs_ref, ovs_ref), (vw_ref, ovw_ref)):
        v = v_ref[...].astype(BF16)
        o_ref[...] = jnp.concatenate(
            [piece for g in range(NSA_KV_GROUPS) for piece in (v[:, g * NSA_HEAD_DIM:(g + 1) * NSA_HEAD_DIM], ones)],
            axis=1)


def nsa_kv_prep(z, tables, batch, seq, *, rt=1024):
    m = z.shape[0]
    rt = _tile(seq, rt)
    nt = seq // rt
    cb = NSA_Q_WIDTH // NSA_KV_WIDTH + 2
    zspec = lambda j: pl.BlockSpec((rt, NSA_KV_WIDTH), lambda b, t: (b * nt + t, cb + j))
    tspec = pl.BlockSpec((rt, NSA_HEAD_DIM), lambda b, t: (t, 0))
    kspec = pl.BlockSpec((rt, NSA_KV_WIDTH), lambda b, t: (b * nt + t, 0))
    vspec = pl.BlockSpec((rt, 2 * NSA_KV_WIDTH), lambda b, t: (b * nt + t, 0))
    ksds = jax.ShapeDtypeStruct((m, NSA_KV_WIDTH), BF16)
    vsds = jax.ShapeDtypeStruct((m, 2 * NSA_KV_WIDTH), BF16)
    return pl.pallas_call(
        _nsa_kv_kernel,
        grid=(batch, nt),
        in_specs=[zspec(0), zspec(1), zspec(2), zspec(3), tspec, tspec, tspec],
        out_specs=[kspec, vspec, kspec, vspec],
        out_shape=[ksds, vsds, ksds, vsds],
        compiler_params=_params("parallel", "parallel"),
        name="nsa_kv_prep",
    )(z, z, z, z, *tables)


def _nsa_cmp_kernel(x_ref, pos_ref, w1_ref, b1_ref, w2_ref, b2_ref, o_ref, *, nblk):
    half = NSA_CMP_STRIDE * NSA_HEAD_DIM
    pos_bias = _dot(pos_ref[...], w1_ref[...])[0:1] + b1_ref[...]
    row = lax.broadcasted_iota(jnp.int32, (nblk, NSA_HEAD_DIM), 0)
    xg = jnp.concatenate(
        [x_ref[pl.ds(l, nblk, stride=NSA_CMP_STRIDE), :] for l in range(NSA_CMP_STRIDE)], axis=1).astype(BF16)
    first = _dot(xg, w1_ref[:half])
    second = _dot(xg, w1_ref[half:])
    pre = first + pltpu.roll(second, nblk - 1, 0) + pos_bias
    out = _dot(_silu(pre).astype(BF16), w2_ref[...]) + b2_ref[...]
    o_ref[...] = jnp.where(row < nblk - 1, out, 0.0).astype(o_ref.dtype)


def nsa_compress(z, batch, seq, cmp_pos, cmp_w1, cmp_b1, cmp_w2, cmp_b2):
    nblk = seq // NSA_CMP_STRIDE
    cb = NSA_Q_WIDTH // NSA_HEAD_DIM
    flat = NSA_CMP_LEN * NSA_HEAD_DIM
    pos = jnp.zeros((2, 8, flat), BF16).at[:, 0].set(cmp_pos.reshape(2, flat).astype(BF16))
    return pl.pallas_call(
        functools.partial(_nsa_cmp_kernel, nblk=nblk),
        grid=(2, batch, NSA_KV_GROUPS),
        in_specs=[
            pl.BlockSpec((seq, NSA_HEAD_DIM), lambda s, b, g: (b, cb + s * NSA_KV_GROUPS + g)),
            pl.BlockSpec((None, 8, flat), lambda s, b, g: (s, 0, 0)),
            pl.BlockSpec((None, flat, NSA_HEAD_DIM), lambda s, b, g: (s, 0, 0)),
            pl.BlockSpec((None, 1, NSA_HEAD_DIM), lambda s, b, g: (s, 0, 0)),
            pl.BlockSpec((None, NSA_HEAD_DIM, NSA_HEAD_DIM), lambda s, b, g: (s, 0, 0)),
            pl.BlockSpec((None, 1, NSA_HEAD_DIM), lambda s, b, g: (s, 0, 0)),
        ],
        out_specs=pl.BlockSpec((None, None, None, nblk, NSA_HEAD_DIM), lambda s, b, g: (s, b, g, 0, 0)),
        out_shape=jax.ShapeDtypeStruct((2, batch, NSA_KV_GROUPS, nblk, NSA_HEAD_DIM), BF16),
        compiler_params=_params("parallel", "parallel", "parallel"),
        name="nsa_compress",
    )(z, pos, cmp_w1.astype(BF16), cmp_b1.reshape(2, 1, NSA_HEAD_DIM), cmp_w2.astype(BF16),
      cmp_b2.reshape(2, 1, NSA_HEAD_DIM))


def _row_reduce(s, op, lane_op):
    acc = s[:, :LANES]
    for c in range(1, s.shape[1] // LANES):
        acc = op(acc, s[:, c * LANES:(c + 1) * LANES])
    return lane_op(acc, axis=-1, keepdims=True)


def _nsa_attn_kernel(q_ref, gate_ref, c_ref, su_ref, sd_ref, kc_ref, vc_ref, ks_ref, vs_ref, kw_ref, vw_ref,
                     ovl_ref, blk_ref, spread_ref, o_ref,
                     qp_ref, qr_ref, s_ref, p_ref, bias_ref, m_ref, acc_ref, psum_ref, sel_ref, mix_ref, gx_ref,
                     *, tq, tk, seq, top_n):
    P, HD, SB, W, SL = NSA_HPG, NSA_HEAD_DIM, NSA_SEL_BLOCK, NSA_WINDOW, NSA_SLAB
    R = P * tq
    nsel = seq // SB
    ncmp = seq // NSA_CMP_STRIDE
    t0 = pl.program_id(2) * tq

    qf = q_ref[...]
    q_scale = NSA_SCALE * LOG2_E
    qr = _rope(qf, c_ref[...], su_ref[...], sd_ref[...], P)
    for p in range(P):
        qp_ref[p * tq:(p + 1) * tq] = (qf[:, p * HD:(p + 1) * HD] * q_scale).astype(BF16)
        qr_ref[p * tq:(p + 1) * tq] = (qr[:, p * HD:(p + 1) * HD] * q_scale).astype(BF16)

    def reset_state():
        m_ref[...] = jnp.full_like(m_ref, MASK_NEG)
        acc_ref[...] = jnp.zeros_like(acc_ref)

    def softmax_tile(k_t, v_t, n):
        for g0 in range(0, R, NSA_GROUP_ROWS):
            grp = slice(g0, g0 + NSA_GROUP_ROWS)
            s_grp = _dot_nt(qr_ref[grp], k_t)
            alphas, probs = [], []
            for r0 in range(g0, g0 + NSA_GROUP_ROWS, SL):
                rows = slice(r0, r0 + SL)
                b0 = r0 % tq
                s = s_grp[r0 - g0:r0 - g0 + SL] + bias_ref[b0:b0 + SL, :n]
                m_old = m_ref[rows]
                m_new = jnp.maximum(m_old, _row_reduce(s, jnp.maximum, jnp.max))
                alphas.append(jnp.exp2(m_old - m_new))
                m_ref[rows] = m_new
                probs.append(jnp.exp2(s - m_new).astype(BF16))
            alpha = jnp.concatenate(alphas, axis=0)
            acc_ref[grp] = alpha * acc_ref[grp] + _dot(jnp.concatenate(probs, axis=0), v_t)

    t_c = t0 + lax.broadcasted_iota(jnp.int32, (tq, ncmp), 0)
    end_c = lax.broadcasted_iota(jnp.int32, (tq, ncmp), 1) * NSA_CMP_STRIDE + (NSA_CMP_LEN - 1)
    bias_ref[:, :ncmp] = jnp.where(end_c <= t_c, 0.0, MASK_NEG)
    s_ref[:, :ncmp] = _dot_nt(qp_ref[...], kc_ref[...])
    psum_ref[...] = jnp.zeros_like(psum_ref)
    for j in range(R // SL):
        rows = slice(j * SL, (j + 1) * SL)
        b0 = (j * SL) % tq
        bias = bias_ref[b0:b0 + SL, :ncmp]
        s = s_ref[rows, :ncmp] + bias
        e = jnp.where(bias == 0.0, jnp.exp2(s - _row_reduce(s, jnp.maximum, jnp.max)), 0.0)
        pn = e / jnp.maximum(_row_reduce(e, jnp.add, jnp.sum), 1e-30)
        p_ref[rows, :ncmp] = pn.astype(BF16)
        psum_ref[b0:b0 + SL] += pn
    o_c = _dot(p_ref[:, :ncmp], vc_ref[...])
    gates = _sigmoid(gate_ref[...])
    g_hi = gates.astype(BF16)
    g_lo = (gates - g_hi.astype(F32)).astype(BF16)
    gx_ref[...] = _dot(g_hi, spread_ref[...]) + _dot(g_lo, spread_ref[...])
    gate = lambda p, j: gx_ref[:, (3 * p + j) * LANES:(3 * p + j + 1) * LANES]
    for p in range(P):
        rows = slice(p * tq, (p + 1) * tq)
        mix_ref[rows] = gate(p, 0) * o_c[rows]

    p_sum = psum_ref[...]
    p_hi = p_sum.astype(BF16)
    p_lo = (p_sum - p_hi.astype(F32)).astype(BF16)
    imp_t = _dot_nt(ovl_ref[...], p_hi) + _dot_nt(ovl_ref[...], p_lo)

    blk = lax.broadcasted_iota(jnp.int32, (nsel, tq), 0)
    cur = (t0 + lax.broadcasted_iota(jnp.int32, (nsel, tq), 1)) // SB
    forced = (blk == 0) | (blk == cur) | (blk == cur - 1)
    score = jnp.where(blk <= cur, imp_t + jnp.where(forced, NSA_FORCE_BONUS, 0.0), -jnp.inf)
    rank = jnp.zeros((nsel, tq), jnp.int32)
    for sp in range(nsel):
        other = jnp.broadcast_to(score[sp:sp + 1, :], (nsel, tq))
        rank = rank + jnp.where(blk > sp, jnp.where(other >= score, 1, 0), jnp.where(other > score, 1, 0))
    sel_ref[...] = jnp.where((rank < top_n) & (blk <= cur), 1.0, 0.0).T.astype(BF16)

    wlen = W + tq
    base = pl.multiple_of(jnp.clip(t0 - W, 0, seq - wlen), tq)
    delta = (t0 + lax.broadcasted_iota(jnp.int32, (tq, wlen), 0)) - (base + lax.broadcasted_iota(jnp.int32, (tq, wlen), 1))
    bias_ref[:, :wlen] = jnp.where((delta >= 0) & (delta < W), 0.0, MASK_NEG)
    reset_state()
    softmax_tile(kw_ref[pl.ds(base, wlen), :], vw_ref[pl.ds(base, wlen), :], wlen)
    for p in range(P):
        rows = slice(p * tq, (p + 1) * tq)
        o_w = acc_ref[rows, :HD] / acc_ref[rows, HD:]
        mix_ref[rows] += gate(p, 2) * o_w

    t_k = t0 + lax.broadcasted_iota(jnp.int32, (tq, tk), 0)
    lane_k = lax.broadcasted_iota(jnp.int32, (tq, tk), 1)
    reset_state()

    def sel_step(kt, carry):
        k0 = pl.multiple_of(kt * tk, tk)
        picked = _dot(sel_ref[...], blk_ref[kt])
        bias_ref[:, :tk] = jnp.where((picked > 0.5) & (lane_k + k0 <= t_k), 0.0, MASK_NEG)
        softmax_tile(ks_ref[pl.ds(k0, tk), :], vs_ref[pl.ds(k0, tk), :], tk)
        return carry

    lax.fori_loop(0, (t0 + tq + tk - 1) // tk, sel_step, 0)
    for p in range(P):
        rows = slice(p * tq, (p + 1) * tq)
        o_s = acc_ref[rows, :HD] / acc_ref[rows, HD:]
        o_ref[:, p * HD:(p + 1) * HD] = (mix_ref[rows] + gate(p, 1) * o_s).astype(o_ref.dtype)


def nsa_attention(z, zg, kv, cmp, tables, batch, seq, *, tq=256, tk=1024):
    m = z.shape[0]
    ks, vs, kw, vw = kv
    tq = _tile(seq, tq)
    tk = _tile(seq, tk)
    nt = seq // tq
    nsel = seq // NSA_SEL_BLOCK
    ncmp = seq // NSA_CMP_STRIDE
    gw = NSA_HPG * NSA_HEAD_DIM
    c_start = jnp.arange(ncmp) * NSA_CMP_STRIDE
    s_start = jnp.arange(nsel) * NSA_SEL_BLOCK
    ovl = ((c_start[None, :] <= s_start[:, None] + NSA_SEL_BLOCK - 1)
           & (c_start[None, :] + NSA_CMP_LEN - 1 >= s_start[:, None])
           & (jnp.arange(ncmp)[None, :] < ncmp - 1)).astype(BF16)
    blk_of_key = (jnp.arange(seq)[None, :] // NSA_SEL_BLOCK == jnp.arange(nsel)[:, None]).astype(BF16)
    blk_of_key = blk_of_key.reshape(nsel, seq // tk, tk).transpose(1, 0, 2)
    n_gate = 3 * NSA_HPG
    spread = (jnp.arange(LANES)[:, None] == jnp.arange(n_gate * LANES)[None, :] // LANES).astype(BF16)
    kspec =pl.BlockSpec((seq, NSA_HEAD_DIM), lambda b, g, i: (b, g))
    vspec = pl.BlockSpec((seq, 2 * NSA_HEAD_DIM), lambda b, g, i: (b, g))
    cmpspec = lambda s: pl.BlockSpec((None, None, None, ncmp, NSA_HEAD_DIM), lambda b, g, i: (s, b, g, 0, 0))
    tspec = pl.BlockSpec((tq, NSA_HEAD_DIM), lambda b, g, i: (i, 0))
    rows = NSA_HPG * tq
    width = max(tk, NSA_WINDOW + tq, ncmp)
    return pl.pallas_call(
        functools.partial(_nsa_attn_kernel, tq=tq, tk=tk, seq=seq, top_n=min(NSA_TOPK, nsel)),
        grid=(batch, NSA_KV_GROUPS, nt),
        in_specs=[
            pl.BlockSpec((tq, gw), lambda b, g, i: (b * nt + i, g)),
            pl.BlockSpec((tq, LANES), lambda b, g, i: (b * nt + i, g)),
            tspec, tspec, tspec,
            cmpspec(0), cmpspec(1),
            kspec, vspec, kspec, vspec,
            pl.BlockSpec((nsel, ncmp), lambda b, g, i: (0, 0)),
            pl.BlockSpec((seq // tk, nsel, tk), lambda b, g, i: (0, 0, 0)),
            pl.BlockSpec((LANES, n_gate * LANES), lambda b, g, i: (0, 0)),
        ],
        out_specs=pl.BlockSpec((tq, gw), lambda b, g, i: (b * nt + i, g)),
        out_shape=jax.ShapeDtypeStruct((m, NSA_Q_WIDTH), BF16),
        scratch_shapes=[
            pltpu.VMEM((rows, NSA_HEAD_DIM), BF16),
            pltpu.VMEM((rows, NSA_HEAD_DIM), BF16),
            pltpu.VMEM((rows, width), F32),
            pltpu.VMEM((rows, width), BF16),
            pltpu.VMEM((tq, width), F32),
            pltpu.VMEM((rows, 1), F32),
            pltpu.VMEM((rows, 2 * NSA_HEAD_DIM), F32),
            pltpu.VMEM((tq, ncmp), F32),
            pltpu.VMEM((tq, nsel), BF16),
            pltpu.VMEM((rows, NSA_HEAD_DIM), F32),
            pltpu.VMEM((tq, n_gate * LANES), F32),
        ],
        compiler_params=_params("parallel", "parallel", "arbitrary"),
        name="nsa_attn",
    )(z, zg, *tables, cmp, cmp, ks, vs, kw, vw, ovl, blk_of_key, spread)


def split_nsa_w_in(w_in):
    layers, d, _ = w_in.shape
    c0 = NSA_Q_WIDTH + 6 * NSA_KV_WIDTH
    per_group = 3 * NSA_HPG
    cols = []
    for g in range(NSA_KV_GROUPS):
        cols += [w_in[:, :, c0 + g * per_group:c0 + (g + 1) * per_group],
                 jnp.zeros((layers, d, LANES - per_group), BF16)]
    return w_in, jnp.concatenate(cols, axis=2)


def mixer_nsa(x, xg, ss, layer, batch, seq, w_in_parts, cmp_pos, cmp_w1, cmp_b1, cmp_w2, cmp_b2, w_out, next_g):
    w, w_gates = w_in_parts
    z = proj(xg, w, layer, ss=ss, n_cols=NSA_Q_WIDTH + 6 * NSA_KV_WIDTH, name="nsa_in_proj")
    zg = proj(xg, w_gates, layer, ss=ss, name="nsa_gate_proj")
    tables = _rope_tables(seq)
    kv = nsa_kv_prep(z, tables, batch, seq)
    cmp = nsa_compress(z, batch, seq, cmp_pos, cmp_w1, cmp_b1, cmp_w2, cmp_b2)
    o = nsa_attention(z, zg, kv, cmp, tables, batch, seq)
    return proj(o, w_out, layer, residual=x, next_g=next_g, tn=512, name="nsa_out_proj")


def kernel(x, mem, mem_norm_g,
           ffn1_norm_g, ffn1_w_gate, ffn1_w_up, ffn1_w_down,
           mix_norm_g,
           ab_w_in, gla_w_gate_up, gla_b_gate, gla_norm_g, sg_ln_g, sg_ln_b, sg_w_s, sg_b_s, ab_w_out,
           nsa_w_in, nsa_cmp_pos, nsa_cmp_w1, nsa_cmp_b1, nsa_cmp_w2, nsa_cmp_b2, nsa_w_out,
           cross_norm_g, cross_w_q, cross_w_kv, cross_w_o,
           ffn2_norm_g, ffn2_w_gate, ffn2_w_up, ffn2_w_down,
           final_norm_g):
    batch, seq, d = x.shape
    depth = ffn1_norm_g.shape[0]
    bf = lambda w: w.astype(BF16)
    ffn1_w32 = (ffn1_w_gate, ffn1_w_up, ffn1_w_down)
    ffn2_w32 = (ffn2_w_gate, ffn2_w_up, ffn2_w_down)
    ffn_w = tuple(bf(w[0]) for w in ffn1_w32)
    ab_w_in_parts, ab_w_out = split_ab_w_in(bf(ab_w_in)), bf(ab_w_out)
    nsa_w_in_parts, nsa_w_out = split_nsa_w_in(bf(nsa_w_in)), bf(nsa_w_out)
    cross_w_q, cross_w_kv, cross_w_o = bf(cross_w_q), bf(cross_w_kv), bf(cross_w_o)

    mem_n = rmsnorm(mem.reshape(-1, d), mem_norm_g)
    x = x.reshape(batch * seq, d)
    xg, ss = norm_prep(x, ffn1_norm_g[0])
    for i in range(depth):
        x, xg, ss, ffn_w = swiglu_half_step(x, xg, ss, ffn_w, mix_norm_g[i], (ffn2_w32, i))
        j = i // 2
        if i % 2 == 0:
            x, xg, ss = mixer_gla_sg(x, xg, ss, j, batch, seq, ab_w_in_parts, gla_w_gate_up[j], gla_b_gate[j],
                                     gla_norm_g[j], sg_ln_g[j], sg_ln_b[j], sg_w_s[j], sg_b_s[j], ab_w_out,
                                     cross_norm_g[i])
        else:
            x, xg, ss = mixer_nsa(x, xg, ss, j, batch, seq, nsa_w_in_parts, nsa_cmp_pos[j], nsa_cmp_w1[j],
                                  nsa_cmp_b1[j], nsa_cmp_w2[j], nsa_cmp_b2[j], nsa_w_out, cross_norm_g[i])
        kv = proj(mem_n, cross_w_kv, i, out_dtype=BF16, name="cross_kv_proj")
        x, xg, ss = cross_attention(x, xg, ss, kv, batch, seq, cross_w_q, cross_w_o, i, ffn2_norm_g[i])
        last = i + 1 == depth
        x, xg, ss, ffn_w = swiglu_half_step(x, xg, ss, ffn_w, None if last else ffn1_norm_g[i + 1],
                                            None if last else (ffn1_w32, i + 1))
    return rmsnorm(x, final_norm_g, out_dtype=F32).reshape(batch, seq, d)
```

```python
import functools

import jax
import jax.numpy as jnp
from jax import lax
from jax.experimental import pallas as pl
from jax.experimental.pallas import tpu as pltpu

F32 = jnp.float32
BF16 = jnp.bfloat16

NORM_EPS = 1e-6
LN_EPS = 1e-5
ROPE_THETA = 500000.0

GLA_HEADS = 8
GLA_DK = 128
GLA_DV = 256
GLA_K_WIDTH = GLA_HEADS * GLA_DK
GLA_V_WIDTH = GLA_HEADS * GLA_DV
GLA_GATE_RANK = 16
GLA_INV_TAU = 1.0 / 16.0
GLA_CHUNK = 64
GLA_SUB = 16
SG_WIDTH = 2048
SG_GROUPS = 4
SG_GROUP_WIDTH = SG_WIDTH // SG_GROUPS
SG_CHUNK = 128

NSA_HEAD_DIM = 128
NSA_KV_GROUPS = 4
NSA_HPG = 8
NSA_Q_WIDTH = NSA_KV_GROUPS * NSA_HPG * NSA_HEAD_DIM
NSA_KV_WIDTH = NSA_KV_GROUPS * NSA_HEAD_DIM
NSA_CMP_STRIDE = 16
NSA_CMP_LEN = 32
NSA_SEL_BLOCK = 64
NSA_TOPK = 16
NSA_WINDOW = 512
NSA_FORCE_BONUS = 1e4
NSA_SCALE = NSA_HEAD_DIM ** -0.5
ROPE_DIM = NSA_HEAD_DIM // 4
ROPE_HALF = ROPE_DIM // 2
NSA_SLAB = 64
NSA_GROUP_ROWS = 256
LOG2_E = 1.4426950408889634

X_HEADS = 4
X_HEAD_DIM = 128
X_WIDTH = X_HEADS * X_HEAD_DIM
X_SCALE = X_HEAD_DIM ** -0.5

LANES = 128
MASK_NEG = -1e30
VMEM_LIMIT_BYTES = 56 * 1024 * 1024


def _params(*sem):
    return pltpu.CompilerParams(dimension_semantics=sem, vmem_limit_bytes=VMEM_LIMIT_BYTES)


def _dot(a, b):
    return jnp.dot(a, b, preferred_element_type=F32)


def _dot_nt(a, b):
    return lax.dot_general(a, b, (((1,), (1,)), ((), ())), preferred_element_type=F32)


def _dot_tn(a, b):
    return lax.dot_general(a, b, (((0,), (0,)), ((), ())), preferred_element_type=F32)


def _sigmoid(x):
    return 1.0 / (1.0 + jnp.exp(-x))


def _silu(x):
    return x * _sigmoid(x)


def _gelu_tanh(x):
    c = 0.7978845608028654
    return x * (0.5 * (1.0 + jnp.tanh(c * (x + 0.044715 * (x * x * x)))))


def _tile(n, pref):
    if n <= pref:
        return n
    t = pref
    while n % t:
        t //= 2
    return t


def _rmsnorm_kernel(x_ref, g_ref, o_ref):
    x = x_ref[...]
    ms = jnp.mean(x * x, axis=-1, keepdims=True)
    o_ref[...] = (x * lax.rsqrt(ms + NORM_EPS) * g_ref[...]).astype(o_ref.dtype)


def rmsnorm(x, g, out_dtype=BF16):
    m, d = x.shape
    tm = _tile(m, 256)
    return pl.pallas_call(
        _rmsnorm_kernel,
        grid=(m // tm,),
        in_specs=[pl.BlockSpec((tm, d), lambda i: (i, 0)), pl.BlockSpec((1, d), lambda i: (0, 0))],
        out_specs=pl.BlockSpec((tm, d), lambda i: (i, 0)),
        out_shape=jax.ShapeDtypeStruct((m, d), out_dtype),
        compiler_params=_params("parallel"),
        name="rmsnorm",
    )(x, g.reshape(1, d))


def _row_sumsq(x):
    return jnp.broadcast_to(jnp.sum(x * x, axis=-1, keepdims=True), (x.shape[0], LANES))


def _norm_prep_kernel(x_ref, g_ref, xg_ref, ss_ref):
    x = x_ref[...]
    xg_ref[...] = (x * g_ref[...]).astype(BF16)
    ss_ref[...] = _row_sumsq(x)


def norm_prep(x, g):
    m, d = x.shape
    tm = _tile(m, 256)
    return pl.pallas_call(
        _norm_prep_kernel,
        grid=(m // tm,),
        in_specs=[pl.BlockSpec((tm, d), lambda i: (i, 0)), pl.BlockSpec((1, d), lambda i: (0, 0))],
        out_specs=[pl.BlockSpec((tm, d), lambda i: (i, 0)), pl.BlockSpec((tm, LANES), lambda i: (i, 0))],
        out_shape=[jax.ShapeDtypeStruct((m, d), BF16), jax.ShapeDtypeStruct((m, LANES), F32)],
        compiler_params=_params("parallel"),
        name="norm_prep",
    )(x, g.reshape(1, d))


def _scale_rows(acc, ss_ref, d):
    rstd = lax.rsqrt(ss_ref[...] / d + NORM_EPS)
    return acc * jnp.concatenate([rstd] * (acc.shape[1] // LANES), axis=1)


def _proj_kernel(*refs, n_a, normed, residual, scale, next_norm, n_cast, d_in):
    it = iter(refs)
    a_refs = [next(it) for _ in range(n_a)]
    w_refs = [next(it) for _ in range(n_a)]
    ss_ref = next(it) if normed else None
    r_ref = next(it) if residual else None
    g_ref = next(it) if next_norm else None
    cast_in = [next(it) for _ in range(n_cast)]
    o_ref = next(it)
    acc = _dot(a_refs[0][...], w_refs[0][...])
    for a_ref, w_ref in zip(a_refs[1:], w_refs[1:]):
        acc = acc + _dot(a_ref[...], w_ref[...])
    if normed:
        acc = _scale_rows(acc, ss_ref, d_in)
    if residual:
        acc = r_ref[...] + scale * acc
    o_ref[...] = acc.astype(o_ref.dtype)
    if next_norm:
        xg_ref, ss_out_ref = next(it), next(it)
        xg_ref[...] = (acc * g_ref[...]).astype(BF16)
        part = _row_sumsq(acc)

        @pl.when(pl.program_id(1) == 0)
        def _():
            ss_out_ref[...] = part

        @pl.when(pl.program_id(1) != 0)
        def _():
            ss_out_ref[...] += part
    for src_ref in cast_in:
        next(it)[...] = src_ref[...].astype(BF16)


def _cast_specs(w32, layer, n_steps, nj):
    _, k, n = w32.shape
    nblk = n_steps
    while k % nblk or (k // nblk) % 16:
        nblk -= 1
    rb = k // nblk
    blk = lambda i, j: jnp.minimum(i * nj + j, nblk - 1)
    return (pl.BlockSpec((None, rb, n), lambda i, j: (layer, blk(i, j), 0)),
            pl.BlockSpec((rb, n), lambda i, j: (blk(i, j), 0)),
            jax.ShapeDtypeStruct((k, n), BF16))


def proj(a, w, layer, *, ss=None, residual=None, scale=1.0, next_g=None, n_cols=None, out_dtype=F32,
         cast=(), tm=1024, tn=1024, name="proj"):
    a_parts = a if isinstance(a, tuple) else (a,)
    m, kp = a_parts[0].shape
    n = w.shape[2] if n_cols is None else n_cols
    tm = _tile(m, tm)
    tn = _tile(n, tn)
    in_specs = [pl.BlockSpec((tm, kp), lambda i, j: (i, 0)) for _ in a_parts]
    in_specs += [pl.BlockSpec((None, kp, tn), lambda i, j, r=r: (layer, r, j)) for r in range(len(a_parts))]
    args = list(a_parts) + [w] * len(a_parts)
    if ss is not None:
        in_specs.append(pl.BlockSpec((tm, LANES), lambda i, j: (i, 0)))
        args.append(ss)
    if residual is not None:
        in_specs.append(pl.BlockSpec((tm, tn), lambda i, j: (i, j)))
        args.append(residual)
        out_dtype = F32
    out_specs = [pl.BlockSpec((tm, tn), lambda i, j: (i, j))]
    out_shape = [jax.ShapeDtypeStruct((m, n), out_dtype)]
    if next_g is not None:
        in_specs.append(pl.BlockSpec((1, tn), lambda i, j: (0, j)))
        args.append(next_g.reshape(1, n))
        out_specs += [pl.BlockSpec((tm, tn), lambda i, j: (i, j)), pl.BlockSpec((tm, LANES), lambda i, j: (i, 0))]
        out_shape += [jax.ShapeDtypeStruct((m, n), BF16), jax.ShapeDtypeStruct((m, LANES), F32)]
    for w32, cast_layer in cast:
        in_spec, out_spec, sds = _cast_specs(w32, cast_layer, (m // tm) * (n // tn), n // tn)
        in_specs.append(in_spec)
        args.append(w32)
        out_specs.append(out_spec)
        out_shape.append(sds)
    out = pl.pallas_call(
        functools.partial(_proj_kernel, n_a=len(a_parts), normed=ss is not None, residual=residual is not None,
                          scale=scale, next_norm=next_g is not None, n_cast=len(cast), d_in=kp * len(a_parts)),
        grid=(m // tm, n // tn),
        in_specs=in_specs,
        out_specs=out_specs,
        out_shape=out_shape,
        compiler_params=_params("parallel", "arbitrary"),
        name=name,
    )(*args)
    return out if next_g is not None or cast else out[0]


def _ffn_up_kernel(*refs, n_cast, d_in):
    h_ref, wg_ref, wu_ref, ss_ref = refs[:4]
    cast_in = refs[4:4 + n_cast]
    o_ref = refs[4 + n_cast]
    cast_out = refs[5 + n_cast:]
    h = h_ref[...]
    g = _scale_rows(_dot(h, wg_ref[...]), ss_ref, d_in)
    u = _scale_rows(_dot(h, wu_ref[...]), ss_ref, d_in)
    o_ref[...] = (_silu(g) * u).astype(o_ref.dtype)
    for src_ref, dst_ref in zip(cast_in, cast_out):
        dst_ref[...] = src_ref[...].astype(BF16)


def ffn_up(xg, ss, wg, wu, *, cast=(), tm=1024, tn=512):
    m, k = xg.shape
    n = wg.shape[1]
    tm = _tile(m, tm)
    tn = _tile(n, tn)
    wspec = pl.BlockSpec((k, tn), lambda i, j: (0, j))
    in_specs = [pl.BlockSpec((tm, k), lambda i, j: (i, 0)), wspec, wspec, pl.BlockSpec((tm, LANES), lambda i, j: (i, 0))]
    out_specs = [pl.BlockSpec((tm, tn), lambda i, j: (i, j))]
    out_shape = [jax.ShapeDtypeStruct((m, n), BF16)]
    args = [xg, wg, wu, ss]
    for w32, cast_layer in cast:
        in_spec, out_spec, sds = _cast_specs(w32, cast_layer, (m // tm) * (n // tn), n // tn)
        in_specs.append(in_spec)
        args.append(w32)
        out_specs.append(out_spec)
        out_shape.append(sds)
    return pl.pallas_call(
        functools.partial(_ffn_up_kernel, n_cast=len(cast), d_in=k),
        grid=(m // tm, n // tn),
        in_specs=in_specs,
        out_specs=out_specs,
        out_shape=out_shape,
        compiler_params=_params("parallel", "arbitrary"),
        name="ffn_up",
    )(*args)


def swiglu_half_step(x, xg, ss, weights, next_g, next_weights32):
    w_gate, w_up, w_down = weights
    cast_up, cast_down = (), ()
    if next_weights32 is not None:
        (g32, u32, d32), nl = next_weights32
        cast_up, cast_down = ((g32, nl), (u32, nl)), ((d32, nl),)
    act, *next_gu = ffn_up(xg, ss, w_gate, w_up, cast=cast_up)
    out = proj(act, w_down[None], 0, residual=x, scale=0.5, next_g=next_g, cast=cast_down, tn=512, name="ffn_down")
    out = list(out) if isinstance(out, (list, tuple)) else [out]
    x_new = out.pop(0)
    xg_new, ss_new = (out.pop(0), out.pop(0)) if next_g is not None else (None, None)
    return x_new, xg_new, ss_new, tuple(next_gu) + tuple(out)


def _gla_kernel(q_ref, k_ref, v_ref, r_ref, a_ref, wgu_ref, bg_ref, ng_ref, o_ref, st_ref, *, rt):
    C, SB = GLA_CHUNK, GLA_SUB

    @pl.when(pl.program_id(2) == 0)
    def _():
        st_ref[...] = jnp.zeros_like(st_ref)

    zg = _dot(a_ref[...].astype(BF16), wgu_ref[...]) + bg_ref[...]
    lf = (jnp.minimum(zg, 0.0) - jnp.log(1.0 + jnp.exp(-jnp.abs(zg)))) * (GLA_INV_TAU * LOG2_E)

    row = lax.broadcasted_iota(jnp.int32, (rt, GLA_DK), 0)
    rc = row & (C - 1)
    b = lf
    for s in (1, 2, 4, 8, 16, 32):
        b = b + jnp.where(rc >= s, pltpu.roll(b, s, 0), 0.0)

    q = q_ref[...] * (GLA_DK ** -0.5)
    k = k_ref[...]
    v = v_ref[...].astype(BF16)

    nb = rt // SB
    b3 = b.reshape(nb, SB, GLA_DK)
    q3 = q.reshape(nb, SB, GLA_DK)
    k3 = k.reshape(nb, SB, GLA_DK)
    rowc = lax.broadcasted_iota(jnp.int32, (rt, C), 0)
    lane = lax.broadcasted_iota(jnp.int32, (rt, C), 1)
    sub_row = rowc & (SB - 1)
    col_in_sub = lane - ((rowc & (C - 1)) - sub_row)
    causal_col = jnp.where(sub_row >= col_in_sub, col_in_sub, -1)
    a_diag = jnp.zeros((rt, C), F32)
    for j in range(SB):
        bj = jnp.broadcast_to(b3[:, j:j + 1, :], (nb, SB, GLA_DK))
        kj = jnp.broadcast_to(k3[:, j:j + 1, :], (nb, SB, GLA_DK))
        pj = (q3 * kj * jnp.exp2(b3 - bj)).reshape(rt, GLA_DK)
        rs = jnp.sum(pj, axis=-1, keepdims=True)
        a_diag = jnp.where(causal_col == j, rs, a_diag)

    lane_c = lax.broadcasted_iota(jnp.int32, (SB, C), 1)
    outs = []
    for c in range(rt // C):
        sl = slice(c * C, (c + 1) * C)
        bc, qc, kc, vc = b[sl], q[sl], k[sl], v[sl]
        rows = [jnp.zeros((SB, C), F32)]
        for i in range(1, C // SB):
            b0 = bc[i * SB:i * SB + 1]
            qi = qc[i * SB:(i + 1) * SB] * jnp.exp2(bc[i * SB:(i + 1) * SB] - b0)
            ki = kc * jnp.exp2(jnp.minimum(b0 - bc, 0.0))
            ai = _dot_nt(qi.astype(BF16), ki.astype(BF16))
            rows.append(jnp.where(lane_c < i * SB, ai, 0.0))
        a_c = a_diag[sl] + jnp.concatenate(rows, axis=0)
        o_intra = _dot(a_c.astype(BF16), vc)
        st = st_ref[...]
        o_inter = _dot_nt((qc * jnp.exp2(bc)).astype(BF16), st.astype(BF16))
        bl = bc[C - 1:C]
        kd = kc * jnp.exp2(bl - bc)
        st_ref[...] = st * jnp.exp2(bl) + _dot_tn(vc, kd.astype(BF16))
        outs.append(o_inter + o_intra)
    o = jnp.concatenate(outs, axis=0)

    ms = jnp.mean(o * o, axis=-1, keepdims=True)
    y = o * lax.rsqrt(ms + NORM_EPS) * ng_ref[...]
    o_ref[...] = (y * _silu(r_ref[...])).astype(o_ref.dtype)


def gla_mixer(z, za, w_gate_up, b_gate, norm_g, batch, seq, *, rt=512):
    m = z.shape[0]
    rt = _tile(seq, rt)
    nt = seq // rt
    kb = GLA_K_WIDTH // GLA_DK
    vb = 2 * GLA_K_WIDTH // GLA_DV
    rb = vb + GLA_V_WIDTH // GLA_DV
    wgu = jnp.zeros((LANES, GLA_K_WIDTH), BF16).at[:GLA_GATE_RANK].set(w_gate_up.astype(BF16))
    row = lambda b, h, t: b * nt + t
    return pl.pallas_call(
        functools.partial(_gla_kernel, rt=rt),
        grid=(batch, GLA_HEADS, nt),
        in_specs=[
            pl.BlockSpec((rt, GLA_DK), lambda b, h, t: (row(b, h, t), h)),
            pl.BlockSpec((rt, GLA_DK), lambda b, h, t: (row(b, h, t), kb + h)),
            pl.BlockSpec((rt, GLA_DV), lambda b, h, t: (row(b, h, t), vb + h)),
            pl.BlockSpec((rt, GLA_DV), lambda b, h, t: (row(b, h, t), rb + h)),
            pl.BlockSpec((rt, LANES), lambda b, h, t: (row(b, h, t), 0)),
            pl.BlockSpec((LANES, GLA_DK), lambda b, h, t: (0, h)),
            pl.BlockSpec((1, GLA_DK), lambda b, h, t: (0, h)),
            pl.BlockSpec((1, GLA_DV), lambda b, h, t: (0, 0)),
        ],
        out_specs=pl.BlockSpec((rt, GLA_DV), lambda b, h, t: (row(b, h, t), h)),
        out_shape=jax.ShapeDtypeStruct((m, GLA_V_WIDTH), BF16),
        scratch_shapes=[pltpu.VMEM((GLA_DV, GLA_DK), F32)],
        compiler_params=_params("parallel", "parallel", "arbitrary"),
        name="gla",
    )(z, z, z, z, za, wgu, b_gate.reshape(1, GLA_K_WIDTH), norm_g.reshape(1, GLA_DV))


def _sg_kernel(u_ref, s_ref, lng_ref, lnb_ref, ws_ref, bs_ref, o_ref, *, rt):
    s = _gelu_tanh(s_ref[...])
    mu = jnp.mean(s, axis=-1, keepdims=True)
    sc = s - mu
    var = jnp.mean(sc * sc, axis=-1, keepdims=True)
    sn = (sc * lax.rsqrt(var + LN_EPS) * lng_ref[...] + lnb_ref[...]).astype(BF16)
    u = _gelu_tanh(u_ref[...])
    ri = lax.broadcasted_iota(jnp.int32, (SG_CHUNK, SG_CHUNK), 0)
    ci = lax.broadcasted_iota(jnp.int32, (SG_CHUNK, SG_CHUNK), 1)
    for g in range(SG_GROUPS):
        w = jnp.where(ci <= ri, ws_ref[g], 0.0).astype(BF16)
        cols = slice(g * SG_GROUP_WIDTH, (g + 1) * SG_GROUP_WIDTH)
        for c in range(rt // SG_CHUNK):
            rows = slice(c * SG_CHUNK, (c + 1) * SG_CHUNK)
            mixed = _dot(w, sn[rows, cols]) + bs_ref[g]
            o_ref[rows, cols] = (u[rows, cols] * mixed).astype(o_ref.dtype)


def sg_mixer(z, ln_g, ln_b, w_s, b_s, *, rt=512):
    m = z.shape[0]
    rt = _tile(m, rt)
    return pl.pallas_call(
        functools.partial(_sg_kernel, rt=rt),
        grid=(m // rt,),
        in_specs=[
            pl.BlockSpec((rt, SG_WIDTH), lambda i: (i, 0)),
            pl.BlockSpec((rt, SG_WIDTH), lambda i: (i, 1)),
            pl.BlockSpec((1, SG_WIDTH), lambda i: (0, 0)),
            pl.BlockSpec((1, SG_WIDTH), lambda i: (0, 0)),
            pl.BlockSpec((SG_GROUPS, SG_CHUNK, SG_CHUNK), lambda i: (0, 0, 0)),
            pl.BlockSpec((SG_GROUPS, SG_CHUNK, 1), lambda i: (0, 0, 0)),
        ],
        out_specs=pl.BlockSpec((rt, SG_WIDTH), lambda i: (i, 0)),
        out_shape=jax.ShapeDtypeStruct((m, SG_WIDTH), BF16),
        compiler_params=_params("parallel"),
        name="spatial_gate",
    )(z, z, ln_g.reshape(1, SG_WIDTH), ln_b.reshape(1, SG_WIDTH), w_s, b_s.reshape(SG_GROUPS, SG_CHUNK, 1))


def split_ab_w_in(w_in):
    layers, d, _ = w_in.shape
    a0 = 2 * GLA_K_WIDTH + 2 * GLA_V_WIDTH
    w_a = jnp.concatenate(
        [w_in[:, :, a0:a0 + GLA_GATE_RANK], jnp.zeros((layers, d, LANES - GLA_GATE_RANK), BF16)], axis=2)
    return w_in, w_a, w_in[:, :, a0 + GLA_GATE_RANK:]


def mixer_gla_sg(x, xg, ss, layer, batch, seq, w_in_parts, w_gate_up, b_gate, gla_norm_g,
                 sg_ln_g, sg_ln_b, sg_w_s, sg_b_s, w_out, next_g):
    w, w_a, w_us = w_in_parts
    z = proj(xg, w, layer, ss=ss, n_cols=2 * GLA_K_WIDTH + 2 * GLA_V_WIDTH, name="ab_in_proj")
    za = proj(xg, w_a, layer, ss=ss, name="ab_gate_proj")
    zus = proj(xg, w_us, layer, ss=ss, name="ab_sg_proj")
    y_gla = gla_mixer(z, za, w_gate_up, b_gate, gla_norm_g, batch, seq)
    y_sg = sg_mixer(zus, sg_ln_g, sg_ln_b, sg_w_s, sg_b_s)
    return proj((y_gla, y_sg), w_out, layer, residual=x, next_g=next_g, tn=512, name="ab_out_proj")


def _cross_kernel(xg_ref, ss_ref, x_ref, wq_ref, wo_ref, k_ref, v_ref, g_ref, o_ref, xg_out_ref, ss_out_ref, *, d_in):
    q = _scale_rows(_dot(xg_ref[...], wq_ref[...]), ss_ref, d_in).astype(BF16)
    heads = []
    for hd in range(X_HEADS):
        cols = slice(hd * X_HEAD_DIM, (hd + 1) * X_HEAD_DIM)
        s = _dot_nt(q[:, cols], k_ref[:, cols]) * X_SCALE
        p = jnp.exp(s - jnp.max(s, axis=-1, keepdims=True))
        p = p / jnp.sum(p, axis=-1, keepdims=True)
        heads.append(_dot(p.astype(BF16), v_ref[:, cols]).astype(BF16))
    x_new = x_ref[...] + _dot(jnp.concatenate(heads, axis=1), wo_ref[...])
    o_ref[...] = x_new
    xg_out_ref[...] = (x_new * g_ref[...]).astype(BF16)
    ss_out_ref[...] = _row_sumsq(x_new)


def cross_attention(x, xg, ss, kv, batch, seq, w_q, w_o, layer, next_g, *, tm=256):
    m, d = xg.shape
    mem = kv.shape[0] // batch
    tm = _tile(seq, tm)
    nt = seq // tm
    rows = lambda b, t: (b * nt + t, 0)
    return pl.pallas_call(
        functools.partial(_cross_kernel, d_in=d),
        grid=(batch, nt),
        in_specs=[
            pl.BlockSpec((tm, d), rows),
            pl.BlockSpec((tm, LANES), rows),
            pl.BlockSpec((tm, d), rows),
            pl.BlockSpec((None, d, X_WIDTH), lambda b, t: (layer, 0, 0)),
            pl.BlockSpec((None, X_WIDTH, d), lambda b, t: (layer, 0, 0)),
            pl.BlockSpec((mem, X_WIDTH), lambda b, t: (b, 0)),
            pl.BlockSpec((mem, X_WIDTH), lambda b, t: (b, 1)),
            pl.BlockSpec((1, d), lambda b, t: (0, 0)),
        ],
        out_specs=[pl.BlockSpec((tm, d), rows), pl.BlockSpec((tm, d), rows), pl.BlockSpec((tm, LANES), rows)],
        out_shape=[jax.ShapeDtypeStruct((m, d), F32), jax.ShapeDtypeStruct((m, d), BF16),
                   jax.ShapeDtypeStruct((m, LANES), F32)],
        compiler_params=_params("parallel", "parallel"),
        name="cross_attn",
    )(xg, ss, x, w_q, w_o, kv, kv, next_g.reshape(1, d))


def _rope_tables(seq):
    inv_freq = jnp.power(ROPE_THETA, -jnp.arange(ROPE_HALF, dtype=F32) / ROPE_HALF)
    ang = jnp.arange(seq).astype(F32)[:, None] * inv_freq[None, :]
    cos, sin = jnp.cos(ang), jnp.sin(ang)
    rest = NSA_HEAD_DIM - ROPE_DIM
    c = jnp.concatenate([cos, cos, jnp.ones((seq, rest), F32)], axis=1)
    s_up = jnp.concatenate([jnp.zeros_like(sin), sin, jnp.zeros((seq, rest), F32)], axis=1)
    s_dn = jnp.concatenate([-sin, jnp.zeros_like(sin), jnp.zeros((seq, rest), F32)], axis=1)
    return c, s_up, s_dn


def _rope(x, c, s_up, s_dn, heads):
    width = heads * NSA_HEAD_DIM
    rep = lambda t: jnp.concatenate([t] * heads, axis=1)
    return (x * rep(c) + pltpu.roll(x, ROPE_HALF, 1) * rep(s_up)
            + pltpu.roll(x, width - ROPE_HALF, 1) * rep(s_dn))


def _nsa_kv_kernel(ks_ref, vs_ref, kw_ref, vw_ref, c_ref, su_ref, sd_ref, oks_ref, ovs_ref, okw_ref, ovw_ref):
    c, su, sd = c_ref[...], su_ref[...], sd_ref[...]
    oks_ref[...] = _rope(ks_ref[...], c, su, sd, NSA_KV_GROUPS).astype(BF16)
    okw_ref[...] = _rope(kw_ref[...], c, su, sd, NSA_KV_GROUPS).astype(BF16)
    ones = jnp.ones((ks_ref.shape[0], NSA_HEAD_DIM), BF16)
    for v_ref, o_ref in ((vs_ref, ovs_ref), (vw_ref, ovw_ref)):
        v = v_ref[...].astype(BF16)
        o_ref[...] = jnp.concatenate(
            [piece for g in range(NSA_KV_GROUPS) for piece in (v[:, g * NSA_HEAD_DIM:(g + 1) * NSA_HEAD_DIM], ones)],
            axis=1)


def nsa_kv_prep(z, tables, batch, seq, *, rt=1024):
    m = z.shape[0]
    rt = _tile(seq, rt)
    nt = seq // rt
    cb = NSA_Q_WIDTH // NSA_KV_WIDTH + 2
    zspec = lambda j: pl.BlockSpec((rt, NSA_KV_WIDTH), lambda b, t: (b * nt + t, cb + j))
    tspec = pl.BlockSpec((rt, NSA_HEAD_DIM), lambda b, t: (t, 0))
    kspec = pl.BlockSpec((rt, NSA_KV_WIDTH), lambda b, t: (b * nt + t, 0))
    vspec = pl.BlockSpec((rt, 2 * NSA_KV_WIDTH), lambda b, t: (b * nt + t, 0))
    ksds = jax.ShapeDtypeStruct((m, NSA_KV_WIDTH), BF16)
    vsds = jax.ShapeDtypeStruct((m, 2 * NSA_KV_WIDTH), BF16)
    return pl.pallas_call(
        _nsa_kv_kernel,
        grid=(batch, nt),
        in_specs=[zspec(0), zspec(1), zspec(2), zspec(3), tspec, tspec, tspec],
        out_specs=[kspec, vspec, kspec, vspec],
        out_shape=[ksds, vsds, ksds, vsds],
        compiler_params=_params("parallel", "parallel"),
        name="nsa_kv_prep",
    )(z, z, z, z, *tables)


def _nsa_cmp_kernel(x_ref, pos_ref, w1_ref, b1_ref, w2_ref, b2_ref, o_ref, *, nblk):
    half = NSA_CMP_STRIDE * NSA_HEAD_DIM
    pos_bias = _dot(pos_ref[...], w1_ref[...])[0:1] + b1_ref[...]
    row = lax.broadcasted_iota(jnp.int32, (nblk, NSA_HEAD_DIM), 0)
    xg = jnp.concatenate(
        [x_ref[pl.ds(l, nblk, stride=NSA_CMP_STRIDE), :] for l in range(NSA_CMP_STRIDE)], axis=1).astype(BF16)
    first = _dot(xg, w1_ref[:half])
    second = _dot(xg, w1_ref[half:])
    pre = first + pltpu.roll(second, nblk - 1, 0) + pos_bias
    out = _dot(_silu(pre).astype(BF16), w2_ref[...]) + b2_ref[...]
    o_ref[...] = jnp.where(row < nblk - 1, out, 0.0).astype(o_ref.dtype)


def nsa_compress(z, batch, seq, cmp_pos, cmp_w1, cmp_b1, cmp_w2, cmp_b2):
    nblk = seq // NSA_CMP_STRIDE
    cb = NSA_Q_WIDTH // NSA_HEAD_DIM
    flat = NSA_CMP_LEN * NSA_HEAD_DIM
    pos = jnp.zeros((2, 8, flat), BF16).at[:, 0].set(cmp_pos.reshape(2, flat).astype(BF16))
    return pl.pallas_call(
        functools.partial(_nsa_cmp_kernel, nblk=nblk),
        grid=(2, batch, NSA_KV_GROUPS),
        in_specs=[
            pl.BlockSpec((seq, NSA_HEAD_DIM), lambda s, b, g: (b, cb + s * NSA_KV_GROUPS + g)),
            pl.BlockSpec((None, 8, flat), lambda s, b, g: (s, 0, 0)),
            pl.BlockSpec((None, flat, NSA_HEAD_DIM), lambda s, b, g: (s, 0, 0)),
            pl.BlockSpec((None, 1, NSA_HEAD_DIM), lambda s, b, g: (s, 0, 0)),
            pl.BlockSpec((None, NSA_HEAD_DIM, NSA_HEAD_DIM), lambda s, b, g: (s, 0, 0)),
            pl.BlockSpec((None, 1, NSA_HEAD_DIM), lambda s, b, g: (s, 0, 0)),
        ],
        out_specs=pl.BlockSpec((None, None, None, nblk, NSA_HEAD_DIM), lambda s, b, g: (s, b, g, 0, 0)),
        out_shape=jax.ShapeDtypeStruct((2, batch, NSA_KV_GROUPS, nblk, NSA_HEAD_DIM), BF16),
        compiler_params=_params("parallel", "parallel", "parallel"),
        name="nsa_compress",
    )(z, pos, cmp_w1.astype(BF16), cmp_b1.reshape(2, 1, NSA_HEAD_DIM), cmp_w2.astype(BF16),
      cmp_b2.reshape(2, 1, NSA_HEAD_DIM))


def _row_reduce(s, op, lane_op):
    acc = s[:, :LANES]
    for c in range(1, s.shape[1] // LANES):
        acc = op(acc, s[:, c * LANES:(c + 1) * LANES])
    return lane_op(acc, axis=-1, keepdims=True)


def _nsa_attn_kernel(q_ref, gate_ref, c_ref, su_ref, sd_ref, kc_ref, vc_ref, ks_ref, vs_ref, kw_ref, vw_ref,
                     ovl_ref, blk_ref, spread_ref, o_ref,
                     qp_ref, qr_ref, s_ref, p_ref, bias_ref, m_ref, acc_ref, psum_ref, sel_ref, mix_ref, gx_ref,
                     *, tq, tk, seq, top_n):
    P, HD, SB, W, SL = NSA_HPG, NSA_HEAD_DIM, NSA_SEL_BLOCK, NSA_WINDOW, NSA_SLAB
    R = P * tq
    nsel = seq // SB
    ncmp = seq // NSA_CMP_STRIDE
    t0 = pl.program_id(2) * tq

    qf = q_ref[...]
    q_scale = NSA_SCALE * LOG2_E
    qr = _rope(qf, c_ref[...], su_ref[...], sd_ref[...], P)
    for p in range(P):
        qp_ref[p * tq:(p + 1) * tq] = (qf[:, p * HD:(p + 1) * HD] * q_scale).astype(BF16)
        qr_ref[p * tq:(p + 1) * tq] = (qr[:, p * HD:(p + 1) * HD] * q_scale).astype(BF16)

    def reset_state():
        m_ref[...] = jnp.full_like(m_ref, MASK_NEG)
        acc_ref[...] = jnp.zeros_like(acc_ref)

    def softmax_tile(k_t, v_t, n):
        for g0 in range(0, R, NSA_GROUP_ROWS):
            grp = slice(g0, g0 + NSA_GROUP_ROWS)
            s_grp = _dot_nt(qr_ref[grp], k_t)
            alphas, probs = [], []
            for r0 in range(g0, g0 + NSA_GROUP_ROWS, SL):
                rows = slice(r0, r0 + SL)
                b0 = r0 % tq
                s = s_grp[r0 - g0:r0 - g0 + SL] + bias_ref[b0:b0 + SL, :n]
                m_old = m_ref[rows]
                m_new = jnp.maximum(m_old, _row_reduce(s, jnp.maximum, jnp.max))
                alphas.append(jnp.exp2(m_old - m_new))
                m_ref[rows] = m_new
                probs.append(jnp.exp2(s - m_new).astype(BF16))
            alpha = jnp.concatenate(alphas, axis=0)
            acc_ref[grp] = alpha * acc_ref[grp] + _dot(jnp.concatenate(probs, axis=0), v_t)

    t_c = t0 + lax.broadcasted_iota(jnp.int32, (tq, ncmp), 0)
    end_c = lax.broadcasted_iota(jnp.int32, (tq, ncmp), 1) * NSA_CMP_STRIDE + (NSA_CMP_LEN - 1)
    bias_ref[:, :ncmp] = jnp.where(end_c <= t_c, 0.0, MASK_NEG)
    s_ref[:, :ncmp] = _dot_nt(qp_ref[...], kc_ref[...])
    psum_ref[...] = jnp.zeros_like(psum_ref)
    for j in range(R // SL):
        rows = slice(j * SL, (j + 1) * SL)
        b0 = (j * SL) % tq
        bias = bias_ref[b0:b0 + SL, :ncmp]
        s = s_ref[rows, :ncmp] + bias
        e = jnp.where(bias == 0.0, jnp.exp2(s - _row_reduce(s, jnp.maximum, jnp.max)), 0.0)
        pn = e / jnp.maximum(_row_reduce(e, jnp.add, jnp.sum), 1e-30)
        p_ref[rows, :ncmp] = pn.astype(BF16)
        psum_ref[b0:b0 + SL] += pn
    o_c = _dot(p_ref[:, :ncmp], vc_ref[...])
    gx_ref[...] = _dot(_sigmoid(gate_ref[...]).astype(BF16), spread_ref[...])
    gate = lambda p, j: gx_ref[:, (3 * p + j) * LANES:(3 * p + j + 1) * LANES]
    for p in range(P):
        rows = slice(p * tq, (p + 1) * tq)
        mix_ref[rows] = gate(p, 0) * o_c[rows]

    p_sum = psum_ref[...]
    p_hi = p_sum.astype(BF16)
    p_lo = (p_sum - p_hi.astype(F32)).astype(BF16)
    imp_t = _dot_nt(ovl_ref[...], p_hi) + _dot_nt(ovl_ref[...], p_lo)

    blk = lax.broadcasted_iota(jnp.int32, (nsel, tq), 0)
    cur = (t0 + lax.broadcasted_iota(jnp.int32, (nsel, tq), 1)) // SB
    forced = (blk == 0) | (blk == cur) | (blk == cur - 1)
    score = jnp.where(blk <= cur, imp_t + jnp.where(forced, NSA_FORCE_BONUS, 0.0), -jnp.inf)
    rank = jnp.zeros((nsel, tq), jnp.int32)
    for sp in range(nsel):
        other = jnp.broadcast_to(score[sp:sp + 1, :], (nsel, tq))
        rank = rank + jnp.where(blk > sp, jnp.where(other >= score, 1, 0), jnp.where(other > score, 1, 0))
    sel_ref[...] = jnp.where((rank < top_n) & (blk <= cur), 1.0, 0.0).T.astype(BF16)

    wlen = W + tq
    base = pl.multiple_of(jnp.clip(t0 - W, 0, seq - wlen), tq)
    delta = (t0 + lax.broadcasted_iota(jnp.int32, (tq, wlen), 0)) - (base + lax.broadcasted_iota(jnp.int32, (tq, wlen), 1))
    bias_ref[:, :wlen] = jnp.where((delta >= 0) & (delta < W), 0.0, MASK_NEG)
    reset_state()
    softmax_tile(kw_ref[pl.ds(base, wlen), :], vw_ref[pl.ds(base, wlen), :], wlen)
    for p in range(P):
        rows = slice(p * tq, (p + 1) * tq)
        o_w = acc_ref[rows, :HD] / acc_ref[rows, HD:]
        mix_ref[rows] += gate(p, 2) * o_w

    t_k = t0 + lax.broadcasted_iota(jnp.int32, (tq, tk), 0)
    lane_k = lax.broadcasted_iota(jnp.int32, (tq, tk), 1)
    reset_state()

    def sel_step(kt, carry):
        k0 = pl.multiple_of(kt * tk, tk)
        picked = _dot(sel_ref[...], blk_ref[kt])
        bias_ref[:, :tk] = jnp.where((picked > 0.5) & (lane_k + k0 <= t_k), 0.0, MASK_NEG)
        softmax_tile(ks_ref[pl.ds(k0, tk), :], vs_ref[pl.ds(k0, tk), :], tk)
        return carry

    lax.fori_loop(0, (t0 + tq + tk - 1) // tk, sel_step, 0)
    for p in range(P):
        rows = slice(p * tq, (p + 1) * tq)
        o_s = acc_ref[rows, :HD] / acc_ref[rows, HD:]
        o_ref[:, p * HD:(p + 1) * HD] = (mix_ref[rows] + gate(p, 1) * o_s).astype(o_ref.dtype)


def nsa_attention(z, zg, kv, cmp, tables, batch, seq, *, tq=256, tk=1024):
    m = z.shape[0]
    ks, vs, kw, vw = kv
    tq = _tile(seq, tq)
    tk = _tile(seq, tk)
    nt = seq // tq
    nsel = seq // NSA_SEL_BLOCK
    ncmp = seq // NSA_CMP_STRIDE
    gw = NSA_HPG * NSA_HEAD_DIM
    c_start = jnp.arange(ncmp) * NSA_CMP_STRIDE
    s_start = jnp.arange(nsel) * NSA_SEL_BLOCK
    ovl = ((c_start[None, :] <= s_start[:, None] + NSA_SEL_BLOCK - 1)
           & (c_start[None, :] + NSA_CMP_LEN - 1 >= s_start[:, None])
           & (jnp.arange(ncmp)[None, :] < ncmp - 1)).astype(BF16)
    blk_of_key = (jnp.arange(seq)[None, :] // NSA_SEL_BLOCK == jnp.arange(nsel)[:, None]).astype(BF16)
    blk_of_key = blk_of_key.reshape(nsel, seq // tk, tk).transpose(1, 0, 2)
    n_gate = 3 * NSA_HPG
    spread = (jnp.arange(LANES)[:, None] == jnp.arange(n_gate * LANES)[None, :] // LANES).astype(BF16)
    kspec =pl.BlockSpec((seq, NSA_HEAD_DIM), lambda b, g, i: (b, g))
    vspec = pl.BlockSpec((seq, 2 * NSA_HEAD_DIM), lambda b, g, i: (b, g))
    cmpspec = lambda s: pl.BlockSpec((None, None, None, ncmp, NSA_HEAD_DIM), lambda b, g, i: (s, b, g, 0, 0))
    tspec = pl.BlockSpec((tq, NSA_HEAD_DIM), lambda b, g, i: (i, 0))
    rows = NSA_HPG * tq
    width = max(tk, NSA_WINDOW + tq, ncmp)
    return pl.pallas_call(
        functools.partial(_nsa_attn_kernel, tq=tq, tk=tk, seq=seq, top_n=min(NSA_TOPK, nsel)),
        grid=(batch, NSA_KV_GROUPS, nt),
        in_specs=[
            pl.BlockSpec((tq, gw), lambda b, g, i: (b * nt + i, g)),
            pl.BlockSpec((tq, LANES), lambda b, g, i: (b * nt + i, g)),
            tspec, tspec, tspec,
            cmpspec(0), cmpspec(1),
            kspec, vspec, kspec, vspec,
            pl.BlockSpec((nsel, ncmp), lambda b, g, i: (0, 0)),
            pl.BlockSpec((seq // tk, nsel, tk), lambda b, g, i: (0, 0, 0)),
            pl.BlockSpec((LANES, n_gate * LANES), lambda b, g, i: (0, 0)),
        ],
        out_specs=pl.BlockSpec((tq, gw), lambda b, g, i: (b * nt + i, g)),
        out_shape=jax.ShapeDtypeStruct((m, NSA_Q_WIDTH), BF16),
        scratch_shapes=[
            pltpu.VMEM((rows, NSA_HEAD_DIM), BF16),
            pltpu.VMEM((rows, NSA_HEAD_DIM), BF16),
            pltpu.VMEM((rows, width), F32),
            pltpu.VMEM((rows, width), BF16),
            pltpu.VMEM((tq, width), F32),
            pltpu.VMEM((rows, 1), F32),
            pltpu.VMEM((rows, 2 * NSA_HEAD_DIM), F32),
            pltpu.VMEM((tq, ncmp), F32),
            pltpu.VMEM((tq, nsel), BF16),
            pltpu.VMEM((rows, NSA_HEAD_DIM), F32),
            pltpu.VMEM((tq, n_gate * LANES), F32),
        ],
        compiler_params=_params("parallel", "parallel", "arbitrary"),
        name="nsa_attn",
    )(z, zg, *tables, cmp, cmp, ks, vs, kw, vw, ovl, blk_of_key, spread)


def split_nsa_w_in(w_in):
    layers, d, _ = w_in.shape
    c0 = NSA_Q_WIDTH + 6 * NSA_KV_WIDTH
    per_group = 3 * NSA_HPG
    cols = []
    for g in range(NSA_KV_GROUPS):
        cols += [w_in[:, :, c0 + g * per_group:c0 + (g + 1) * per_group],
                 jnp.zeros((layers, d, LANES - per_group), BF16)]
    return w_in, jnp.concatenate(cols, axis=2)


def mixer_nsa(x, xg, ss, layer, batch, seq, w_in_parts, cmp_pos, cmp_w1, cmp_b1, cmp_w2, cmp_b2, w_out, next_g):
    w, w_gates = w_in_parts
    z = proj(xg, w, layer, ss=ss, n_cols=NSA_Q_WIDTH + 6 * NSA_KV_WIDTH, name="nsa_in_proj")
    zg = proj(xg, w_gates, layer, ss=ss, name="nsa_gate_proj")
    tables = _rope_tables(seq)
    kv = nsa_kv_prep(z, tables, batch, seq)
    cmp = nsa_compress(z, batch, seq, cmp_pos, cmp_w1, cmp_b1, cmp_w2, cmp_b2)
    o = nsa_attention(z, zg, kv, cmp, tables, batch, seq)
    return proj(o, w_out, layer, residual=x, next_g=next_g, tn=512, name="nsa_out_proj")


def kernel(x, mem, mem_norm_g,
           ffn1_norm_g, ffn1_w_gate, ffn1_w_up, ffn1_w_down,
           mix_norm_g,
           ab_w_in, gla_w_gate_up, gla_b_gate, gla_norm_g, sg_ln_g, sg_ln_b, sg_w_s, sg_b_s, ab_w_out,
           nsa_w_in, nsa_cmp_pos, nsa_cmp_w1, nsa_cmp_b1, nsa_cmp_w2, nsa_cmp_b2, nsa_w_out,
           cross_norm_g, cross_w_q, cross_w_kv, cross_w_o,
           ffn2_norm_g, ffn2_w_gate, ffn2_w_up, ffn2_w_down,
           final_norm_g):
    batch, seq, d = x.shape
    depth = ffn1_norm_g.shape[0]
    bf = lambda w: w.astype(BF16)
    ffn1_w32 = (ffn1_w_gate, ffn1_w_up, ffn1_w_down)
    ffn2_w32 = (ffn2_w_gate, ffn2_w_up, ffn2_w_down)
    ffn_w = tuple(bf(w[0]) for w in ffn1_w32)
    ab_w_in_parts, ab_w_out = split_ab_w_in(bf(ab_w_in)), bf(ab_w_out)
    nsa_w_in_parts, nsa_w_out = split_nsa_w_in(bf(nsa_w_in)), bf(nsa_w_out)
    cross_w_q, cross_w_kv, cross_w_o = bf(cross_w_q), bf(cross_w_kv), bf(cross_w_o)

    mem_n = rmsnorm(mem.reshape(-1, d), mem_norm_g)
    x = x.reshape(batch * seq, d)
    xg, ss = norm_prep(x, ffn1_norm_g[0])
    for i in range(depth):
        x, xg, ss, ffn_w = swiglu_half_step(x, xg, ss, ffn_w, mix_norm_g[i], (ffn2_w32, i))
        j = i // 2
        if i % 2 == 0:
            x, xg, ss = mixer_gla_sg(x, xg, ss, j, batch, seq, ab_w_in_parts, gla_w_gate_up[j], gla_b_gate[j],
                                     gla_norm_g[j], sg_ln_g[j], sg_ln_b[j], sg_w_s[j], sg_b_s[j], ab_w_out,
                                     cross_norm_g[i])
        else:
            x, xg, ss = mixer_nsa(x, xg, ss, j, batch, seq, nsa_w_in_parts, nsa_cmp_pos[j], nsa_cmp_w1[j],
                                  nsa_cmp_b1[j], nsa_cmp_w2[j], nsa_cmp_b2[j], nsa_w_out, cross_norm_g[i])
        kv = proj(mem_n, cross_w_kv, i, out_dtype=BF16, name="cross_kv_proj")
        x, xg, ss = cross_attention(x, xg, ss, kv, batch, seq, cross_w_q, cross_w_o, i, ffn2_norm_g[i])
        last = i + 1 == depth
        x, xg, ss, ffn_w = swiglu_half_step(x, xg, ss, ffn_w, None if last else ffn1_norm_g[i + 1],
                                            None if last else (ffn1_w32, i + 1))
    return rmsnorm(x, final_norm_g, out_dtype=F32).reshape(batch, seq, d)
```

```python
import functools

import jax
import jax.numpy as jnp
from jax import lax
from jax.experimental import pallas as pl
from jax.experimental.pallas import tpu as pltpu

F32 = jnp.float32
BF16 = jnp.bfloat16

NORM_EPS = 1e-6
LN_EPS = 1e-5
ROPE_THETA = 500000.0

GLA_HEADS = 8
GLA_DK = 128
GLA_DV = 256
GLA_K_WIDTH = GLA_HEADS * GLA_DK
GLA_V_WIDTH = GLA_HEADS * GLA_DV
GLA_GATE_RANK = 16
GLA_INV_TAU = 1.0 / 16.0
GLA_CHUNK = 64
GLA_SUB = 16
SG_WIDTH = 2048
SG_GROUPS = 4
SG_GROUP_WIDTH = SG_WIDTH // SG_GROUPS
SG_CHUNK = 128

NSA_HEAD_DIM = 128
NSA_KV_GROUPS = 4
NSA_HPG = 8
NSA_Q_WIDTH = NSA_KV_GROUPS * NSA_HPG * NSA_HEAD_DIM
NSA_KV_WIDTH = NSA_KV_GROUPS * NSA_HEAD_DIM
NSA_CMP_STRIDE = 16
NSA_CMP_LEN = 32
NSA_SEL_BLOCK = 64
NSA_TOPK = 16
NSA_WINDOW = 512
NSA_FORCE_BONUS = 1e4
NSA_SCALE = NSA_HEAD_DIM ** -0.5
ROPE_DIM = NSA_HEAD_DIM // 4
ROPE_HALF = ROPE_DIM // 2
NSA_SLAB = 64
NSA_GROUP_ROWS = 256
LOG2_E = 1.4426950408889634

X_HEADS = 4
X_HEAD_DIM = 128
X_WIDTH = X_HEADS * X_HEAD_DIM
X_SCALE = X_HEAD_DIM ** -0.5

LANES = 128
MASK_NEG = -1e30
VMEM_LIMIT_BYTES = 56 * 1024 * 1024


def _params(*sem):
    return pltpu.CompilerParams(dimension_semantics=sem, vmem_limit_bytes=VMEM_LIMIT_BYTES)


def _dot(a, b):
    return jnp.dot(a, b, preferred_element_type=F32)


def _dot_nt(a, b):
    return lax.dot_general(a, b, (((1,), (1,)), ((), ())), preferred_element_type=F32)


def _dot_tn(a, b):
    return lax.dot_general(a, b, (((0,), (0,)), ((), ())), preferred_element_type=F32)


def _sigmoid(x):
    return 1.0 / (1.0 + jnp.exp(-x))


def _silu(x):
    return x * _sigmoid(x)


def _gelu_tanh(x):
    c = 0.7978845608028654
    return x * (0.5 * (1.0 + jnp.tanh(c * (x + 0.044715 * (x * x * x)))))


def _tile(n, pref):
    if n <= pref:
        return n
    t = pref
    while n % t:
        t //= 2
    return t


def _rmsnorm_kernel(x_ref, g_ref, o_ref):
    x = x_ref[...]
    ms = jnp.mean(x * x, axis=-1, keepdims=True)
    o_ref[...] = (x * lax.rsqrt(ms + NORM_EPS) * g_ref[...]).astype(o_ref.dtype)


def rmsnorm(x, g, out_dtype=BF16):
    m, d = x.shape
    tm = _tile(m, 256)
    return pl.pallas_call(
        _rmsnorm_kernel,
        grid=(m // tm,),
        in_specs=[pl.BlockSpec((tm, d), lambda i: (i, 0)), pl.BlockSpec((1, d), lambda i: (0, 0))],
        out_specs=pl.BlockSpec((tm, d), lambda i: (i, 0)),
        out_shape=jax.ShapeDtypeStruct((m, d), out_dtype),
        compiler_params=_params("parallel"),
        name="rmsnorm",
    )(x, g.reshape(1, d))


def _row_sumsq(x):
    return jnp.broadcast_to(jnp.sum(x * x, axis=-1, keepdims=True), (x.shape[0], LANES))


def _norm_prep_kernel(x_ref, g_ref, xg_ref, ss_ref):
    x = x_ref[...]
    xg_ref[...] = (x * g_ref[...]).astype(BF16)
    ss_ref[...] = _row_sumsq(x)


def norm_prep(x, g):
    m, d = x.shape
    tm = _tile(m, 256)
    return pl.pallas_call(
        _norm_prep_kernel,
        grid=(m // tm,),
        in_specs=[pl.BlockSpec((tm, d), lambda i: (i, 0)), pl.BlockSpec((1, d), lambda i: (0, 0))],
        out_specs=[pl.BlockSpec((tm, d), lambda i: (i, 0)), pl.BlockSpec((tm, LANES), lambda i: (i, 0))],
        out_shape=[jax.ShapeDtypeStruct((m, d), BF16), jax.ShapeDtypeStruct((m, LANES), F32)],
        compiler_params=_params("parallel"),
        name="norm_prep",
    )(x, g.reshape(1, d))


def _scale_rows(acc, ss_ref, d):
    rstd = lax.rsqrt(ss_ref[...] / d + NORM_EPS)
    return acc * jnp.concatenate([rstd] * (acc.shape[1] // LANES), axis=1)


def _proj_kernel(*refs, n_a, normed, residual, scale, next_norm, n_cast, d_in):
    it = iter(refs)
    a_refs = [next(it) for _ in range(n_a)]
    w_refs = [next(it) for _ in range(n_a)]
    ss_ref = next(it) if normed else None
    r_ref = next(it) if residual else None
    g_ref = next(it) if next_norm else None
    cast_in = [next(it) for _ in range(n_cast)]
    o_ref = next(it)
    acc = _dot(a_refs[0][...], w_refs[0][...])
    for a_ref, w_ref in zip(a_refs[1:], w_refs[1:]):
        acc = acc + _dot(a_ref[...], w_ref[...])
    if normed:
        acc = _scale_rows(acc, ss_ref, d_in)
    if residual:
        acc = r_ref[...] + scale * acc
    o_ref[...] = acc.astype(o_ref.dtype)
    if next_norm:
        xg_ref, ss_out_ref = next(it), next(it)
        xg_ref[...] = (acc * g_ref[...]).astype(BF16)
        part = _row_sumsq(acc)

        @pl.when(pl.program_id(1) == 0)
        def _():
            ss_out_ref[...] = part

        @pl.when(pl.program_id(1) != 0)
        def _():
            ss_out_ref[...] += part
    for src_ref in cast_in:
        next(it)[...] = src_ref[...].astype(BF16)


def _as_list(out):
    return list(out) if isinstance(out, (list, tuple)) else [out]


def _cast_specs(w32, layer, n_steps, nj):
    _, k, n = w32.shape
    nblk = n_steps
    while k % nblk or (k // nblk) % 16:
        nblk -= 1
    rb = k // nblk
    blk = lambda i, j: jnp.minimum(i * nj + j, nblk - 1)
    return (pl.BlockSpec((None, rb, n), lambda i, j: (layer, blk(i, j), 0)),
            pl.BlockSpec((rb, n), lambda i, j: (blk(i, j), 0)),
            jax.ShapeDtypeStruct((k, n), BF16))


def proj(a, w, layer, *, ss=None, residual=None, scale=1.0, next_g=None, n_cols=None, out_dtype=F32,
         cast=(), tm=1024, tn=1024, name="proj"):
    a_parts = a if isinstance(a, tuple) else (a,)
    m, kp = a_parts[0].shape
    n = w.shape[2] if n_cols is None else n_cols
    tm = _tile(m, tm)
    tn = _tile(n, tn)
    in_specs = [pl.BlockSpec((tm, kp), lambda i, j: (i, 0)) for _ in a_parts]
    in_specs += [pl.BlockSpec((None, kp, tn), lambda i, j, r=r: (layer, r, j)) for r in range(len(a_parts))]
    args = list(a_parts) + [w] * len(a_parts)
    if ss is not None:
        in_specs.append(pl.BlockSpec((tm, LANES), lambda i, j: (i, 0)))
        args.append(ss)
    if residual is not None:
        in_specs.append(pl.BlockSpec((tm, tn), lambda i, j: (i, j)))
        args.append(residual)
        out_dtype = F32
    out_specs = [pl.BlockSpec((tm, tn), lambda i, j: (i, j))]
    out_shape = [jax.ShapeDtypeStruct((m, n), out_dtype)]
    if next_g is not None:
        in_specs.append(pl.BlockSpec((1, tn), lambda i, j: (0, j)))
        args.append(next_g.reshape(1, n))
        out_specs += [pl.BlockSpec((tm, tn), lambda i, j: (i, j)), pl.BlockSpec((tm, LANES), lambda i, j: (i, 0))]
        out_shape += [jax.ShapeDtypeStruct((m, n), BF16), jax.ShapeDtypeStruct((m, LANES), F32)]
    for w32, cast_layer in cast:
        in_spec, out_spec, sds = _cast_specs(w32, cast_layer, (m // tm) * (n // tn), n // tn)
        in_specs.append(in_spec)
        args.append(w32)
        out_specs.append(out_spec)
        out_shape.append(sds)
    out = pl.pallas_call(
        functools.partial(_proj_kernel, n_a=len(a_parts), normed=ss is not None, residual=residual is not None,
                          scale=scale, next_norm=next_g is not None, n_cast=len(cast), d_in=kp * len(a_parts)),
        grid=(m // tm, n // tn),
        in_specs=in_specs,
        out_specs=out_specs,
        out_shape=out_shape,
        compiler_params=_params("parallel", "arbitrary"),
        name=name,
    )(*args)
    return out if next_g is not None or cast else out[0]


def _ffn_up_kernel(*refs, n_cast, d_in):
    h_ref, wg_ref, wu_ref, ss_ref = refs[:4]
    cast_in = refs[4:4 + n_cast]
    o_ref = refs[4 + n_cast]
    cast_out = refs[5 + n_cast:]
    h = h_ref[...]
    g = _scale_rows(_dot(h, wg_ref[...]), ss_ref, d_in)
    u = _scale_rows(_dot(h, wu_ref[...]), ss_ref, d_in)
    o_ref[...] = (_silu(g) * u).astype(o_ref.dtype)
    for src_ref, dst_ref in zip(cast_in, cast_out):
        dst_ref[...] = src_ref[...].astype(BF16)


def ffn_up(xg, ss, wg, wu, *, cast=(), tm=1024, tn=512):
    m, k = xg.shape
    n = wg.shape[1]
    tm = _tile(m, tm)
    tn = _tile(n, tn)
    wspec = pl.BlockSpec((k, tn), lambda i, j: (0, j))
    in_specs = [pl.BlockSpec((tm, k), lambda i, j: (i, 0)), wspec, wspec, pl.BlockSpec((tm, LANES), lambda i, j: (i, 0))]
    out_specs = [pl.BlockSpec((tm, tn), lambda i, j: (i, j))]
    out_shape = [jax.ShapeDtypeStruct((m, n), BF16)]
    args = [xg, wg, wu, ss]
    for w32, cast_layer in cast:
        in_spec, out_spec, sds = _cast_specs(w32, cast_layer, (m // tm) * (n // tn), n // tn)
        in_specs.append(in_spec)
        args.append(w32)
        out_specs.append(out_spec)
        out_shape.append(sds)
    return pl.pallas_call(
        functools.partial(_ffn_up_kernel, n_cast=len(cast), d_in=k),
        grid=(m // tm, n // tn),
        in_specs=in_specs,
        out_specs=out_specs,
        out_shape=out_shape,
        compiler_params=_params("parallel", "arbitrary"),
        name="ffn_up",
    )(*args)


def swiglu_half_step(x, xg, ss, weights, next_g, next_weights32):
    w_gate, w_up, w_down = weights
    cast_up, cast_down = (), ()
    if next_weights32 is not None:
        (g32, u32, d32), nl = next_weights32
        cast_up, cast_down = ((g32, nl), (u32, nl)), ((d32, nl),)
    act, *next_gu = ffn_up(xg, ss, w_gate, w_up, cast=cast_up)
    out = proj(act, w_down[None], 0, residual=x, scale=0.5, next_g=next_g, cast=cast_down, tn=512, name="ffn_down")
    out = list(out) if isinstance(out, (list, tuple)) else [out]
    x_new = out.pop(0)
    xg_new, ss_new = (out.pop(0), out.pop(0)) if next_g is not None else (None, None)
    return x_new, xg_new, ss_new, tuple(next_gu) + tuple(out)


def _gla_kernel(q_ref, k_ref, v_ref, r_ref, a_ref, wgu_ref, bg_ref, ng_ref, o_ref, st_ref, *, rt):
    C, SB = GLA_CHUNK, GLA_SUB

    @pl.when(pl.program_id(2) == 0)
    def _():
        st_ref[...] = jnp.zeros_like(st_ref)

    zg = _dot(a_ref[...].astype(BF16), wgu_ref[...]) + bg_ref[...]
    lf = (jnp.minimum(zg, 0.0) - jnp.log(1.0 + jnp.exp(-jnp.abs(zg)))) * (GLA_INV_TAU * LOG2_E)

    row = lax.broadcasted_iota(jnp.int32, (rt, GLA_DK), 0)
    rc = row & (C - 1)
    b = lf
    for s in (1, 2, 4, 8, 16, 32):
        b = b + jnp.where(rc >= s, pltpu.roll(b, s, 0), 0.0)

    q = q_ref[...] * (GLA_DK ** -0.5)
    k = k_ref[...]
    v = v_ref[...].astype(BF16)

    nb = rt // SB
    b3 = b.reshape(nb, SB, GLA_DK)
    q3 = q.reshape(nb, SB, GLA_DK)
    k3 = k.reshape(nb, SB, GLA_DK)
    rowc = lax.broadcasted_iota(jnp.int32, (rt, C), 0)
    lane = lax.broadcasted_iota(jnp.int32, (rt, C), 1)
    sub_row = rowc & (SB - 1)
    col_in_sub = lane - ((rowc & (C - 1)) - sub_row)
    causal_col = jnp.where(sub_row >= col_in_sub, col_in_sub, -1)
    a_diag = jnp.zeros((rt, C), F32)
    for j in range(SB):
        bj = jnp.broadcast_to(b3[:, j:j + 1, :], (nb, SB, GLA_DK))
        kj = jnp.broadcast_to(k3[:, j:j + 1, :], (nb, SB, GLA_DK))
        pj = (q3 * kj * jnp.exp2(b3 - bj)).reshape(rt, GLA_DK)
        rs = jnp.sum(pj, axis=-1, keepdims=True)
        a_diag = jnp.where(causal_col == j, rs, a_diag)

    lane_c = lax.broadcasted_iota(jnp.int32, (SB, C), 1)
    outs = []
    for c in range(rt // C):
        sl = slice(c * C, (c + 1) * C)
        bc, qc, kc, vc = b[sl], q[sl], k[sl], v[sl]
        rows = [jnp.zeros((SB, C), F32)]
        for i in range(1, C // SB):
            b0 = bc[i * SB:i * SB + 1]
            qi = qc[i * SB:(i + 1) * SB] * jnp.exp2(bc[i * SB:(i + 1) * SB] - b0)
            ki = kc * jnp.exp2(jnp.minimum(b0 - bc, 0.0))
            ai = _dot_nt(qi.astype(BF16), ki.astype(BF16))
            rows.append(jnp.where(lane_c < i * SB, ai, 0.0))
        a_c = a_diag[sl] + jnp.concatenate(rows, axis=0)
        o_intra = _dot(a_c.astype(BF16), vc)
        st = st_ref[...]
        o_inter = _dot_nt((qc * jnp.exp2(bc)).astype(BF16), st.astype(BF16))
        bl = bc[C - 1:C]
        kd = kc * jnp.exp2(bl - bc)
        st_ref[...] = st * jnp.exp2(bl) + _dot_tn(vc, kd.astype(BF16))
        outs.append(o_inter + o_intra)
    o = jnp.concatenate(outs, axis=0)

    ms = jnp.mean(o * o, axis=-1, keepdims=True)
    y = o * lax.rsqrt(ms + NORM_EPS) * ng_ref[...]
    o_ref[...] = (y * _silu(r_ref[...])).astype(o_ref.dtype)


def gla_mixer(z, za, w_gate_up, b_gate, norm_g, batch, seq, *, rt=512):
    m = z.shape[0]
    rt = _tile(seq, rt)
    nt = seq // rt
    kb = GLA_K_WIDTH // GLA_DK
    vb = 2 * GLA_K_WIDTH // GLA_DV
    rb = vb + GLA_V_WIDTH // GLA_DV
    wgu = jnp.zeros((LANES, GLA_K_WIDTH), BF16).at[:GLA_GATE_RANK].set(w_gate_up.astype(BF16))
    row = lambda b, h, t: b * nt + t
    return pl.pallas_call(
        functools.partial(_gla_kernel, rt=rt),
        grid=(batch, GLA_HEADS, nt),
        in_specs=[
            pl.BlockSpec((rt, GLA_DK), lambda b, h, t: (row(b, h, t), h)),
            pl.BlockSpec((rt, GLA_DK), lambda b, h, t: (row(b, h, t), kb + h)),
            pl.BlockSpec((rt, GLA_DV), lambda b, h, t: (row(b, h, t), vb + h)),
            pl.BlockSpec((rt, GLA_DV), lambda b, h, t: (row(b, h, t), rb + h)),
            pl.BlockSpec((rt, LANES), lambda b, h, t: (row(b, h, t), 0)),
            pl.BlockSpec((LANES, GLA_DK), lambda b, h, t: (0, h)),
            pl.BlockSpec((1, GLA_DK), lambda b, h, t: (0, h)),
            pl.BlockSpec((1, GLA_DV), lambda b, h, t: (0, 0)),
        ],
        out_specs=pl.BlockSpec((rt, GLA_DV), lambda b, h, t: (row(b, h, t), h)),
        out_shape=jax.ShapeDtypeStruct((m, GLA_V_WIDTH), BF16),
        scratch_shapes=[pltpu.VMEM((GLA_DV, GLA_DK), F32)],
        compiler_params=_params("parallel", "parallel", "arbitrary"),
        name="gla",
    )(z, z, z, z, za, wgu, b_gate.reshape(1, GLA_K_WIDTH), norm_g.reshape(1, GLA_DV))


def _sg_kernel(u_ref, s_ref, lng_ref, lnb_ref, ws_ref, bs_ref, o_ref, *, rt):
    s = _gelu_tanh(s_ref[...])
    mu = jnp.mean(s, axis=-1, keepdims=True)
    sc = s - mu
    var = jnp.mean(sc * sc, axis=-1, keepdims=True)
    sn = (sc * lax.rsqrt(var + LN_EPS) * lng_ref[...] + lnb_ref[...]).astype(BF16)
    u = _gelu_tanh(u_ref[...])
    ri = lax.broadcasted_iota(jnp.int32, (SG_CHUNK, SG_CHUNK), 0)
    ci = lax.broadcasted_iota(jnp.int32, (SG_CHUNK, SG_CHUNK), 1)
    for g in range(SG_GROUPS):
        w = jnp.where(ci <= ri, ws_ref[g], 0.0).astype(BF16)
        cols = slice(g * SG_GROUP_WIDTH, (g + 1) * SG_GROUP_WIDTH)
        for c in range(rt // SG_CHUNK):
            rows = slice(c * SG_CHUNK, (c + 1) * SG_CHUNK)
            mixed = _dot(w, sn[rows, cols]) + bs_ref[g]
            o_ref[rows, cols] = (u[rows, cols] * mixed).astype(o_ref.dtype)


def sg_mixer(z, ln_g, ln_b, w_s, b_s, *, rt=512):
    m = z.shape[0]
    rt = _tile(m, rt)
    return pl.pallas_call(
        functools.partial(_sg_kernel, rt=rt),
        grid=(m // rt,),
        in_specs=[
            pl.BlockSpec((rt, SG_WIDTH), lambda i: (i, 0)),
            pl.BlockSpec((rt, SG_WIDTH), lambda i: (i, 1)),
            pl.BlockSpec((1, SG_WIDTH), lambda i: (0, 0)),
            pl.BlockSpec((1, SG_WIDTH), lambda i: (0, 0)),
            pl.BlockSpec((SG_GROUPS, SG_CHUNK, SG_CHUNK), lambda i: (0, 0, 0)),
            pl.BlockSpec((SG_GROUPS, SG_CHUNK, 1), lambda i: (0, 0, 0)),
        ],
        out_specs=pl.BlockSpec((rt, SG_WIDTH), lambda i: (i, 0)),
        out_shape=jax.ShapeDtypeStruct((m, SG_WIDTH), BF16),
        compiler_params=_params("parallel"),
        name="spatial_gate",
    )(z, z, ln_g.reshape(1, SG_WIDTH), ln_b.reshape(1, SG_WIDTH), w_s, b_s.reshape(SG_GROUPS, SG_CHUNK, 1))


def split_ab_w_in(w_in):
    layers, d, _ = w_in.shape
    a0 = 2 * GLA_K_WIDTH + 2 * GLA_V_WIDTH
    w_a = jnp.concatenate(
        [w_in[:, :, a0:a0 + GLA_GATE_RANK], jnp.zeros((layers, d, LANES - GLA_GATE_RANK), BF16)], axis=2)
    return w_in, w_a, w_in[:, :, a0 + GLA_GATE_RANK:]


def mixer_gla_sg(x, xg, ss, layer, batch, seq, w_in_parts, w_gate_up, b_gate, gla_norm_g,
                 sg_ln_g, sg_ln_b, sg_w_s, sg_b_s, w_out, next_g, cast=()):
    w, w_a, w_us = w_in_parts
    z = proj(xg, w, layer, ss=ss, n_cols=2 * GLA_K_WIDTH + 2 * GLA_V_WIDTH, name="ab_in_proj")
    za = proj(xg, w_a, layer, ss=ss, name="ab_gate_proj")
    zus, *casts = _as_list(proj(xg, w_us, layer, ss=ss, cast=cast, name="ab_sg_proj"))
    y_gla = gla_mixer(z, za, w_gate_up, b_gate, gla_norm_g, batch, seq)
    y_sg = sg_mixer(zus, sg_ln_g, sg_ln_b, sg_w_s, sg_b_s)
    x, xg, ss = proj((y_gla, y_sg), w_out[None], 0, residual=x, next_g=next_g, tn=512, name="ab_out_proj")
    return x, xg, ss, tuple(casts)


def _cross_kernel(xg_ref, ss_ref, x_ref, wq_ref, wo_ref, k_ref, v_ref, g_ref, o_ref, xg_out_ref, ss_out_ref, *, d_in):
    q = _scale_rows(_dot(xg_ref[...], wq_ref[...]), ss_ref, d_in).astype(BF16)
    heads = []
    for hd in range(X_HEADS):
        cols = slice(hd * X_HEAD_DIM, (hd + 1) * X_HEAD_DIM)
        s = _dot_nt(q[:, cols], k_ref[:, cols]) * X_SCALE
        p = jnp.exp(s - jnp.max(s, axis=-1, keepdims=True))
        p = p / jnp.sum(p, axis=-1, keepdims=True)
        heads.append(_dot(p.astype(BF16), v_ref[:, cols]).astype(BF16))
    x_new = x_ref[...] + _dot(jnp.concatenate(heads, axis=1), wo_ref[...])
    o_ref[...] = x_new
    xg_out_ref[...] = (x_new * g_ref[...]).astype(BF16)
    ss_out_ref[...] = _row_sumsq(x_new)


def cross_attention(x, xg, ss, kv, batch, seq, w_q, w_o, layer, next_g, *, tm=256):
    m, d = xg.shape
    mem = kv.shape[0] // batch
    tm = _tile(seq, tm)
    nt = seq // tm
    rows = lambda b, t: (b * nt + t, 0)
    return pl.pallas_call(
        functools.partial(_cross_kernel, d_in=d),
        grid=(batch, nt),
        in_specs=[
            pl.BlockSpec((tm, d), rows),
            pl.BlockSpec((tm, LANES), rows),
            pl.BlockSpec((tm, d), rows),
            pl.BlockSpec((None, d, X_WIDTH), lambda b, t: (layer, 0, 0)),
            pl.BlockSpec((None, X_WIDTH, d), lambda b, t: (layer, 0, 0)),
            pl.BlockSpec((mem, X_WIDTH), lambda b, t: (b, 0)),
            pl.BlockSpec((mem, X_WIDTH), lambda b, t: (b, 1)),
            pl.BlockSpec((1, d), lambda b, t: (0, 0)),
        ],
        out_specs=[pl.BlockSpec((tm, d), rows), pl.BlockSpec((tm, d), rows), pl.BlockSpec((tm, LANES), rows)],
        out_shape=[jax.ShapeDtypeStruct((m, d), F32), jax.ShapeDtypeStruct((m, d), BF16),
                   jax.ShapeDtypeStruct((m, LANES), F32)],
        compiler_params=_params("parallel", "parallel"),
        name="cross_attn",
    )(xg, ss, x, w_q, w_o, kv, kv, next_g.reshape(1, d))


def _rope_tables(seq):
    inv_freq = jnp.power(ROPE_THETA, -jnp.arange(ROPE_HALF, dtype=F32) / ROPE_HALF)
    ang = jnp.arange(seq).astype(F32)[:, None] * inv_freq[None, :]
    cos, sin = jnp.cos(ang), jnp.sin(ang)
    rest = NSA_HEAD_DIM - ROPE_DIM
    c = jnp.concatenate([cos, cos, jnp.ones((seq, rest), F32)], axis=1)
    s_up = jnp.concatenate([jnp.zeros_like(sin), sin, jnp.zeros((seq, rest), F32)], axis=1)
    s_dn = jnp.concatenate([-sin, jnp.zeros_like(sin), jnp.zeros((seq, rest), F32)], axis=1)
    return c, s_up, s_dn


def _rope(x, c, s_up, s_dn, heads):
    width = heads * NSA_HEAD_DIM
    rep = lambda t: jnp.concatenate([t] * heads, axis=1)
    return (x * rep(c) + pltpu.roll(x, ROPE_HALF, 1) * rep(s_up)
            + pltpu.roll(x, width - ROPE_HALF, 1) * rep(s_dn))


def _nsa_kv_kernel(ks_ref, vs_ref, kw_ref, vw_ref, c_ref, su_ref, sd_ref, oks_ref, ovs_ref, okw_ref, ovw_ref):
    c, su, sd = c_ref[...], su_ref[...], sd_ref[...]
    oks_ref[...] = _rope(ks_ref[...], c, su, sd, NSA_KV_GROUPS).astype(BF16)
    okw_ref[...] = _rope(kw_ref[...], c, su, sd, NSA_KV_GROUPS).astype(BF16)
    ones = jnp.ones((ks_ref.shape[0], NSA_HEAD_DIM), BF16)
    for v_ref, o_ref in ((vs_ref, ovs_ref), (vw_ref, ovw_ref)):
        v = v_ref[...].astype(BF16)
        o_ref[...] = jnp.concatenate(
            [piece for g in range(NSA_KV_GROUPS) for piece in (v[:, g * NSA_HEAD_DIM:(g + 1) * NSA_HEAD_DIM], ones)],
            axis=1)


def nsa_kv_prep(z, tables, batch, seq, *, rt=1024):
    m = z.shape[0]
    rt = _tile(seq, rt)
    nt = seq // rt
    cb = NSA_Q_WIDTH // NSA_KV_WIDTH + 2
    zspec = lambda j: pl.BlockSpec((rt, NSA_KV_WIDTH), lambda b, t: (b * nt + t, cb + j))
    tspec = pl.BlockSpec((rt, NSA_HEAD_DIM), lambda b, t: (t, 0))
    kspec = pl.BlockSpec((rt, NSA_KV_WIDTH), lambda b, t: (b * nt + t, 0))
    vspec = pl.BlockSpec((rt, 2 * NSA_KV_WIDTH), lambda b, t: (b * nt + t, 0))
    ksds = jax.ShapeDtypeStruct((m, NSA_KV_WIDTH), BF16)
    vsds = jax.ShapeDtypeStruct((m, 2 * NSA_KV_WIDTH), BF16)
    return pl.pallas_call(
        _nsa_kv_kernel,
        grid=(batch, nt),
        in_specs=[zspec(0), zspec(1), zspec(2), zspec(3), tspec, tspec, tspec],
        out_specs=[kspec, vspec, kspec, vspec],
        out_shape=[ksds, vsds, ksds, vsds],
        compiler_params=_params("parallel", "parallel"),
        name="nsa_kv_prep",
    )(z, z, z, z, *tables)


def _nsa_cmp_kernel(x_ref, pos_ref, w1_ref, b1_ref, w2_ref, b2_ref, o_ref, *, nblk):
    half = NSA_CMP_STRIDE * NSA_HEAD_DIM
    pos_bias = _dot(pos_ref[...], w1_ref[...])[0:1] + b1_ref[...]
    row = lax.broadcasted_iota(jnp.int32, (nblk, NSA_HEAD_DIM), 0)
    xg = jnp.concatenate(
        [x_ref[pl.ds(l, nblk, stride=NSA_CMP_STRIDE), :] for l in range(NSA_CMP_STRIDE)], axis=1).astype(BF16)
    first = _dot(xg, w1_ref[:half])
    second = _dot(xg, w1_ref[half:])
    pre = first + pltpu.roll(second, nblk - 1, 0) + pos_bias
    out = _dot(_silu(pre).astype(BF16), w2_ref[...]) + b2_ref[...]
    o_ref[...] = jnp.where(row < nblk - 1, out, 0.0).astype(o_ref.dtype)


def nsa_compress(z, batch, seq, cmp_pos, cmp_w1, cmp_b1, cmp_w2, cmp_b2):
    nblk = seq // NSA_CMP_STRIDE
    cb = NSA_Q_WIDTH // NSA_HEAD_DIM
    flat = NSA_CMP_LEN * NSA_HEAD_DIM
    pos = jnp.zeros((2, 8, flat), BF16).at[:, 0].set(cmp_pos.reshape(2, flat).astype(BF16))
    return pl.pallas_call(
        functools.partial(_nsa_cmp_kernel, nblk=nblk),
        grid=(2, batch, NSA_KV_GROUPS),
        in_specs=[
            pl.BlockSpec((seq, NSA_HEAD_DIM), lambda s, b, g: (b, cb + s * NSA_KV_GROUPS + g)),
            pl.BlockSpec((None, 8, flat), lambda s, b, g: (s, 0, 0)),
            pl.BlockSpec((None, flat, NSA_HEAD_DIM), lambda s, b, g: (s, 0, 0)),
            pl.BlockSpec((None, 1, NSA_HEAD_DIM), lambda s, b, g: (s, 0, 0)),
            pl.BlockSpec((None, NSA_HEAD_DIM, NSA_HEAD_DIM), lambda s, b, g: (s, 0, 0)),
            pl.BlockSpec((None, 1, NSA_HEAD_DIM), lambda s, b, g: (s, 0, 0)),
        ],
        out_specs=pl.BlockSpec((None, None, None, nblk, NSA_HEAD_DIM), lambda s, b, g: (s, b, g, 0, 0)),
        out_shape=jax.ShapeDtypeStruct((2, batch, NSA_KV_GROUPS, nblk, NSA_HEAD_DIM), BF16),
        compiler_params=_params("parallel", "parallel", "parallel"),
        name="nsa_compress",
    )(z, pos, cmp_w1.astype(BF16), cmp_b1.reshape(2, 1, NSA_HEAD_DIM), cmp_w2.astype(BF16),
      cmp_b2.reshape(2, 1, NSA_HEAD_DIM))


def _row_reduce(s, op, lane_op):
    acc = s[:, :LANES]
    for c in range(1, s.shape[1] // LANES):
        acc = op(acc, s[:, c * LANES:(c + 1) * LANES])
    return lane_op(acc, axis=-1, keepdims=True)


def _nsa_attn_kernel(q_ref, gate_ref, c_ref, su_ref, sd_ref, kc_ref, vc_ref, ks_ref, vs_ref, kw_ref, vw_ref,
                     ovl_ref, blk_ref, spread_ref, o_ref,
                     qp_ref, qr_ref, s_ref, p_ref, bias_ref, m_ref, acc_ref, psum_ref, sel_ref, mix_ref, gx_ref,
                     *, tq, tk, seq, top_n):
    P, HD, SB, W, SL = NSA_HPG, NSA_HEAD_DIM, NSA_SEL_BLOCK, NSA_WINDOW, NSA_SLAB
    R = P * tq
    nsel = seq // SB
    ncmp = seq // NSA_CMP_STRIDE
    t0 = pl.program_id(2) * tq

    qf = q_ref[...]
    q_scale = NSA_SCALE * LOG2_E
    qr = _rope(qf, c_ref[...], su_ref[...], sd_ref[...], P)
    for p in range(P):
        qp_ref[p * tq:(p + 1) * tq] = (qf[:, p * HD:(p + 1) * HD] * q_scale).astype(BF16)
        qr_ref[p * tq:(p + 1) * tq] = (qr[:, p * HD:(p + 1) * HD] * q_scale).astype(BF16)

    def reset_state():
        m_ref[...] = jnp.full_like(m_ref, MASK_NEG)
        acc_ref[...] = jnp.zeros_like(acc_ref)

    def softmax_tile(k_t, v_t, n):
        for g0 in range(0, R, NSA_GROUP_ROWS):
            grp = slice(g0, g0 + NSA_GROUP_ROWS)
            s_grp = _dot_nt(qr_ref[grp], k_t)
            alphas, probs = [], []
            for r0 in range(g0, g0 + NSA_GROUP_ROWS, SL):
                rows = slice(r0, r0 + SL)
                b0 = r0 % tq
                s = s_grp[r0 - g0:r0 - g0 + SL] + bias_ref[b0:b0 + SL, :n]
                m_old = m_ref[rows]
                m_new = jnp.maximum(m_old, _row_reduce(s, jnp.maximum, jnp.max))
                alphas.append(jnp.exp2(m_old - m_new))
                m_ref[rows] = m_new
                probs.append(jnp.exp2(s - m_new).astype(BF16))
            alpha = jnp.concatenate(alphas, axis=0)
            acc_ref[grp] = alpha * acc_ref[grp] + _dot(jnp.concatenate(probs, axis=0), v_t)

    t_c = t0 + lax.broadcasted_iota(jnp.int32, (tq, ncmp), 0)
    end_c = lax.broadcasted_iota(jnp.int32, (tq, ncmp), 1) * NSA_CMP_STRIDE + (NSA_CMP_LEN - 1)
    bias_ref[:, :ncmp] = jnp.where(end_c <= t_c, 0.0, MASK_NEG)
    s_ref[:, :ncmp] = _dot_nt(qp_ref[...], kc_ref[...])
    psum_ref[...] = jnp.zeros_like(psum_ref)
    for j in range(R // SL):
        rows = slice(j * SL, (j + 1) * SL)
        b0 = (j * SL) % tq
        bias = bias_ref[b0:b0 + SL, :ncmp]
        s = s_ref[rows, :ncmp] + bias
        e = jnp.where(bias == 0.0, jnp.exp2(s - _row_reduce(s, jnp.maximum, jnp.max)), 0.0)
        pn = e / jnp.maximum(_row_reduce(e, jnp.add, jnp.sum), 1e-30)
        p_ref[rows, :ncmp] = pn.astype(BF16)
        psum_ref[b0:b0 + SL] += pn
    o_c = _dot(p_ref[:, :ncmp], vc_ref[...])
    gx_ref[...] = _dot(_sigmoid(gate_ref[...]).astype(BF16), spread_ref[...])
    gate = lambda p, j: gx_ref[:, (3 * p + j) * LANES:(3 * p + j + 1) * LANES]
    for p in range(P):
        rows = slice(p * tq, (p + 1) * tq)
        mix_ref[rows] = gate(p, 0) * o_c[rows]

    p_sum = psum_ref[...]
    p_hi = p_sum.astype(BF16)
    p_lo = (p_sum - p_hi.astype(F32)).astype(BF16)
    imp_t = _dot_nt(ovl_ref[...], p_hi) + _dot_nt(ovl_ref[...], p_lo)

    blk = lax.broadcasted_iota(jnp.int32, (nsel, tq), 0)
    cur = (t0 + lax.broadcasted_iota(jnp.int32, (nsel, tq), 1)) // SB
    forced = (blk == 0) | (blk == cur) | (blk == cur - 1)
    score = jnp.where(blk <= cur, imp_t + jnp.where(forced, NSA_FORCE_BONUS, 0.0), -jnp.inf)
    rank = jnp.zeros((nsel, tq), jnp.int32)
    for sp in range(nsel):
        other = jnp.broadcast_to(score[sp:sp + 1, :], (nsel, tq))
        rank = rank + jnp.where(blk > sp, jnp.where(other >= score, 1, 0), jnp.where(other > score, 1, 0))
    sel_ref[...] = jnp.where((rank < top_n) & (blk <= cur), 1.0, 0.0).T.astype(BF16)

    wlen = W + tq
    base = pl.multiple_of(jnp.clip(t0 - W, 0, seq - wlen), tq)
    delta = (t0 + lax.broadcasted_iota(jnp.int32, (tq, wlen), 0)) - (base + lax.broadcasted_iota(jnp.int32, (tq, wlen), 1))
    bias_ref[:, :wlen] = jnp.where((delta >= 0) & (delta < W), 0.0, MASK_NEG)
    reset_state()
    softmax_tile(kw_ref[pl.ds(base, wlen), :], vw_ref[pl.ds(base, wlen), :], wlen)
    for p in range(P):
        rows = slice(p * tq, (p + 1) * tq)
        o_w = acc_ref[rows, :HD] / acc_ref[rows, HD:]
        mix_ref[rows] += gate(p, 2) * o_w

    t_k = t0 + lax.broadcasted_iota(jnp.int32, (tq, tk), 0)
    lane_k = lax.broadcasted_iota(jnp.int32, (tq, tk), 1)
    reset_state()

    def sel_step(kt, carry):
        k0 = pl.multiple_of(kt * tk, tk)
        picked = _dot(sel_ref[...], blk_ref[kt])
        bias_ref[:, :tk] = jnp.where((picked > 0.5) & (lane_k + k0 <= t_k), 0.0, MASK_NEG)
        softmax_tile(ks_ref[pl.ds(k0, tk), :], vs_ref[pl.ds(k0, tk), :], tk)
        return carry

    lax.fori_loop(0, (t0 + tq + tk - 1) // tk, sel_step, 0)
    for p in range(P):
        rows = slice(p * tq, (p + 1) * tq)
        o_s = acc_ref[rows, :HD] / acc_ref[rows, HD:]
        o_ref[:, p * HD:(p + 1) * HD] = (mix_ref[rows] + gate(p, 1) * o_s).astype(o_ref.dtype)


def nsa_attention(z, zg, kv, cmp, tables, batch, seq, *, tq=256, tk=1024):
    m = z.shape[0]
    ks, vs, kw, vw = kv
    tq = _tile(seq, tq)
    tk = _tile(seq, tk)
    nt = seq // tq
    nsel = seq // NSA_SEL_BLOCK
    ncmp = seq // NSA_CMP_STRIDE
    gw = NSA_HPG * NSA_HEAD_DIM
    c_start = jnp.arange(ncmp) * NSA_CMP_STRIDE
    s_start = jnp.arange(nsel) * NSA_SEL_BLOCK
    ovl = ((c_start[None, :] <= s_start[:, None] + NSA_SEL_BLOCK - 1)
           & (c_start[None, :] + NSA_CMP_LEN - 1 >= s_start[:, None])
           & (jnp.arange(ncmp)[None, :] < ncmp - 1)).astype(BF16)
    blk_of_key = (jnp.arange(seq)[None, :] // NSA_SEL_BLOCK == jnp.arange(nsel)[:, None]).astype(BF16)
    blk_of_key = blk_of_key.reshape(nsel, seq // tk, tk).transpose(1, 0, 2)
    n_gate = 3 * NSA_HPG
    spread = (jnp.arange(LANES)[:, None] == jnp.arange(n_gate * LANES)[None, :] // LANES).astype(BF16)
    kspec =pl.BlockSpec((seq, NSA_HEAD_DIM), lambda b, g, i: (b, g))
    vspec = pl.BlockSpec((seq, 2 * NSA_HEAD_DIM), lambda b, g, i: (b, g))
    cmpspec = lambda s: pl.BlockSpec((None, None, None, ncmp, NSA_HEAD_DIM), lambda b, g, i: (s, b, g, 0, 0))
    tspec = pl.BlockSpec((tq, NSA_HEAD_DIM), lambda b, g, i: (i, 0))
    rows = NSA_HPG * tq
    mask_width = max(tk, NSA_WINDOW + tq, ncmp)
    return pl.pallas_call(
        functools.partial(_nsa_attn_kernel, tq=tq, tk=tk, seq=seq, top_n=min(NSA_TOPK, nsel)),
        grid=(batch, NSA_KV_GROUPS, nt),
        in_specs=[
            pl.BlockSpec((tq, gw), lambda b, g, i: (b * nt + i, g)),
            pl.BlockSpec((tq, LANES), lambda b, g, i: (b * nt + i, g)),
            tspec, tspec, tspec,
            cmpspec(0), cmpspec(1),
            kspec, vspec, kspec, vspec,
            pl.BlockSpec((nsel, ncmp), lambda b, g, i: (0, 0)),
            pl.BlockSpec((seq // tk, nsel, tk), lambda b, g, i: (0, 0, 0)),
            pl.BlockSpec((LANES, n_gate * LANES), lambda b, g, i: (0, 0)),
        ],
        out_specs=pl.BlockSpec((tq, gw), lambda b, g, i: (b * nt + i, g)),
        out_shape=jax.ShapeDtypeStruct((m, NSA_Q_WIDTH), BF16),
        scratch_shapes=[
            pltpu.VMEM((rows, NSA_HEAD_DIM), BF16),
            pltpu.VMEM((rows, NSA_HEAD_DIM), BF16),
            pltpu.VMEM((rows, ncmp), F32),
            pltpu.VMEM((rows, ncmp), BF16),
            pltpu.VMEM((tq, mask_width), F32),
            pltpu.VMEM((rows, 1), F32),
            pltpu.VMEM((rows, 2 * NSA_HEAD_DIM), F32),
            pltpu.VMEM((tq, ncmp), F32),
            pltpu.VMEM((tq, nsel), BF16),
            pltpu.VMEM((rows, NSA_HEAD_DIM), F32),
            pltpu.VMEM((tq, n_gate * LANES), F32),
        ],
        compiler_params=_params("parallel", "parallel", "arbitrary"),
        name="nsa_attn",
    )(z, zg, *tables, cmp, cmp, ks, vs, kw, vw, ovl, blk_of_key, spread)


def split_nsa_w_in(w_in):
    layers, d, _ = w_in.shape
    c0 = NSA_Q_WIDTH + 6 * NSA_KV_WIDTH
    per_group = 3 * NSA_HPG
    cols = []
    for g in range(NSA_KV_GROUPS):
        cols += [w_in[:, :, c0 + g * per_group:c0 + (g + 1) * per_group],
                 jnp.zeros((layers, d, LANES - per_group), BF16)]
    return w_in, jnp.concatenate(cols, axis=2)


def mixer_nsa(x, xg, ss, layer, batch, seq, w_in_parts, cmp_pos, cmp_w1, cmp_b1, cmp_w2, cmp_b2, w_out, next_g,
              cast=()):
    w, w_gates = w_in_parts
    z, *casts = _as_list(proj(xg, w, layer, ss=ss, n_cols=NSA_Q_WIDTH + 6 * NSA_KV_WIDTH, cast=cast,
                              name="nsa_in_proj"))
    zg = proj(xg, w_gates, layer, ss=ss, name="nsa_gate_proj")
    tables = _rope_tables(seq)
    kv = nsa_kv_prep(z, tables, batch, seq)
    cmp = nsa_compress(z, batch, seq, cmp_pos, cmp_w1, cmp_b1, cmp_w2, cmp_b2)
    o = nsa_attention(z, zg, kv, cmp, tables, batch, seq)
    x, xg, ss = proj(o, w_out[None], 0, residual=x, next_g=next_g, tn=512, name="nsa_out_proj")
    return x, xg, ss, tuple(casts)


def kernel(x, mem, mem_norm_g,
           ffn1_norm_g, ffn1_w_gate, ffn1_w_up, ffn1_w_down,
           mix_norm_g,
           ab_w_in, gla_w_gate_up, gla_b_gate, gla_norm_g, sg_ln_g, sg_ln_b, sg_w_s, sg_b_s, ab_w_out,
           nsa_w_in, nsa_cmp_pos, nsa_cmp_w1, nsa_cmp_b1, nsa_cmp_w2, nsa_cmp_b2, nsa_w_out,
           cross_norm_g, cross_w_q, cross_w_kv, cross_w_o,
           ffn2_norm_g, ffn2_w_gate, ffn2_w_up, ffn2_w_down,
           final_norm_g):
    batch, seq, d = x.shape
    depth = ffn1_norm_g.shape[0]
    bf = lambda w: w.astype(BF16)
    ffn1_w32 = (ffn1_w_gate, ffn1_w_up, ffn1_w_down)
    ffn2_w32 = (ffn2_w_gate, ffn2_w_up, ffn2_w_down)
    ffn_w = tuple(bf(w[0]) for w in ffn1_w32)
    mix_w_out = bf(ab_w_out[0])
    ab_w_in_parts, nsa_w_in_parts = split_ab_w_in(bf(ab_w_in)), split_nsa_w_in(bf(nsa_w_in))
    cross_w_q, cross_w_kv, cross_w_o = bf(cross_w_q), bf(cross_w_kv), bf(cross_w_o)

    mem_n = rmsnorm(mem.reshape(-1, d), mem_norm_g)
    x = x.reshape(batch * seq, d)
    xg, ss = norm_prep(x, ffn1_norm_g[0])
    for i in range(depth):
        x, xg, ss, ffn_w = swiglu_half_step(x, xg, ss, ffn_w, mix_norm_g[i], (ffn2_w32, i))
        j = i // 2
        more = i + 1 < depth
        if i % 2 == 0:
            x, xg, ss, nxt = mixer_gla_sg(x, xg, ss, j, batch, seq, ab_w_in_parts, gla_w_gate_up[j], gla_b_gate[j],
                                          gla_norm_g[j], sg_ln_g[j], sg_ln_b[j], sg_w_s[j], sg_b_s[j], mix_w_out,
                                          cross_norm_g[i], cast=((nsa_w_out, j),) if more else ())
        else:
            x, xg, ss, nxt = mixer_nsa(x, xg, ss, j, batch, seq, nsa_w_in_parts, nsa_cmp_pos[j], nsa_cmp_w1[j],
                                       nsa_cmp_b1[j], nsa_cmp_w2[j], nsa_cmp_b2[j], mix_w_out, cross_norm_g[i],
                                       cast=((ab_w_out, j + 1),) if more else ())
        mix_w_out = nxt[0] if more else None
        kv = proj(mem_n, cross_w_kv, i, out_dtype=BF16, name="cross_kv_proj")
        x, xg, ss = cross_attention(x, xg, ss, kv, batch, seq, cross_w_q, cross_w_o, i, ffn2_norm_g[i])
        last = i + 1 == depth
        x, xg, ss, ffn_w = swiglu_half_step(x, xg, ss, ffn_w, None if last else ffn1_norm_g[i + 1],
                                            None if last else (ffn1_w32, i + 1))
    return rmsnorm(x, final_norm_g, out_dtype=F32).reshape(batch, seq, d)
```

```python
import functools

import jax
import jax.numpy as jnp
from jax import lax
from jax.experimental import pallas as pl
from jax.experimental.pallas import tpu as pltpu

F32 = jnp.float32
BF16 = jnp.bfloat16

NORM_EPS = 1e-6
LN_EPS = 1e-5
ROPE_THETA = 500000.0

GLA_HEADS = 8
GLA_DK = 128
GLA_DV = 256
GLA_K_WIDTH = GLA_HEADS * GLA_DK
GLA_V_WIDTH = GLA_HEADS * GLA_DV
GLA_GATE_RANK = 16
GLA_INV_TAU = 1.0 / 16.0
GLA_CHUNK = 64
GLA_SUB = 16
SG_WIDTH = 2048
SG_GROUPS = 4
SG_GROUP_WIDTH = SG_WIDTH // SG_GROUPS
SG_CHUNK = 128

NSA_HEAD_DIM = 128
NSA_KV_GROUPS = 4
NSA_HPG = 8
NSA_Q_WIDTH = NSA_KV_GROUPS * NSA_HPG * NSA_HEAD_DIM
NSA_KV_WIDTH = NSA_KV_GROUPS * NSA_HEAD_DIM
NSA_CMP_STRIDE = 16
NSA_CMP_LEN = 32
NSA_SEL_BLOCK = 64
NSA_TOPK = 16
NSA_WINDOW = 512
NSA_FORCE_BONUS = 1e4
NSA_SCALE = NSA_HEAD_DIM ** -0.5
ROPE_DIM = NSA_HEAD_DIM // 4
ROPE_HALF = ROPE_DIM // 2
NSA_SLAB = 128
NSA_GROUP_ROWS = 256
LOG2_E = 1.4426950408889634

X_HEADS = 4
X_HEAD_DIM = 128
X_WIDTH = X_HEADS * X_HEAD_DIM
X_SCALE = X_HEAD_DIM ** -0.5

LANES = 128
MASK_NEG = -1e30
VMEM_LIMIT_BYTES = 56 * 1024 * 1024


def _params(*sem):
    return pltpu.CompilerParams(dimension_semantics=sem, vmem_limit_bytes=VMEM_LIMIT_BYTES)


def _dot(a, b):
    return jnp.dot(a, b, preferred_element_type=F32)


def _dot_nt(a, b):
    return lax.dot_general(a, b, (((1,), (1,)), ((), ())), preferred_element_type=F32)


def _dot_tn(a, b):
    return lax.dot_general(a, b, (((0,), (0,)), ((), ())), preferred_element_type=F32)


def _sigmoid(x):
    return 1.0 / (1.0 + jnp.exp(-x))


def _silu(x):
    return x * _sigmoid(x)


def _gelu_tanh(x):
    c = 0.7978845608028654
    return x * (0.5 * (1.0 + jnp.tanh(c * (x + 0.044715 * (x * x * x)))))


def _tile(n, pref):
    if n <= pref:
        return n
    t = pref
    while n % t:
        t //= 2
    return t


def _rmsnorm_kernel(x_ref, g_ref, o_ref):
    x = x_ref[...]
    ms = jnp.mean(x * x, axis=-1, keepdims=True)
    o_ref[...] = (x * lax.rsqrt(ms + NORM_EPS) * g_ref[...]).astype(o_ref.dtype)


def rmsnorm(x, g, out_dtype=BF16):
    m, d = x.shape
    tm = _tile(m, 256)
    return pl.pallas_call(
        _rmsnorm_kernel,
        grid=(m // tm,),
        in_specs=[pl.BlockSpec((tm, d), lambda i: (i, 0)), pl.BlockSpec((1, d), lambda i: (0, 0))],
        out_specs=pl.BlockSpec((tm, d), lambda i: (i, 0)),
        out_shape=jax.ShapeDtypeStruct((m, d), out_dtype),
        compiler_params=_params("parallel"),
        name="rmsnorm",
    )(x, g.reshape(1, d))


def _row_sumsq(x):
    return jnp.broadcast_to(jnp.sum(x * x, axis=-1, keepdims=True), (x.shape[0], LANES))


def _norm_prep_kernel(x_ref, g_ref, xg_ref, ss_ref):
    x = x_ref[...]
    xg_ref[...] = (x * g_ref[...]).astype(BF16)
    ss_ref[...] = _row_sumsq(x)


def norm_prep(x, g):
    m, d = x.shape
    tm = _tile(m, 256)
    return pl.pallas_call(
        _norm_prep_kernel,
        grid=(m // tm,),
        in_specs=[pl.BlockSpec((tm, d), lambda i: (i, 0)), pl.BlockSpec((1, d), lambda i: (0, 0))],
        out_specs=[pl.BlockSpec((tm, d), lambda i: (i, 0)), pl.BlockSpec((tm, LANES), lambda i: (i, 0))],
        out_shape=[jax.ShapeDtypeStruct((m, d), BF16), jax.ShapeDtypeStruct((m, LANES), F32)],
        compiler_params=_params("parallel"),
        name="norm_prep",
    )(x, g.reshape(1, d))


def _scale_rows(acc, ss_ref, d):
    rstd = lax.rsqrt(ss_ref[...] / d + NORM_EPS)
    return acc * jnp.concatenate([rstd] * (acc.shape[1] // LANES), axis=1)


def _proj_kernel(*refs, n_a, normed, residual, scale, next_norm, n_cast, d_in):
    it = iter(refs)
    a_refs = [next(it) for _ in range(n_a)]
    w_refs = [next(it) for _ in range(n_a)]
    ss_ref = next(it) if normed else None
    r_ref = next(it) if residual else None
    g_ref = next(it) if next_norm else None
    cast_in = [next(it) for _ in range(n_cast)]
    o_ref = next(it)
    acc = _dot(a_refs[0][...], w_refs[0][...])
    for a_ref, w_ref in zip(a_refs[1:], w_refs[1:]):
        acc = acc + _dot(a_ref[...], w_ref[...])
    if normed:
        acc = _scale_rows(acc, ss_ref, d_in)
    if residual:
        acc = r_ref[...] + scale * acc
    o_ref[...] = acc.astype(o_ref.dtype)
    if next_norm:
        xg_ref, ss_out_ref = next(it), next(it)
        xg_ref[...] = (acc * g_ref[...]).astype(BF16)
        part = _row_sumsq(acc)

        @pl.when(pl.program_id(1) == 0)
        def _():
            ss_out_ref[...] = part

        @pl.when(pl.program_id(1) != 0)
        def _():
            ss_out_ref[...] += part
    for src_ref in cast_in:
        next(it)[...] = src_ref[...].astype(BF16)


def _as_list(out):
    return list(out) if isinstance(out, (list, tuple)) else [out]


def _cast_specs(w32, layer, n_steps, nj):
    _, k, n = w32.shape
    nblk = n_steps
    while k % nblk or (k // nblk) % 16:
        nblk -= 1
    rb = k // nblk
    blk = lambda i, j: jnp.minimum(i * nj + j, nblk - 1)
    return (pl.BlockSpec((None, rb, n), lambda i, j: (layer, blk(i, j), 0)),
            pl.BlockSpec((rb, n), lambda i, j: (blk(i, j), 0)),
            jax.ShapeDtypeStruct((k, n), BF16))


def proj(a, w, layer, *, ss=None, residual=None, scale=1.0, next_g=None, n_cols=None, out_dtype=F32,
         cast=(), tm=1024, tn=1024, name="proj"):
    a_parts = a if isinstance(a, tuple) else (a,)
    m, kp = a_parts[0].shape
    n = w.shape[2] if n_cols is None else n_cols
    tm = _tile(m, tm)
    tn = _tile(n, tn)
    in_specs = [pl.BlockSpec((tm, kp), lambda i, j: (i, 0)) for _ in a_parts]
    in_specs += [pl.BlockSpec((None, kp, tn), lambda i, j, r=r: (layer, r, j)) for r in range(len(a_parts))]
    args = list(a_parts) + [w] * len(a_parts)
    if ss is not None:
        in_specs.append(pl.BlockSpec((tm, LANES), lambda i, j: (i, 0)))
        args.append(ss)
    if residual is not None:
        in_specs.append(pl.BlockSpec((tm, tn), lambda i, j: (i, j)))
        args.append(residual)
        out_dtype = F32
    out_specs = [pl.BlockSpec((tm, tn), lambda i, j: (i, j))]
    out_shape = [jax.ShapeDtypeStruct((m, n), out_dtype)]
    if next_g is not None:
        in_specs.append(pl.BlockSpec((1, tn), lambda i, j: (0, j)))
        args.append(next_g.reshape(1, n))
        out_specs += [pl.BlockSpec((tm, tn), lambda i, j: (i, j)), pl.BlockSpec((tm, LANES), lambda i, j: (i, 0))]
        out_shape += [jax.ShapeDtypeStruct((m, n), BF16), jax.ShapeDtypeStruct((m, LANES), F32)]
    for w32, cast_layer in cast:
        in_spec, out_spec, sds = _cast_specs(w32, cast_layer, (m // tm) * (n // tn), n // tn)
        in_specs.append(in_spec)
        args.append(w32)
        out_specs.append(out_spec)
        out_shape.append(sds)
    out = pl.pallas_call(
        functools.partial(_proj_kernel, n_a=len(a_parts), normed=ss is not None, residual=residual is not None,
                          scale=scale, next_norm=next_g is not None, n_cast=len(cast), d_in=kp * len(a_parts)),
        grid=(m // tm, n // tn),
        in_specs=in_specs,
        out_specs=out_specs,
        out_shape=out_shape,
        compiler_params=_params("parallel", "arbitrary"),
        name=name,
    )(*args)
    return out if next_g is not None or cast else out[0]


def _ffn_up_kernel(*refs, n_cast, d_in):
    h_ref, wg_ref, wu_ref, ss_ref = refs[:4]
    cast_in = refs[4:4 + n_cast]
    o_ref = refs[4 + n_cast]
    cast_out = refs[5 + n_cast:]
    h = h_ref[...]
    g = _scale_rows(_dot(h, wg_ref[...]), ss_ref, d_in)
    u = _scale_rows(_dot(h, wu_ref[...]), ss_ref, d_in)
    o_ref[...] = (_silu(g) * u).astype(o_ref.dtype)
    for src_ref, dst_ref in zip(cast_in, cast_out):
        dst_ref[...] = src_ref[...].astype(BF16)


def ffn_up(xg, ss, wg, wu, *, cast=(), tm=1024, tn=512):
    m, k = xg.shape
    n = wg.shape[1]
    tm = _tile(m, tm)
    tn = _tile(n, tn)
    wspec = pl.BlockSpec((k, tn), lambda i, j: (0, j))
    in_specs = [pl.BlockSpec((tm, k), lambda i, j: (i, 0)), wspec, wspec, pl.BlockSpec((tm, LANES), lambda i, j: (i, 0))]
    out_specs = [pl.BlockSpec((tm, tn), lambda i, j: (i, j))]
    out_shape = [jax.ShapeDtypeStruct((m, n), BF16)]
    args = [xg, wg, wu, ss]
    for w32, cast_layer in cast:
        in_spec, out_spec, sds = _cast_specs(w32, cast_layer, (m // tm) * (n // tn), n // tn)
        in_specs.append(in_spec)
        args.append(w32)
        out_specs.append(out_spec)
        out_shape.append(sds)
    return pl.pallas_call(
        functools.partial(_ffn_up_kernel, n_cast=len(cast), d_in=k),
        grid=(m // tm, n // tn),
        in_specs=in_specs,
        out_specs=out_specs,
        out_shape=out_shape,
        compiler_params=_params("parallel", "arbitrary"),
        name="ffn_up",
    )(*args)


def swiglu_half_step(x, xg, ss, weights, next_g, next_weights32):
    w_gate, w_up, w_down = weights
    cast_up, cast_down = (), ()
    if next_weights32 is not None:
        (g32, u32, d32), nl = next_weights32
        cast_up, cast_down = ((g32, nl), (u32, nl)), ((d32, nl),)
    act, *next_gu = ffn_up(xg, ss, w_gate, w_up, cast=cast_up)
    out = proj(act, w_down[None], 0, residual=x, scale=0.5, next_g=next_g, cast=cast_down, tn=512, name="ffn_down")
    out = list(out) if isinstance(out, (list, tuple)) else [out]
    x_new = out.pop(0)
    xg_new, ss_new = (out.pop(0), out.pop(0)) if next_g is not None else (None, None)
    return x_new, xg_new, ss_new, tuple(next_gu) + tuple(out)


def _gla_kernel(q_ref, k_ref, v_ref, r_ref, a_ref, wgu_ref, bg_ref, ng_ref, o_ref, st_ref, *, rt):
    C, SB = GLA_CHUNK, GLA_SUB

    @pl.when(pl.program_id(2) == 0)
    def _():
        st_ref[...] = jnp.zeros_like(st_ref)

    zg = _dot(a_ref[...].astype(BF16), wgu_ref[...]) + bg_ref[...]
    lf = (jnp.minimum(zg, 0.0) - jnp.log(1.0 + jnp.exp(-jnp.abs(zg)))) * (GLA_INV_TAU * LOG2_E)

    row = lax.broadcasted_iota(jnp.int32, (rt, GLA_DK), 0)
    rc = row & (C - 1)
    b = lf
    for s in (1, 2, 4, 8, 16, 32):
        b = b + jnp.where(rc >= s, pltpu.roll(b, s, 0), 0.0)

    q = q_ref[...] * (GLA_DK ** -0.5)
    k = k_ref[...]
    v = v_ref[...].astype(BF16)

    nb = rt // SB
    b3 = b.reshape(nb, SB, GLA_DK)
    q3 = q.reshape(nb, SB, GLA_DK)
    k3 = k.reshape(nb, SB, GLA_DK)
    rowc = lax.broadcasted_iota(jnp.int32, (rt, C), 0)
    lane = lax.broadcasted_iota(jnp.int32, (rt, C), 1)
    sub_row = rowc & (SB - 1)
    col_in_sub = lane - ((rowc & (C - 1)) - sub_row)
    causal_col = jnp.where(sub_row >= col_in_sub, col_in_sub, -1)
    a_diag = jnp.zeros((rt, C), F32)
    for j in range(SB):
        bj = jnp.broadcast_to(b3[:, j:j + 1, :], (nb, SB, GLA_DK))
        kj = jnp.broadcast_to(k3[:, j:j + 1, :], (nb, SB, GLA_DK))
        pj = (q3 * kj * jnp.exp2(b3 - bj)).reshape(rt, GLA_DK)
        rs = jnp.sum(pj, axis=-1, keepdims=True)
        a_diag = jnp.where(causal_col == j, rs, a_diag)

    lane_c = lax.broadcasted_iota(jnp.int32, (SB, C), 1)
    outs = []
    for c in range(rt // C):
        sl = slice(c * C, (c + 1) * C)
        bc, qc, kc, vc = b[sl], q[sl], k[sl], v[sl]
        rows = [jnp.zeros((SB, C), F32)]
        for i in range(1, C // SB):
            b0 = bc[i * SB:i * SB + 1]
            qi = qc[i * SB:(i + 1) * SB] * jnp.exp2(bc[i * SB:(i + 1) * SB] - b0)
            ki = kc * jnp.exp2(jnp.minimum(b0 - bc, 0.0))
            ai = _dot_nt(qi.astype(BF16), ki.astype(BF16))
            rows.append(jnp.where(lane_c < i * SB, ai, 0.0))
        a_c = a_diag[sl] + jnp.concatenate(rows, axis=0)
        o_intra = _dot(a_c.astype(BF16), vc)
        st = st_ref[...]
        o_inter = _dot_nt((qc * jnp.exp2(bc)).astype(BF16), st.astype(BF16))
        bl = bc[C - 1:C]
        kd = kc * jnp.exp2(bl - bc)
        st_ref[...] = st * jnp.exp2(bl) + _dot_tn(vc, kd.astype(BF16))
        outs.append(o_inter + o_intra)
    o = jnp.concatenate(outs, axis=0)

    ms = jnp.mean(o * o, axis=-1, keepdims=True)
    y = o * lax.rsqrt(ms + NORM_EPS) * ng_ref[...]
    o_ref[...] = (y * _silu(r_ref[...])).astype(o_ref.dtype)


def gla_mixer(z, za, w_gate_up, b_gate, norm_g, batch, seq, *, rt=512):
    m = z.shape[0]
    rt = _tile(seq, rt)
    nt = seq // rt
    kb = GLA_K_WIDTH // GLA_DK
    vb = 2 * GLA_K_WIDTH // GLA_DV
    rb = vb + GLA_V_WIDTH // GLA_DV
    wgu = jnp.zeros((LANES, GLA_K_WIDTH), BF16).at[:GLA_GATE_RANK].set(w_gate_up.astype(BF16))
    row = lambda b, h, t: b * nt + t
    return pl.pallas_call(
        functools.partial(_gla_kernel, rt=rt),
        grid=(batch, GLA_HEADS, nt),
        in_specs=[
            pl.BlockSpec((rt, GLA_DK), lambda b, h, t: (row(b, h, t), h)),
            pl.BlockSpec((rt, GLA_DK), lambda b, h, t: (row(b, h, t), kb + h)),
            pl.BlockSpec((rt, GLA_DV), lambda b, h, t: (row(b, h, t), vb + h)),
            pl.BlockSpec((rt, GLA_DV), lambda b, h, t: (row(b, h, t), rb + h)),
            pl.BlockSpec((rt, LANES), lambda b, h, t: (row(b, h, t), 0)),
            pl.BlockSpec((LANES, GLA_DK), lambda b, h, t: (0, h)),
            pl.BlockSpec((1, GLA_DK), lambda b, h, t: (0, h)),
            pl.BlockSpec((1, GLA_DV), lambda b, h, t: (0, 0)),
        ],
        out_specs=pl.BlockSpec((rt, GLA_DV), lambda b, h, t: (row(b, h, t), h)),
        out_shape=jax.ShapeDtypeStruct((m, GLA_V_WIDTH), BF16),
        scratch_shapes=[pltpu.VMEM((GLA_DV, GLA_DK), F32)],
        compiler_params=_params("parallel", "parallel", "arbitrary"),
        name="gla",
    )(z, z, z, z, za, wgu, b_gate.reshape(1, GLA_K_WIDTH), norm_g.reshape(1, GLA_DV))


def _sg_kernel(u_ref, s_ref, lng_ref, lnb_ref, ws_ref, bs_ref, o_ref, *, rt):
    s = _gelu_tanh(s_ref[...])
    mu = jnp.mean(s, axis=-1, keepdims=True)
    sc = s - mu
    var = jnp.mean(sc * sc, axis=-1, keepdims=True)
    sn = (sc * lax.rsqrt(var + LN_EPS) * lng_ref[...] + lnb_ref[...]).astype(BF16)
    u = _gelu_tanh(u_ref[...])
    ri = lax.broadcasted_iota(jnp.int32, (SG_CHUNK, SG_CHUNK), 0)
    ci = lax.broadcasted_iota(jnp.int32, (SG_CHUNK, SG_CHUNK), 1)
    for g in range(SG_GROUPS):
        w = jnp.where(ci <= ri, ws_ref[g], 0.0).astype(BF16)
        cols = slice(g * SG_GROUP_WIDTH, (g + 1) * SG_GROUP_WIDTH)
        for c in range(rt // SG_CHUNK):
            rows = slice(c * SG_CHUNK, (c + 1) * SG_CHUNK)
            mixed = _dot(w, sn[rows, cols]) + bs_ref[g]
            o_ref[rows, cols] = (u[rows, cols] * mixed).astype(o_ref.dtype)


def sg_mixer(z, ln_g, ln_b, w_s, b_s, *, rt=512):
    m = z.shape[0]
    rt = _tile(m, rt)
    return pl.pallas_call(
        functools.partial(_sg_kernel, rt=rt),
        grid=(m // rt,),
        in_specs=[
            pl.BlockSpec((rt, SG_WIDTH), lambda i: (i, 0)),
            pl.BlockSpec((rt, SG_WIDTH), lambda i: (i, 1)),
            pl.BlockSpec((1, SG_WIDTH), lambda i: (0, 0)),
            pl.BlockSpec((1, SG_WIDTH), lambda i: (0, 0)),
            pl.BlockSpec((SG_GROUPS, SG_CHUNK, SG_CHUNK), lambda i: (0, 0, 0)),
            pl.BlockSpec((SG_GROUPS, SG_CHUNK, 1), lambda i: (0, 0, 0)),
        ],
        out_specs=pl.BlockSpec((rt, SG_WIDTH), lambda i: (i, 0)),
        out_shape=jax.ShapeDtypeStruct((m, SG_WIDTH), BF16),
        compiler_params=_params("parallel"),
        name="spatial_gate",
    )(z, z, ln_g.reshape(1, SG_WIDTH), ln_b.reshape(1, SG_WIDTH), w_s, b_s.reshape(SG_GROUPS, SG_CHUNK, 1))


def split_ab_w_in(w_in):
    layers, d, _ = w_in.shape
    a0 = 2 * GLA_K_WIDTH + 2 * GLA_V_WIDTH
    w_a = jnp.concatenate(
        [w_in[:, :, a0:a0 + GLA_GATE_RANK], jnp.zeros((layers, d, LANES - GLA_GATE_RANK), BF16)], axis=2)
    return w_in, w_a, w_in[:, :, a0 + GLA_GATE_RANK:]


def mixer_gla_sg(x, xg, ss, layer, batch, seq, w_in_parts, w_gate_up, b_gate, gla_norm_g,
                 sg_ln_g, sg_ln_b, sg_w_s, sg_b_s, w_out, next_g, cast=()):
    w, w_a, w_us = w_in_parts
    z = proj(xg, w, layer, ss=ss, n_cols=2 * GLA_K_WIDTH + 2 * GLA_V_WIDTH, name="ab_in_proj")
    za = proj(xg, w_a, layer, ss=ss, name="ab_gate_proj")
    zus, *casts = _as_list(proj(xg, w_us, layer, ss=ss, cast=cast, name="ab_sg_proj"))
    y_gla = gla_mixer(z, za, w_gate_up, b_gate, gla_norm_g, batch, seq)
    y_sg = sg_mixer(zus, sg_ln_g, sg_ln_b, sg_w_s, sg_b_s)
    x, xg, ss = proj((y_gla, y_sg), w_out[None], 0, residual=x, next_g=next_g, tn=512, name="ab_out_proj")
    return x, xg, ss, tuple(casts)


def _cross_kernel(xg_ref, ss_ref, x_ref, wq_ref, wo_ref, k_ref, v_ref, g_ref, o_ref, xg_out_ref, ss_out_ref, *, d_in):
    q = _scale_rows(_dot(xg_ref[...], wq_ref[...]), ss_ref, d_in).astype(BF16)
    heads = []
    for hd in range(X_HEADS):
        cols = slice(hd * X_HEAD_DIM, (hd + 1) * X_HEAD_DIM)
        s = _dot_nt(q[:, cols], k_ref[:, cols]) * X_SCALE
        p = jnp.exp(s - jnp.max(s, axis=-1, keepdims=True))
        p = p / jnp.sum(p, axis=-1, keepdims=True)
        heads.append(_dot(p.astype(BF16), v_ref[:, cols]).astype(BF16))
    x_new = x_ref[...] + _dot(jnp.concatenate(heads, axis=1), wo_ref[...])
    o_ref[...] = x_new
    xg_out_ref[...] = (x_new * g_ref[...]).astype(BF16)
    ss_out_ref[...] = _row_sumsq(x_new)


def cross_attention(x, xg, ss, kv, batch, seq, w_q, w_o, layer, next_g, *, tm=256):
    m, d = xg.shape
    mem = kv.shape[0] // batch
    tm = _tile(seq, tm)
    nt = seq // tm
    rows = lambda b, t: (b * nt + t, 0)
    return pl.pallas_call(
        functools.partial(_cross_kernel, d_in=d),
        grid=(batch, nt),
        in_specs=[
            pl.BlockSpec((tm, d), rows),
            pl.BlockSpec((tm, LANES), rows),
            pl.BlockSpec((tm, d), rows),
            pl.BlockSpec((None, d, X_WIDTH), lambda b, t: (layer, 0, 0)),
            pl.BlockSpec((None, X_WIDTH, d), lambda b, t: (layer, 0, 0)),
            pl.BlockSpec((mem, X_WIDTH), lambda b, t: (b, 0)),
            pl.BlockSpec((mem, X_WIDTH), lambda b, t: (b, 1)),
            pl.BlockSpec((1, d), lambda b, t: (0, 0)),
        ],
        out_specs=[pl.BlockSpec((tm, d), rows), pl.BlockSpec((tm, d), rows), pl.BlockSpec((tm, LANES), rows)],
        out_shape=[jax.ShapeDtypeStruct((m, d), F32), jax.ShapeDtypeStruct((m, d), BF16),
                   jax.ShapeDtypeStruct((m, LANES), F32)],
        compiler_params=_params("parallel", "parallel"),
        name="cross_attn",
    )(xg, ss, x, w_q, w_o, kv, kv, next_g.reshape(1, d))


def _rope_tables(seq):
    inv_freq = jnp.power(ROPE_THETA, -jnp.arange(ROPE_HALF, dtype=F32) / ROPE_HALF)
    ang = jnp.arange(seq).astype(F32)[:, None] * inv_freq[None, :]
    cos, sin = jnp.cos(ang), jnp.sin(ang)
    rest = NSA_HEAD_DIM - ROPE_DIM
    c = jnp.concatenate([cos, cos, jnp.ones((seq, rest), F32)], axis=1)
    s_up = jnp.concatenate([jnp.zeros_like(sin), sin, jnp.zeros((seq, rest), F32)], axis=1)
    s_dn = jnp.concatenate([-sin, jnp.zeros_like(sin), jnp.zeros((seq, rest), F32)], axis=1)
    return c, s_up, s_dn


def _rope(x, c, s_up, s_dn, heads):
    width = heads * NSA_HEAD_DIM
    rep = lambda t: jnp.concatenate([t] * heads, axis=1)
    return (x * rep(c) + pltpu.roll(x, ROPE_HALF, 1) * rep(s_up)
            + pltpu.roll(x, width - ROPE_HALF, 1) * rep(s_dn))


def _nsa_kv_kernel(ks_ref, vs_ref, kw_ref, vw_ref, c_ref, su_ref, sd_ref, oks_ref, ovs_ref, okw_ref, ovw_ref):
    c, su, sd = c_ref[...], su_ref[...], sd_ref[...]
    oks_ref[...] = _rope(ks_ref[...], c, su, sd, NSA_KV_GROUPS).astype(BF16)
    okw_ref[...] = _rope(kw_ref[...], c, su, sd, NSA_KV_GROUPS).astype(BF16)
    ones = jnp.ones((ks_ref.shape[0], NSA_HEAD_DIM), BF16)
    for v_ref, o_ref in ((vs_ref, ovs_ref), (vw_ref, ovw_ref)):
        v = v_ref[...].astype(BF16)
        o_ref[...] = jnp.concatenate(
            [piece for g in range(NSA_KV_GROUPS) for piece in (v[:, g * NSA_HEAD_DIM:(g + 1) * NSA_HEAD_DIM], ones)],
            axis=1)


def nsa_kv_prep(z, tables, batch, seq, *, rt=1024):
    m = z.shape[0]
    rt = _tile(seq, rt)
    nt = seq // rt
    cb = NSA_Q_WIDTH // NSA_KV_WIDTH + 2
    zspec = lambda j: pl.BlockSpec((rt, NSA_KV_WIDTH), lambda b, t: (b * nt + t, cb + j))
    tspec = pl.BlockSpec((rt, NSA_HEAD_DIM), lambda b, t: (t, 0))
    kspec = pl.BlockSpec((rt, NSA_KV_WIDTH), lambda b, t: (b * nt + t, 0))
    vspec = pl.BlockSpec((rt, 2 * NSA_KV_WIDTH), lambda b, t: (b * nt + t, 0))
    ksds = jax.ShapeDtypeStruct((m, NSA_KV_WIDTH), BF16)
    vsds = jax.ShapeDtypeStruct((m, 2 * NSA_KV_WIDTH), BF16)
    return pl.pallas_call(
        _nsa_kv_kernel,
        grid=(batch, nt),
        in_specs=[zspec(0), zspec(1), zspec(2), zspec(3), tspec, tspec, tspec],
        out_specs=[kspec, vspec, kspec, vspec],
        out_shape=[ksds, vsds, ksds, vsds],
        compiler_params=_params("parallel", "parallel"),
        name="nsa_kv_prep",
    )(z, z, z, z, *tables)


def _nsa_cmp_kernel(x_ref, pos_ref, w1_ref, b1_ref, w2_ref, b2_ref, o_ref, *, nblk):
    half = NSA_CMP_STRIDE * NSA_HEAD_DIM
    pos_bias = _dot(pos_ref[...], w1_ref[...])[0:1] + b1_ref[...]
    row = lax.broadcasted_iota(jnp.int32, (nblk, NSA_HEAD_DIM), 0)
    xg = jnp.concatenate(
        [x_ref[pl.ds(l, nblk, stride=NSA_CMP_STRIDE), :] for l in range(NSA_CMP_STRIDE)], axis=1).astype(BF16)
    first = _dot(xg, w1_ref[:half])
    second = _dot(xg, w1_ref[half:])
    pre = first + pltpu.roll(second, nblk - 1, 0) + pos_bias
    out = _dot(_silu(pre).astype(BF16), w2_ref[...]) + b2_ref[...]
    o_ref[...] = jnp.where(row < nblk - 1, out, 0.0).astype(o_ref.dtype)


def nsa_compress(z, batch, seq, cmp_pos, cmp_w1, cmp_b1, cmp_w2, cmp_b2):
    nblk = seq // NSA_CMP_STRIDE
    cb = NSA_Q_WIDTH // NSA_HEAD_DIM
    flat = NSA_CMP_LEN * NSA_HEAD_DIM
    pos = jnp.zeros((2, 8, flat), BF16).at[:, 0].set(cmp_pos.reshape(2, flat).astype(BF16))
    return pl.pallas_call(
        functools.partial(_nsa_cmp_kernel, nblk=nblk),
        grid=(2, batch, NSA_KV_GROUPS),
        in_specs=[
            pl.BlockSpec((seq, NSA_HEAD_DIM), lambda s, b, g: (b, cb + s * NSA_KV_GROUPS + g)),
            pl.BlockSpec((None, 8, flat), lambda s, b, g: (s, 0, 0)),
            pl.BlockSpec((None, flat, NSA_HEAD_DIM), lambda s, b, g: (s, 0, 0)),
            pl.BlockSpec((None, 1, NSA_HEAD_DIM), lambda s, b, g: (s, 0, 0)),
            pl.BlockSpec((None, NSA_HEAD_DIM, NSA_HEAD_DIM), lambda s, b, g: (s, 0, 0)),
            pl.BlockSpec((None, 1, NSA_HEAD_DIM), lambda s, b, g: (s, 0, 0)),
        ],
        out_specs=pl.BlockSpec((None, None, None, nblk, NSA_HEAD_DIM), lambda s, b, g: (s, b, g, 0, 0)),
        out_shape=jax.ShapeDtypeStruct((2, batch, NSA_KV_GROUPS, nblk, NSA_HEAD_DIM), BF16),
        compiler_params=_params("parallel", "parallel", "parallel"),
        name="nsa_compress",
    )(z, pos, cmp_w1.astype(BF16), cmp_b1.reshape(2, 1, NSA_HEAD_DIM), cmp_w2.astype(BF16),
      cmp_b2.reshape(2, 1, NSA_HEAD_DIM))


def _row_reduce(s, op, lane_op):
    acc = s[:, :LANES]
    for c in range(1, s.shape[1] // LANES):
        acc = op(acc, s[:, c * LANES:(c + 1) * LANES])
    return lane_op(acc, axis=-1, keepdims=True)


def _nsa_attn_kernel(q_ref, gate_ref, c_ref, su_ref, sd_ref, kc_ref, vc_ref, ks_ref, vs_ref, kw_ref, vw_ref,
                     ovl_ref, blk_ref, spread_ref, o_ref,
                     qp_ref, qr_ref, s_ref, p_ref, bias_ref, m_ref, acc_ref, psum_ref, sel_ref, mix_ref, gx_ref,
                     *, tq, tk, seq, top_n):
    P, HD, SB, W, SL = NSA_HPG, NSA_HEAD_DIM, NSA_SEL_BLOCK, NSA_WINDOW, NSA_SLAB
    R = P * tq
    nsel = seq // SB
    ncmp = seq // NSA_CMP_STRIDE
    t0 = pl.program_id(2) * tq

    qf = q_ref[...]
    q_scale = NSA_SCALE * LOG2_E
    qr = _rope(qf, c_ref[...], su_ref[...], sd_ref[...], P)
    for p in range(P):
        qp_ref[p * tq:(p + 1) * tq] = (qf[:, p * HD:(p + 1) * HD] * q_scale).astype(BF16)
        qr_ref[p * tq:(p + 1) * tq] = (qr[:, p * HD:(p + 1) * HD] * q_scale).astype(BF16)

    def reset_state():
        m_ref[...] = jnp.full_like(m_ref, MASK_NEG)
        acc_ref[...] = jnp.zeros_like(acc_ref)

    def softmax_tile(k_t, v_t, n):
        for g0 in range(0, R, NSA_GROUP_ROWS):
            grp = slice(g0, g0 + NSA_GROUP_ROWS)
            s_grp = _dot_nt(qr_ref[grp], k_t)
            alphas, probs = [], []
            for r0 in range(g0, g0 + NSA_GROUP_ROWS, SL):
                rows = slice(r0, r0 + SL)
                b0 = r0 % tq
                s = s_grp[r0 - g0:r0 - g0 + SL] + bias_ref[b0:b0 + SL, :n]
                m_old = m_ref[rows]
                m_new = jnp.maximum(m_old, _row_reduce(s, jnp.maximum, jnp.max))
                alphas.append(jnp.exp2(m_old - m_new))
                m_ref[rows] = m_new
                probs.append(jnp.exp2(s - m_new).astype(BF16))
            alpha = jnp.concatenate(alphas, axis=0)
            acc_ref[grp] = alpha * acc_ref[grp] + _dot(jnp.concatenate(probs, axis=0), v_t)

    t_c = t0 + lax.broadcasted_iota(jnp.int32, (tq, ncmp), 0)
    end_c = lax.broadcasted_iota(jnp.int32, (tq, ncmp), 1) * NSA_CMP_STRIDE + (NSA_CMP_LEN - 1)
    bias_ref[:, :ncmp] = jnp.where(end_c <= t_c, 0.0, MASK_NEG)
    s_ref[:, :ncmp] = _dot_nt(qp_ref[...], kc_ref[...])
    psum_ref[...] = jnp.zeros_like(psum_ref)
    for j in range(R // SL):
        rows = slice(j * SL, (j + 1) * SL)
        b0 = (j * SL) % tq
        bias = bias_ref[b0:b0 + SL, :ncmp]
        s = s_ref[rows, :ncmp] + bias
        e = jnp.where(bias == 0.0, jnp.exp2(s - _row_reduce(s, jnp.maximum, jnp.max)), 0.0)
        pn = e / jnp.maximum(_row_reduce(e, jnp.add, jnp.sum), 1e-30)
        p_ref[rows, :ncmp] = pn.astype(BF16)
        psum_ref[b0:b0 + SL] += pn
    o_c = _dot(p_ref[:, :ncmp], vc_ref[...])
    gx_ref[...] = _dot(_sigmoid(gate_ref[...]).astype(BF16), spread_ref[...])
    gate = lambda p, j: gx_ref[:, (3 * p + j) * LANES:(3 * p + j + 1) * LANES]
    for p in range(P):
        rows = slice(p * tq, (p + 1) * tq)
        mix_ref[rows] = gate(p, 0) * o_c[rows]

    p_sum = psum_ref[...]
    p_hi = p_sum.astype(BF16)
    p_lo = (p_sum - p_hi.astype(F32)).astype(BF16)
    imp_t = _dot_nt(ovl_ref[...], p_hi) + _dot_nt(ovl_ref[...], p_lo)

    blk = lax.broadcasted_iota(jnp.int32, (nsel, tq), 0)
    cur = (t0 + lax.broadcasted_iota(jnp.int32, (nsel, tq), 1)) // SB
    forced = (blk == 0) | (blk == cur) | (blk == cur - 1)
    score = jnp.where(blk <= cur, imp_t + jnp.where(forced, NSA_FORCE_BONUS, 0.0), -jnp.inf)
    rank = jnp.zeros((nsel, tq), jnp.int32)
    for sp in range(nsel):
        other = jnp.broadcast_to(score[sp:sp + 1, :], (nsel, tq))
        rank = rank + jnp.where(blk > sp, jnp.where(other >= score, 1, 0), jnp.where(other > score, 1, 0))
    sel_ref[...] = jnp.where((rank < top_n) & (blk <= cur), 1.0, 0.0).T.astype(BF16)

    wlen = W + tq
    base = pl.multiple_of(jnp.clip(t0 - W, 0, seq - wlen), tq)
    delta = (t0 + lax.broadcasted_iota(jnp.int32, (tq, wlen), 0)) - (base + lax.broadcasted_iota(jnp.int32, (tq, wlen), 1))
    bias_ref[:, :wlen] = jnp.where((delta >= 0) & (delta < W), 0.0, MASK_NEG)
    reset_state()
    softmax_tile(kw_ref[pl.ds(base, wlen), :], vw_ref[pl.ds(base, wlen), :], wlen)
    for p in range(P):
        rows = slice(p * tq, (p + 1) * tq)
        o_w = acc_ref[rows, :HD] / acc_ref[rows, HD:]
        mix_ref[rows] += gate(p, 2) * o_w

    t_k = t0 + lax.broadcasted_iota(jnp.int32, (tq, tk), 0)
    lane_k = lax.broadcasted_iota(jnp.int32, (tq, tk), 1)
    reset_state()

    def sel_step(kt, carry):
        k0 = pl.multiple_of(kt * tk, tk)
        picked = _dot(sel_ref[...], blk_ref[kt])
        bias_ref[:, :tk] = jnp.where((picked > 0.5) & (lane_k + k0 <= t_k), 0.0, MASK_NEG)
        softmax_tile(ks_ref[pl.ds(k0, tk), :], vs_ref[pl.ds(k0, tk), :], tk)
        return carry

    lax.fori_loop(0, (t0 + tq + tk - 1) // tk, sel_step, 0)
    for p in range(P):
        rows = slice(p * tq, (p + 1) * tq)
        o_s = acc_ref[rows, :HD] / acc_ref[rows, HD:]
        o_ref[:, p * HD:(p + 1) * HD] = (mix_ref[rows] + gate(p, 1) * o_s).astype(o_ref.dtype)


def nsa_attention(z, zg, kv, cmp, tables, batch, seq, *, tq=256, tk=1024):
    m = z.shape[0]
    ks, vs, kw, vw = kv
    tq = _tile(seq, tq)
    tk = _tile(seq, tk)
    nt = seq // tq
    nsel = seq // NSA_SEL_BLOCK
    ncmp = seq // NSA_CMP_STRIDE
    gw = NSA_HPG * NSA_HEAD_DIM
    c_start = jnp.arange(ncmp) * NSA_CMP_STRIDE
    s_start = jnp.arange(nsel) * NSA_SEL_BLOCK
    ovl = ((c_start[None, :] <= s_start[:, None] + NSA_SEL_BLOCK - 1)
           & (c_start[None, :] + NSA_CMP_LEN - 1 >= s_start[:, None])
           & (jnp.arange(ncmp)[None, :] < ncmp - 1)).astype(BF16)
    blk_of_key = (jnp.arange(seq)[None, :] // NSA_SEL_BLOCK == jnp.arange(nsel)[:, None]).astype(BF16)
    blk_of_key = blk_of_key.reshape(nsel, seq // tk, tk).transpose(1, 0, 2)
    n_gate = 3 * NSA_HPG
    spread = (jnp.arange(LANES)[:, None] == jnp.arange(n_gate * LANES)[None, :] // LANES).astype(BF16)
    kspec =pl.BlockSpec((seq, NSA_HEAD_DIM), lambda b, g, i: (b, g))
    vspec = pl.BlockSpec((seq, 2 * NSA_HEAD_DIM), lambda b, g, i: (b, g))
    cmpspec = lambda s: pl.BlockSpec((None, None, None, ncmp, NSA_HEAD_DIM), lambda b, g, i: (s, b, g, 0, 0))
    tspec = pl.BlockSpec((tq, NSA_HEAD_DIM), lambda b, g, i: (i, 0))
    rows = NSA_HPG * tq
    mask_width = max(tk, NSA_WINDOW + tq, ncmp)
    return pl.pallas_call(
        functools.partial(_nsa_attn_kernel, tq=tq, tk=tk, seq=seq, top_n=min(NSA_TOPK, nsel)),
        grid=(batch, NSA_KV_GROUPS, nt),
        in_specs=[
            pl.BlockSpec((tq, gw), lambda b, g, i: (b * nt + i, g)),
            pl.BlockSpec((tq, LANES), lambda b, g, i: (b * nt + i, g)),
            tspec, tspec, tspec,
            cmpspec(0), cmpspec(1),
            kspec, vspec, kspec, vspec,
            pl.BlockSpec((nsel, ncmp), lambda b, g, i: (0, 0)),
            pl.BlockSpec((seq // tk, nsel, tk), lambda b, g, i: (0, 0, 0)),
            pl.BlockSpec((LANES, n_gate * LANES), lambda b, g, i: (0, 0)),
        ],
        out_specs=pl.BlockSpec((tq, gw), lambda b, g, i: (b * nt + i, g)),
        out_shape=jax.ShapeDtypeStruct((m, NSA_Q_WIDTH), BF16),
        scratch_shapes=[
            pltpu.VMEM((rows, NSA_HEAD_DIM), BF16),
            pltpu.VMEM((rows, NSA_HEAD_DIM), BF16),
            pltpu.VMEM((rows, ncmp), F32),
            pltpu.VMEM((rows, ncmp), BF16),
            pltpu.VMEM((tq, mask_width), F32),
            pltpu.VMEM((rows, 1), F32),
            pltpu.VMEM((rows, 2 * NSA_HEAD_DIM), F32),
            pltpu.VMEM((tq, ncmp), F32),
            pltpu.VMEM((tq, nsel), BF16),
            pltpu.VMEM((rows, NSA_HEAD_DIM), F32),
            pltpu.VMEM((tq, n_gate * LANES), F32),
        ],
        compiler_params=_params("parallel", "parallel", "arbitrary"),
        name="nsa_attn",
    )(z, zg, *tables, cmp, cmp, ks, vs, kw, vw, ovl, blk_of_key, spread)


def split_nsa_w_in(w_in):
    layers, d, _ = w_in.shape
    c0 = NSA_Q_WIDTH + 6 * NSA_KV_WIDTH
    per_group = 3 * NSA_HPG
    cols = []
    for g in range(NSA_KV_GROUPS):
        cols += [w_in[:, :, c0 + g * per_group:c0 + (g + 1) * per_group],
                 jnp.zeros((layers, d, LANES - per_group), BF16)]
    return w_in, jnp.concatenate(cols, axis=2)


def mixer_nsa(x, xg, ss, layer, batch, seq, w_in_parts, cmp_pos, cmp_w1, cmp_b1, cmp_w2, cmp_b2, w_out, next_g,
              cast=()):
    w, w_gates = w_in_parts
    z, *casts = _as_list(proj(xg, w, layer, ss=ss, n_cols=NSA_Q_WIDTH + 6 * NSA_KV_WIDTH, cast=cast,
                              name="nsa_in_proj"))
    zg = proj(xg, w_gates, layer, ss=ss, name="nsa_gate_proj")
    tables = _rope_tables(seq)
    kv = nsa_kv_prep(z, tables, batch, seq)
    cmp = nsa_compress(z, batch, seq, cmp_pos, cmp_w1, cmp_b1, cmp_w2, cmp_b2)
    o = nsa_attention(z, zg, kv, cmp, tables, batch, seq)
    x, xg, ss = proj(o, w_out[None], 0, residual=x, next_g=next_g, tn=512, name="nsa_out_proj")
    return x, xg, ss, tuple(casts)


def kernel(x, mem, mem_norm_g,
           ffn1_norm_g, ffn1_w_gate, ffn1_w_up, ffn1_w_down,
           mix_norm_g,
           ab_w_in, gla_w_gate_up, gla_b_gate, gla_norm_g, sg_ln_g, sg_ln_b, sg_w_s, sg_b_s, ab_w_out,
           nsa_w_in, nsa_cmp_pos, nsa_cmp_w1, nsa_cmp_b1, nsa_cmp_w2, nsa_cmp_b2, nsa_w_out,
           cross_norm_g, cross_w_q, cross_w_kv, cross_w_o,
           ffn2_norm_g, ffn2_w_gate, ffn2_w_up, ffn2_w_down,
           final_norm_g):
    batch, seq, d = x.shape
    depth = ffn1_norm_g.shape[0]
    bf = lambda w: w.astype(BF16)
    ffn1_w32 = (ffn1_w_gate, ffn1_w_up, ffn1_w_down)
    ffn2_w32 = (ffn2_w_gate, ffn2_w_up, ffn2_w_down)
    ffn_w = tuple(bf(w[0]) for w in ffn1_w32)
    mix_w_out = bf(ab_w_out[0])
    ab_w_in_parts, nsa_w_in_parts = split_ab_w_in(bf(ab_w_in)), split_nsa_w_in(bf(nsa_w_in))
    cross_w_q, cross_w_kv, cross_w_o = bf(cross_w_q), bf(cross_w_kv), bf(cross_w_o)

    mem_n = rmsnorm(mem.reshape(-1, d), mem_norm_g)
    x = x.reshape(batch * seq, d)
    xg, ss = norm_prep(x, ffn1_norm_g[0])
    for i in range(depth):
        x, xg, ss, ffn_w = swiglu_half_step(x, xg, ss, ffn_w, mix_norm_g[i], (ffn2_w32, i))
        j = i // 2
        more = i + 1 < depth
        if i % 2 == 0:
            x, xg, ss, nxt = mixer_gla_sg(x, xg, ss, j, batch, seq, ab_w_in_parts, gla_w_gate_up[j], gla_b_gate[j],
                                          gla_norm_g[j], sg_ln_g[j], sg_ln_b[j], sg_w_s[j], sg_b_s[j], mix_w_out,
                                          cross_norm_g[i], cast=((nsa_w_out, j),) if more else ())
        else:
            x, xg, ss, nxt = mixer_nsa(x, xg, ss, j, batch, seq, nsa_w_in_parts, nsa_cmp_pos[j], nsa_cmp_w1[j],
                                       nsa_cmp_b1[j], nsa_cmp_w2[j], nsa_cmp_b2[j], mix_w_out, cross_norm_g[i],
                                       cast=((ab_w_out, j + 1),) if more else ())
        mix_w_out = nxt[0] if more else None
        kv = proj(mem_n, cross_w_kv, i, out_dtype=BF16, name="cross_kv_proj")
        x, xg, ss = cross_attention(x, xg, ss, kv, batch, seq, cross_w_q, cross_w_o, i, ffn2_norm_g[i])
        last = i + 1 == depth
        x, xg, ss, ffn_w = swiglu_half_step(x, xg, ss, ffn_w, None if last else ffn1_norm_g[i + 1],
                                            None if last else (ffn1_w32, i + 1))
    return rmsnorm(x, final_norm_g, out_dtype=F32).reshape(batch, seq, d)
```

```python
import functools

import jax
import jax.numpy as jnp
from jax import lax
from jax.experimental import pallas as pl
from jax.experimental.pallas import tpu as pltpu

F32 = jnp.float32
BF16 = jnp.bfloat16

NORM_EPS = 1e-6
LN_EPS = 1e-5
ROPE_THETA = 500000.0

GLA_HEADS = 8
GLA_DK = 128
GLA_DV = 256
GLA_K_WIDTH = GLA_HEADS * GLA_DK
GLA_V_WIDTH = GLA_HEADS * GLA_DV
GLA_GATE_RANK = 16
GLA_INV_TAU = 1.0 / 16.0
GLA_CHUNK = 64
GLA_SUB = 16
SG_WIDTH = 2048
SG_GROUPS = 4
SG_GROUP_WIDTH = SG_WIDTH // SG_GROUPS
SG_CHUNK = 128

NSA_HEAD_DIM = 128
NSA_KV_GROUPS = 4
NSA_HPG = 8
NSA_Q_WIDTH = NSA_KV_GROUPS * NSA_HPG * NSA_HEAD_DIM
NSA_KV_WIDTH = NSA_KV_GROUPS * NSA_HEAD_DIM
NSA_CMP_STRIDE = 16
NSA_CMP_LEN = 32
NSA_SEL_BLOCK = 64
NSA_TOPK = 16
NSA_WINDOW = 512
NSA_FORCE_BONUS = 1e4
NSA_SCALE = NSA_HEAD_DIM ** -0.5
ROPE_DIM = NSA_HEAD_DIM // 4
ROPE_HALF = ROPE_DIM // 2
NSA_SLAB = 128
NSA_GROUP_ROWS = 256
LOG2_E = 1.4426950408889634

X_HEADS = 4
X_HEAD_DIM = 128
X_WIDTH = X_HEADS * X_HEAD_DIM
X_SCALE = X_HEAD_DIM ** -0.5

LANES = 128
MASK_NEG = -1e30
VMEM_LIMIT_BYTES = 56 * 1024 * 1024


def _params(*sem):
    return pltpu.CompilerParams(dimension_semantics=sem, vmem_limit_bytes=VMEM_LIMIT_BYTES)


def _dot(a, b):
    return jnp.dot(a, b, preferred_element_type=F32)


def _dot_nt(a, b):
    return lax.dot_general(a, b, (((1,), (1,)), ((), ())), preferred_element_type=F32)


def _dot_tn(a, b):
    return lax.dot_general(a, b, (((0,), (0,)), ((), ())), preferred_element_type=F32)


def _sigmoid(x):
    return 1.0 / (1.0 + jnp.exp(-x))


def _silu(x):
    return x * _sigmoid(x)


def _gelu_tanh(x):
    c = 0.7978845608028654
    return x * (0.5 * (1.0 + jnp.tanh(c * (x + 0.044715 * (x * x * x)))))


def _tile(n, pref):
    if n <= pref:
        return n
    t = pref
    while n % t:
        t //= 2
    return t


def _rmsnorm_kernel(x_ref, g_ref, o_ref):
    x = x_ref[...]
    ms = jnp.mean(x * x, axis=-1, keepdims=True)
    o_ref[...] = (x * lax.rsqrt(ms + NORM_EPS) * g_ref[...]).astype(o_ref.dtype)


def rmsnorm(x, g, out_dtype=BF16):
    m, d = x.shape
    tm = _tile(m, 256)
    return pl.pallas_call(
        _rmsnorm_kernel,
        grid=(m // tm,),
        in_specs=[pl.BlockSpec((tm, d), lambda i: (i, 0)), pl.BlockSpec((1, d), lambda i: (0, 0))],
        out_specs=pl.BlockSpec((tm, d), lambda i: (i, 0)),
        out_shape=jax.ShapeDtypeStruct((m, d), out_dtype),
        compiler_params=_params("parallel"),
        name="rmsnorm",
    )(x, g.reshape(1, d))


def _row_sumsq(x):
    return jnp.broadcast_to(jnp.sum(x * x, axis=-1, keepdims=True), (x.shape[0], LANES))


def _norm_prep_kernel(x_ref, g_ref, xg_ref, ss_ref):
    x = x_ref[...]
    xg_ref[...] = (x * g_ref[...]).astype(BF16)
    ss_ref[...] = _row_sumsq(x)


def norm_prep(x, g):
    m, d = x.shape
    tm = _tile(m, 256)
    return pl.pallas_call(
        _norm_prep_kernel,
        grid=(m // tm,),
        in_specs=[pl.BlockSpec((tm, d), lambda i: (i, 0)), pl.BlockSpec((1, d), lambda i: (0, 0))],
        out_specs=[pl.BlockSpec((tm, d), lambda i: (i, 0)), pl.BlockSpec((tm, LANES), lambda i: (i, 0))],
        out_shape=[jax.ShapeDtypeStruct((m, d), BF16), jax.ShapeDtypeStruct((m, LANES), F32)],
        compiler_params=_params("parallel"),
        name="norm_prep",
    )(x, g.reshape(1, d))


def _scale_rows(acc, ss_ref, d):
    rstd = lax.rsqrt(ss_ref[...] / d + NORM_EPS)
    return acc * jnp.concatenate([rstd] * (acc.shape[1] // LANES), axis=1)


def _proj_kernel(*refs, n_a, normed, residual, scale, next_norm, n_cast, d_in):
    it = iter(refs)
    a_refs = [next(it) for _ in range(n_a)]
    w_refs = [next(it) for _ in range(n_a)]
    ss_ref = next(it) if normed else None
    r_ref = next(it) if residual else None
    g_ref = next(it) if next_norm else None
    cast_in = [next(it) for _ in range(n_cast)]
    o_ref = next(it)
    acc = _dot(a_refs[0][...], w_refs[0][...])
    for a_ref, w_ref in zip(a_refs[1:], w_refs[1:]):
        acc = acc + _dot(a_ref[...], w_ref[...])
    if normed:
        acc = _scale_rows(acc, ss_ref, d_in)
    if residual:
        acc = r_ref[...] + scale * acc
    o_ref[...] = acc.astype(o_ref.dtype)
    if next_norm:
        xg_ref, ss_out_ref = next(it), next(it)
        xg_ref[...] = (acc * g_ref[...]).astype(BF16)
        part = _row_sumsq(acc)

        @pl.when(pl.program_id(1) == 0)
        def _():
            ss_out_ref[...] = part

        @pl.when(pl.program_id(1) != 0)
        def _():
            ss_out_ref[...] += part
    for src_ref in cast_in:
        next(it)[...] = src_ref[...].astype(BF16)


def _as_list(out):
    return list(out) if isinstance(out, (list, tuple)) else [out]


def _cast_specs(w32, layer, n_steps, nj):
    _, k, n = w32.shape
    nblk = n_steps
    while k % nblk or (k // nblk) % 16:
        nblk -= 1
    rb = k // nblk
    blk = lambda i, j: jnp.minimum(i * nj + j, nblk - 1)
    return (pl.BlockSpec((None, rb, n), lambda i, j: (layer, blk(i, j), 0)),
            pl.BlockSpec((rb, n), lambda i, j: (blk(i, j), 0)),
            jax.ShapeDtypeStruct((k, n), BF16))


def proj(a, w, layer, *, ss=None, residual=None, scale=1.0, next_g=None, n_cols=None, out_dtype=F32,
         cast=(), tm=1024, tn=1024, name="proj"):
    a_parts = a if isinstance(a, tuple) else (a,)
    m, kp = a_parts[0].shape
    n = w.shape[2] if n_cols is None else n_cols
    tm = _tile(m, tm)
    tn = _tile(n, tn)
    in_specs = [pl.BlockSpec((tm, kp), lambda i, j: (i, 0)) for _ in a_parts]
    in_specs += [pl.BlockSpec((None, kp, tn), lambda i, j, r=r: (layer, r, j)) for r in range(len(a_parts))]
    args = list(a_parts) + [w] * len(a_parts)
    if ss is not None:
        in_specs.append(pl.BlockSpec((tm, LANES), lambda i, j: (i, 0)))
        args.append(ss)
    if residual is not None:
        in_specs.append(pl.BlockSpec((tm, tn), lambda i, j: (i, j)))
        args.append(residual)
        out_dtype = F32
    out_specs = [pl.BlockSpec((tm, tn), lambda i, j: (i, j))]
    out_shape = [jax.ShapeDtypeStruct((m, n), out_dtype)]
    if next_g is not None:
        in_specs.append(pl.BlockSpec((1, tn), lambda i, j: (0, j)))
        args.append(next_g.reshape(1, n))
        out_specs += [pl.BlockSpec((tm, tn), lambda i, j: (i, j)), pl.BlockSpec((tm, LANES), lambda i, j: (i, 0))]
        out_shape += [jax.ShapeDtypeStruct((m, n), BF16), jax.ShapeDtypeStruct((m, LANES), F32)]
    for w32, cast_layer in cast:
        in_spec, out_spec, sds = _cast_specs(w32, cast_layer, (m // tm) * (n // tn), n // tn)
        in_specs.append(in_spec)
        args.append(w32)
        out_specs.append(out_spec)
        out_shape.append(sds)
    out = pl.pallas_call(
        functools.partial(_proj_kernel, n_a=len(a_parts), normed=ss is not None, residual=residual is not None,
                          scale=scale, next_norm=next_g is not None, n_cast=len(cast), d_in=kp * len(a_parts)),
        grid=(m // tm, n // tn),
        in_specs=in_specs,
        out_specs=out_specs,
        out_shape=out_shape,
        compiler_params=_params("parallel", "arbitrary"),
        name=name,
    )(*args)
    return out if next_g is not None or cast else out[0]


def _ffn_up_kernel(*refs, n_cast, d_in):
    h_ref, wg_ref, wu_ref, ss_ref = refs[:4]
    cast_in = refs[4:4 + n_cast]
    o_ref = refs[4 + n_cast]
    cast_out = refs[5 + n_cast:]
    h = h_ref[...]
    g = _scale_rows(_dot(h, wg_ref[...]), ss_ref, d_in)
    u = _scale_rows(_dot(h, wu_ref[...]), ss_ref, d_in)
    o_ref[...] = (_silu(g) * u).astype(o_ref.dtype)
    for src_ref, dst_ref in zip(cast_in, cast_out):
        dst_ref[...] = src_ref[...].astype(BF16)


def ffn_up(xg, ss, wg, wu, *, cast=(), tm=1024, tn=512):
    m, k = xg.shape
    n = wg.shape[1]
    tm = _tile(m, tm)
    tn = _tile(n, tn)
    wspec = pl.BlockSpec((k, tn), lambda i, j: (0, j))
    in_specs = [pl.BlockSpec((tm, k), lambda i, j: (i, 0)), wspec, wspec, pl.BlockSpec((tm, LANES), lambda i, j: (i, 0))]
    out_specs = [pl.BlockSpec((tm, tn), lambda i, j: (i, j))]
    out_shape = [jax.ShapeDtypeStruct((m, n), BF16)]
    args = [xg, wg, wu, ss]
    for w32, cast_layer in cast:
        in_spec, out_spec, sds = _cast_specs(w32, cast_layer, (m // tm) * (n // tn), n // tn)
        in_specs.append(in_spec)
        args.append(w32)
        out_specs.append(out_spec)
        out_shape.append(sds)
    return pl.pallas_call(
        functools.partial(_ffn_up_kernel, n_cast=len(cast), d_in=k),
        grid=(m // tm, n // tn),
        in_specs=in_specs,
        out_specs=out_specs,
        out_shape=out_shape,
        compiler_params=_params("parallel", "arbitrary"),
        name="ffn_up",
    )(*args)


def swiglu_half_step(x, xg, ss, weights, next_g, next_weights32):
    w_gate, w_up, w_down = weights
    cast_up, cast_down = (), ()
    if next_weights32 is not None:
        (g32, u32, d32), nl = next_weights32
        cast_up, cast_down = ((g32, nl), (u32, nl)), ((d32, nl),)
    act, *next_gu = ffn_up(xg, ss, w_gate, w_up, cast=cast_up)
    out = proj(act, w_down[None], 0, residual=x, scale=0.5, next_g=next_g, cast=cast_down, tn=512, name="ffn_down")
    out = list(out) if isinstance(out, (list, tuple)) else [out]
    x_new = out.pop(0)
    xg_new, ss_new = (out.pop(0), out.pop(0)) if next_g is not None else (None, None)
    return x_new, xg_new, ss_new, tuple(next_gu) + tuple(out)


def _gla_kernel(q_ref, k_ref, v_ref, r_ref, a_ref, wgu_ref, bg_ref, ng_ref, o_ref, st_ref, *, rt):
    C, SB = GLA_CHUNK, GLA_SUB

    @pl.when(pl.program_id(2) == 0)
    def _():
        st_ref[...] = jnp.zeros_like(st_ref)

    zg = _dot(a_ref[...].astype(BF16), wgu_ref[...]) + bg_ref[...]
    lf = (jnp.minimum(zg, 0.0) - jnp.log(1.0 + jnp.exp(-jnp.abs(zg)))) * (GLA_INV_TAU * LOG2_E)

    row = lax.broadcasted_iota(jnp.int32, (rt, GLA_DK), 0)
    rc = row & (C - 1)
    b = lf
    for s in (1, 2, 4, 8, 16, 32):
        b = b + jnp.where(rc >= s, pltpu.roll(b, s, 0), 0.0)

    q = q_ref[...] * (GLA_DK ** -0.5)
    k = k_ref[...]
    v = v_ref[...].astype(BF16)

    nb = rt // SB
    b3 = b.reshape(nb, SB, GLA_DK)
    q3 = q.reshape(nb, SB, GLA_DK)
    k3 = k.reshape(nb, SB, GLA_DK)
    rowc = lax.broadcasted_iota(jnp.int32, (rt, C), 0)
    lane = lax.broadcasted_iota(jnp.int32, (rt, C), 1)
    sub_row = rowc & (SB - 1)
    col_in_sub = lane - ((rowc & (C - 1)) - sub_row)
    causal_col = jnp.where(sub_row >= col_in_sub, col_in_sub, -1)
    a_diag = jnp.zeros((rt, C), F32)
    for j in range(SB):
        bj = jnp.broadcast_to(b3[:, j:j + 1, :], (nb, SB, GLA_DK))
        kj = jnp.broadcast_to(k3[:, j:j + 1, :], (nb, SB, GLA_DK))
        pj = (q3 * kj * jnp.exp2(b3 - bj)).reshape(rt, GLA_DK)
        rs = jnp.sum(pj, axis=-1, keepdims=True)
        a_diag = jnp.where(causal_col == j, rs, a_diag)

    lane_c = lax.broadcasted_iota(jnp.int32, (SB, C), 1)
    outs = []
    for c in range(rt // C):
        sl = slice(c * C, (c + 1) * C)
        bc, qc, kc, vc = b[sl], q[sl], k[sl], v[sl]
        rows = [jnp.zeros((SB, C), F32)]
        for i in range(1, C // SB):
            b0 = bc[i * SB:i * SB + 1]
            qi = qc[i * SB:(i + 1) * SB] * jnp.exp2(bc[i * SB:(i + 1) * SB] - b0)
            ki = kc * jnp.exp2(jnp.minimum(b0 - bc, 0.0))
            ai = _dot_nt(qi.astype(BF16), ki.astype(BF16))
            rows.append(jnp.where(lane_c < i * SB, ai, 0.0))
        a_c = a_diag[sl] + jnp.concatenate(rows, axis=0)
        o_intra = _dot(a_c.astype(BF16), vc)
        st = st_ref[...]
        o_inter = _dot_nt((qc * jnp.exp2(bc)).astype(BF16), st.astype(BF16))
        bl = bc[C - 1:C]
        kd = kc * jnp.exp2(bl - bc)
        st_ref[...] = st * jnp.exp2(bl) + _dot_tn(vc, kd.astype(BF16))
        outs.append(o_inter + o_intra)
    o = jnp.concatenate(outs, axis=0)

    ms = jnp.mean(o * o, axis=-1, keepdims=True)
    y = o * lax.rsqrt(ms + NORM_EPS) * ng_ref[...]
    o_ref[...] = (y * _silu(r_ref[...])).astype(o_ref.dtype)


def gla_mixer(z, za, w_gate_up, b_gate, norm_g, batch, seq, *, rt=1024):
    m = z.shape[0]
    rt = _tile(seq, rt)
    nt = seq // rt
    kb = GLA_K_WIDTH // GLA_DK
    vb = 2 * GLA_K_WIDTH // GLA_DV
    rb = vb + GLA_V_WIDTH // GLA_DV
    wgu = jnp.zeros((LANES, GLA_K_WIDTH), BF16).at[:GLA_GATE_RANK].set(w_gate_up.astype(BF16))
    row = lambda b, h, t: b * nt + t
    return pl.pallas_call(
        functools.partial(_gla_kernel, rt=rt),
        grid=(batch, GLA_HEADS, nt),
        in_specs=[
            pl.BlockSpec((rt, GLA_DK), lambda b, h, t: (row(b, h, t), h)),
            pl.BlockSpec((rt, GLA_DK), lambda b, h, t: (row(b, h, t), kb + h)),
            pl.BlockSpec((rt, GLA_DV), lambda b, h, t: (row(b, h, t), vb + h)),
            pl.BlockSpec((rt, GLA_DV), lambda b, h, t: (row(b, h, t), rb + h)),
            pl.BlockSpec((rt, LANES), lambda b, h, t: (row(b, h, t), 0)),
            pl.BlockSpec((LANES, GLA_DK), lambda b, h, t: (0, h)),
            pl.BlockSpec((1, GLA_DK), lambda b, h, t: (0, h)),
            pl.BlockSpec((1, GLA_DV), lambda b, h, t: (0, 0)),
        ],
        out_specs=pl.BlockSpec((rt, GLA_DV), lambda b, h, t: (row(b, h, t), h)),
        out_shape=jax.ShapeDtypeStruct((m, GLA_V_WIDTH), BF16),
        scratch_shapes=[pltpu.VMEM((GLA_DV, GLA_DK), F32)],
        compiler_params=_params("parallel", "parallel", "arbitrary"),
        name="gla",
    )(z, z, z, z, za, wgu, b_gate.reshape(1, GLA_K_WIDTH), norm_g.reshape(1, GLA_DV))


def _sg_kernel(u_ref, s_ref, lng_ref, lnb_ref, ws_ref, bs_ref, o_ref, *, rt):
    s = _gelu_tanh(s_ref[...])
    mu = jnp.mean(s, axis=-1, keepdims=True)
    sc = s - mu
    var = jnp.mean(sc * sc, axis=-1, keepdims=True)
    sn = (sc * lax.rsqrt(var + LN_EPS) * lng_ref[...] + lnb_ref[...]).astype(BF16)
    u = _gelu_tanh(u_ref[...])
    ri = lax.broadcasted_iota(jnp.int32, (SG_CHUNK, SG_CHUNK), 0)
    ci = lax.broadcasted_iota(jnp.int32, (SG_CHUNK, SG_CHUNK), 1)
    for g in range(SG_GROUPS):
        w = jnp.where(ci <= ri, ws_ref[g], 0.0).astype(BF16)
        cols = slice(g * SG_GROUP_WIDTH, (g + 1) * SG_GROUP_WIDTH)
        for c in range(rt // SG_CHUNK):
            rows = slice(c * SG_CHUNK, (c + 1) * SG_CHUNK)
            mixed = _dot(w, sn[rows, cols]) + bs_ref[g]
            o_ref[rows, cols] = (u[rows, cols] * mixed).astype(o_ref.dtype)


def sg_mixer(z, ln_g, ln_b, w_s, b_s, *, rt=512):
    m = z.shape[0]
    rt = _tile(m, rt)
    return pl.pallas_call(
        functools.partial(_sg_kernel, rt=rt),
        grid=(m // rt,),
        in_specs=[
            pl.BlockSpec((rt, SG_WIDTH), lambda i: (i, 0)),
            pl.BlockSpec((rt, SG_WIDTH), lambda i: (i, 1)),
            pl.BlockSpec((1, SG_WIDTH), lambda i: (0, 0)),
            pl.BlockSpec((1, SG_WIDTH), lambda i: (0, 0)),
            pl.BlockSpec((SG_GROUPS, SG_CHUNK, SG_CHUNK), lambda i: (0, 0, 0)),
            pl.BlockSpec((SG_GROUPS, SG_CHUNK, 1), lambda i: (0, 0, 0)),
        ],
        out_specs=pl.BlockSpec((rt, SG_WIDTH), lambda i: (i, 0)),
        out_shape=jax.ShapeDtypeStruct((m, SG_WIDTH), BF16),
        compiler_params=_params("parallel"),
        name="spatial_gate",
    )(z, z, ln_g.reshape(1, SG_WIDTH), ln_b.reshape(1, SG_WIDTH), w_s, b_s.reshape(SG_GROUPS, SG_CHUNK, 1))


def split_ab_w_in(w_in):
    layers, d, _ = w_in.shape
    a0 = 2 * GLA_K_WIDTH + 2 * GLA_V_WIDTH
    w_a = jnp.concatenate(
        [w_in[:, :, a0:a0 + GLA_GATE_RANK], jnp.zeros((layers, d, LANES - GLA_GATE_RANK), BF16)], axis=2)
    return w_in, w_a, w_in[:, :, a0 + GLA_GATE_RANK:]


def mixer_gla_sg(x, xg, ss, layer, batch, seq, w_in_parts, w_gate_up, b_gate, gla_norm_g,
                 sg_ln_g, sg_ln_b, sg_w_s, sg_b_s, w_out, next_g, cast=()):
    w, w_a, w_us = w_in_parts
    z = proj(xg, w, layer, ss=ss, n_cols=2 * GLA_K_WIDTH + 2 * GLA_V_WIDTH, name="ab_in_proj")
    za = proj(xg, w_a, layer, ss=ss, name="ab_gate_proj")
    zus, *casts = _as_list(proj(xg, w_us, layer, ss=ss, cast=cast, name="ab_sg_proj"))
    y_gla = gla_mixer(z, za, w_gate_up, b_gate, gla_norm_g, batch, seq)
    y_sg = sg_mixer(zus, sg_ln_g, sg_ln_b, sg_w_s, sg_b_s)
    x, xg, ss = proj((y_gla, y_sg), w_out[None], 0, residual=x, next_g=next_g, tn=512, name="ab_out_proj")
    return x, xg, ss, tuple(casts)


def _cross_kernel(xg_ref, ss_ref, x_ref, wq_ref, wo_ref, k_ref, v_ref, g_ref, o_ref, xg_out_ref, ss_out_ref, *, d_in):
    q = _scale_rows(_dot(xg_ref[...], wq_ref[...]), ss_ref, d_in).astype(BF16)
    heads = []
    for hd in range(X_HEADS):
        cols = slice(hd * X_HEAD_DIM, (hd + 1) * X_HEAD_DIM)
        s = _dot_nt(q[:, cols], k_ref[:, cols]) * X_SCALE
        p = jnp.exp(s - jnp.max(s, axis=-1, keepdims=True))
        p = p / jnp.sum(p, axis=-1, keepdims=True)
        heads.append(_dot(p.astype(BF16), v_ref[:, cols]).astype(BF16))
    x_new = x_ref[...] + _dot(jnp.concatenate(heads, axis=1), wo_ref[...])
    o_ref[...] = x_new
    xg_out_ref[...] = (x_new * g_ref[...]).astype(BF16)
    ss_out_ref[...] = _row_sumsq(x_new)


def cross_attention(x, xg, ss, kv, batch, seq, w_q, w_o, layer, next_g, *, tm=256):
    m, d = xg.shape
    mem = kv.shape[0] // batch
    tm = _tile(seq, tm)
    nt = seq // tm
    rows = lambda b, t: (b * nt + t, 0)
    return pl.pallas_call(
        functools.partial(_cross_kernel, d_in=d),
        grid=(batch, nt),
        in_specs=[
            pl.BlockSpec((tm, d), rows),
            pl.BlockSpec((tm, LANES), rows),
            pl.BlockSpec((tm, d), rows),
            pl.BlockSpec((None, d, X_WIDTH), lambda b, t: (layer, 0, 0)),
            pl.BlockSpec((None, X_WIDTH, d), lambda b, t: (layer, 0, 0)),
            pl.BlockSpec((mem, X_WIDTH), lambda b, t: (b, 0)),
            pl.BlockSpec((mem, X_WIDTH), lambda b, t: (b, 1)),
            pl.BlockSpec((1, d), lambda b, t: (0, 0)),
        ],
        out_specs=[pl.BlockSpec((tm, d), rows), pl.BlockSpec((tm, d), rows), pl.BlockSpec((tm, LANES), rows)],
        out_shape=[jax.ShapeDtypeStruct((m, d), F32), jax.ShapeDtypeStruct((m, d), BF16),
                   jax.ShapeDtypeStruct((m, LANES), F32)],
        compiler_params=_params("parallel", "parallel"),
        name="cross_attn",
    )(xg, ss, x, w_q, w_o, kv, kv, next_g.reshape(1, d))


def _rope_tables(seq):
    inv_freq = jnp.power(ROPE_THETA, -jnp.arange(ROPE_HALF, dtype=F32) / ROPE_HALF)
    ang = jnp.arange(seq).astype(F32)[:, None] * inv_freq[None, :]
    cos, sin = jnp.cos(ang), jnp.sin(ang)
    rest = NSA_HEAD_DIM - ROPE_DIM
    c = jnp.concatenate([cos, cos, jnp.ones((seq, rest), F32)], axis=1)
    s_up = jnp.concatenate([jnp.zeros_like(sin), sin, jnp.zeros((seq, rest), F32)], axis=1)
    s_dn = jnp.concatenate([-sin, jnp.zeros_like(sin), jnp.zeros((seq, rest), F32)], axis=1)
    return c, s_up, s_dn


def _rope(x, c, s_up, s_dn, heads):
    width = heads * NSA_HEAD_DIM
    rep = lambda t: jnp.concatenate([t] * heads, axis=1)
    return (x * rep(c) + pltpu.roll(x, ROPE_HALF, 1) * rep(s_up)
            + pltpu.roll(x, width - ROPE_HALF, 1) * rep(s_dn))


def _nsa_kv_kernel(ks_ref, vs_ref, kw_ref, vw_ref, c_ref, su_ref, sd_ref, oks_ref, ovs_ref, okw_ref, ovw_ref):
    c, su, sd = c_ref[...], su_ref[...], sd_ref[...]
    oks_ref[...] = _rope(ks_ref[...], c, su, sd, NSA_KV_GROUPS).astype(BF16)
    okw_ref[...] = _rope(kw_ref[...], c, su, sd, NSA_KV_GROUPS).astype(BF16)
    ones = jnp.ones((ks_ref.shape[0], NSA_HEAD_DIM), BF16)
    for v_ref, o_ref in ((vs_ref, ovs_ref), (vw_ref, ovw_ref)):
        v = v_ref[...].astype(BF16)
        o_ref[...] = jnp.concatenate(
            [piece for g in range(NSA_KV_GROUPS) for piece in (v[:, g * NSA_HEAD_DIM:(g + 1) * NSA_HEAD_DIM], ones)],
            axis=1)


def nsa_kv_prep(z, tables, batch, seq, *, rt=1024):
    m = z.shape[0]
    rt = _tile(seq, rt)
    nt = seq // rt
    cb = NSA_Q_WIDTH // NSA_KV_WIDTH + 2
    zspec = lambda j: pl.BlockSpec((rt, NSA_KV_WIDTH), lambda b, t: (b * nt + t, cb + j))
    tspec = pl.BlockSpec((rt, NSA_HEAD_DIM), lambda b, t: (t, 0))
    kspec = pl.BlockSpec((rt, NSA_KV_WIDTH), lambda b, t: (b * nt + t, 0))
    vspec = pl.BlockSpec((rt, 2 * NSA_KV_WIDTH), lambda b, t: (b * nt + t, 0))
    ksds = jax.ShapeDtypeStruct((m, NSA_KV_WIDTH), BF16)
    vsds = jax.ShapeDtypeStruct((m, 2 * NSA_KV_WIDTH), BF16)
    return pl.pallas_call(
        _nsa_kv_kernel,
        grid=(batch, nt),
        in_specs=[zspec(0), zspec(1), zspec(2), zspec(3), tspec, tspec, tspec],
        out_specs=[kspec, vspec, kspec, vspec],
        out_shape=[ksds, vsds, ksds, vsds],
        compiler_params=_params("parallel", "parallel"),
        name="nsa_kv_prep",
    )(z, z, z, z, *tables)


def _nsa_cmp_kernel(x_ref, pos_ref, w1_ref, b1_ref, w2_ref, b2_ref, o_ref, *, nblk):
    half = NSA_CMP_STRIDE * NSA_HEAD_DIM
    pos_bias = _dot(pos_ref[...], w1_ref[...])[0:1] + b1_ref[...]
    row = lax.broadcasted_iota(jnp.int32, (nblk, NSA_HEAD_DIM), 0)
    xg = jnp.concatenate(
        [x_ref[pl.ds(l, nblk, stride=NSA_CMP_STRIDE), :] for l in range(NSA_CMP_STRIDE)], axis=1).astype(BF16)
    first = _dot(xg, w1_ref[:half])
    second = _dot(xg, w1_ref[half:])
    pre = first + pltpu.roll(second, nblk - 1, 0) + pos_bias
    out = _dot(_silu(pre).astype(BF16), w2_ref[...]) + b2_ref[...]
    o_ref[...] = jnp.where(row < nblk - 1, out, 0.0).astype(o_ref.dtype)


def nsa_compress(z, batch, seq, cmp_pos, cmp_w1, cmp_b1, cmp_w2, cmp_b2):
    nblk = seq // NSA_CMP_STRIDE
    cb = NSA_Q_WIDTH // NSA_HEAD_DIM
    flat = NSA_CMP_LEN * NSA_HEAD_DIM
    pos = jnp.zeros((2, 8, flat), BF16).at[:, 0].set(cmp_pos.reshape(2, flat).astype(BF16))
    return pl.pallas_call(
        functools.partial(_nsa_cmp_kernel, nblk=nblk),
        grid=(2, batch, NSA_KV_GROUPS),
        in_specs=[
            pl.BlockSpec((seq, NSA_HEAD_DIM), lambda s, b, g: (b, cb + s * NSA_KV_GROUPS + g)),
            pl.BlockSpec((None, 8, flat), lambda s, b, g: (s, 0, 0)),
            pl.BlockSpec((None, flat, NSA_HEAD_DIM), lambda s, b, g: (s, 0, 0)),
            pl.BlockSpec((None, 1, NSA_HEAD_DIM), lambda s, b, g: (s, 0, 0)),
            pl.BlockSpec((None, NSA_HEAD_DIM, NSA_HEAD_DIM), lambda s, b, g: (s, 0, 0)),
            pl.BlockSpec((None, 1, NSA_HEAD_DIM), lambda s, b, g: (s, 0, 0)),
        ],
        out_specs=pl.BlockSpec((None, None, None, nblk, NSA_HEAD_DIM), lambda s, b, g: (s, b, g, 0, 0)),
        out_shape=jax.ShapeDtypeStruct((2, batch, NSA_KV_GROUPS, nblk, NSA_HEAD_DIM), BF16),
        compiler_params=_params("parallel", "parallel", "parallel"),
        name="nsa_compress",
    )(z, pos, cmp_w1.astype(BF16), cmp_b1.reshape(2, 1, NSA_HEAD_DIM), cmp_w2.astype(BF16),
      cmp_b2.reshape(2, 1, NSA_HEAD_DIM))


def _row_reduce(s, op, lane_op):
    acc = s[:, :LANES]
    for c in range(1, s.shape[1] // LANES):
        acc = op(acc, s[:, c * LANES:(c + 1) * LANES])
    return lane_op(acc, axis=-1, keepdims=True)


def _nsa_attn_kernel(q_ref, gate_ref, c_ref, su_ref, sd_ref, kc_ref, vc_ref, ks_ref, vs_ref, kw_ref, vw_ref,
                     ovl_ref, blk_ref, spread_ref, o_ref,
                     qp_ref, qr_ref, s_ref, p_ref, bias_ref, m_ref, acc_ref, psum_ref, sel_ref, mix_ref, gx_ref,
                     *, tq, tk, seq, top_n):
    P, HD, SB, W, SL = NSA_HPG, NSA_HEAD_DIM, NSA_SEL_BLOCK, NSA_WINDOW, NSA_SLAB
    R = P * tq
    nsel = seq // SB
    ncmp = seq // NSA_CMP_STRIDE
    t0 = pl.program_id(2) * tq

    qf = q_ref[...]
    q_scale = NSA_SCALE * LOG2_E
    qr = _rope(qf, c_ref[...], su_ref[...], sd_ref[...], P)
    for p in range(P):
        qp_ref[p * tq:(p + 1) * tq] = (qf[:, p * HD:(p + 1) * HD] * q_scale).astype(BF16)
        qr_ref[p * tq:(p + 1) * tq] = (qr[:, p * HD:(p + 1) * HD] * q_scale).astype(BF16)

    def reset_state():
        m_ref[...] = jnp.full_like(m_ref, MASK_NEG)
        acc_ref[...] = jnp.zeros_like(acc_ref)

    def softmax_tile(k_t, v_t, n):
        for g0 in range(0, R, NSA_GROUP_ROWS):
            grp = slice(g0, g0 + NSA_GROUP_ROWS)
            s_grp = _dot_nt(qr_ref[grp], k_t)
            alphas, probs = [], []
            for r0 in range(g0, g0 + NSA_GROUP_ROWS, SL):
                rows = slice(r0, r0 + SL)
                b0 = r0 % tq
                s = s_grp[r0 - g0:r0 - g0 + SL] + bias_ref[b0:b0 + SL, :n]
                m_old = m_ref[rows]
                m_new = jnp.maximum(m_old, _row_reduce(s, jnp.maximum, jnp.max))
                alphas.append(jnp.exp2(m_old - m_new))
                m_ref[rows] = m_new
                probs.append(jnp.exp2(s - m_new).astype(BF16))
            alpha = jnp.concatenate(alphas, axis=0)
            acc_ref[grp] = alpha * acc_ref[grp] + _dot(jnp.concatenate(probs, axis=0), v_t)

    t_c = t0 + lax.broadcasted_iota(jnp.int32, (tq, ncmp), 0)
    end_c = lax.broadcasted_iota(jnp.int32, (tq, ncmp), 1) * NSA_CMP_STRIDE + (NSA_CMP_LEN - 1)
    bias_ref[:, :ncmp] = jnp.where(end_c <= t_c, 0.0, MASK_NEG)
    s_ref[:, :ncmp] = _dot_nt(qp_ref[...], kc_ref[...])
    psum_ref[...] = jnp.zeros_like(psum_ref)
    for j in range(R // SL):
        rows = slice(j * SL, (j + 1) * SL)
        b0 = (j * SL) % tq
        bias = bias_ref[b0:b0 + SL, :ncmp]
        s = s_ref[rows, :ncmp] + bias
        e = jnp.where(bias == 0.0, jnp.exp2(s - _row_reduce(s, jnp.maximum, jnp.max)), 0.0)
        pn = e / jnp.maximum(_row_reduce(e, jnp.add, jnp.sum), 1e-30)
        p_ref[rows, :ncmp] = pn.astype(BF16)
        psum_ref[b0:b0 + SL] += pn
    o_c = _dot(p_ref[:, :ncmp], vc_ref[...])
    gx_ref[...] = _dot(_sigmoid(gate_ref[...]).astype(BF16), spread_ref[...])
    gate = lambda p, j: gx_ref[:, (3 * p + j) * LANES:(3 * p + j + 1) * LANES]
    for p in range(P):
        rows = slice(p * tq, (p + 1) * tq)
        mix_ref[rows] = gate(p, 0) * o_c[rows]

    p_sum = psum_ref[...]
    p_hi = p_sum.astype(BF16)
    p_lo = (p_sum - p_hi.astype(F32)).astype(BF16)
    imp_t = _dot_nt(ovl_ref[...], p_hi) + _dot_nt(ovl_ref[...], p_lo)

    blk = lax.broadcasted_iota(jnp.int32, (nsel, tq), 0)
    cur = (t0 + lax.broadcasted_iota(jnp.int32, (nsel, tq), 1)) // SB
    forced = (blk == 0) | (blk == cur) | (blk == cur - 1)
    score = jnp.where(blk <= cur, imp_t + jnp.where(forced, NSA_FORCE_BONUS, 0.0), -jnp.inf)
    rank = jnp.zeros((nsel, tq), jnp.int32)
    for sp in range(nsel):
        other = jnp.broadcast_to(score[sp:sp + 1, :], (nsel, tq))
        rank = rank + jnp.where(blk > sp, jnp.where(other >= score, 1, 0), jnp.where(other > score, 1, 0))
    sel_ref[...] = jnp.where((rank < top_n) & (blk <= cur), 1.0, 0.0).T.astype(BF16)

    wlen = W + tq
    base = pl.multiple_of(jnp.clip(t0 - W, 0, seq - wlen), tq)
    delta = (t0 + lax.broadcasted_iota(jnp.int32, (tq, wlen), 0)) - (base + lax.broadcasted_iota(jnp.int32, (tq, wlen), 1))
    bias_ref[:, :wlen] = jnp.where((delta >= 0) & (delta < W), 0.0, MASK_NEG)
    reset_state()
    softmax_tile(kw_ref[pl.ds(base, wlen), :], vw_ref[pl.ds(base, wlen), :], wlen)
    for p in range(P):
        rows = slice(p * tq, (p + 1) * tq)
        o_w = acc_ref[rows, :HD] / acc_ref[rows, HD:]
        mix_ref[rows] += gate(p, 2) * o_w

    t_k = t0 + lax.broadcasted_iota(jnp.int32, (tq, tk), 0)
    lane_k = lax.broadcasted_iota(jnp.int32, (tq, tk), 1)
    reset_state()

    def sel_step(kt, carry):
        k0 = pl.multiple_of(kt * tk, tk)
        picked = _dot(sel_ref[...], blk_ref[kt])
        bias_ref[:, :tk] = jnp.where((picked > 0.5) & (lane_k + k0 <= t_k), 0.0, MASK_NEG)
        softmax_tile(ks_ref[pl.ds(k0, tk), :], vs_ref[pl.ds(k0, tk), :], tk)
        return carry

    lax.fori_loop(0, (t0 + tq + tk - 1) // tk, sel_step, 0)
    for p in range(P):
        rows = slice(p * tq, (p + 1) * tq)
        o_s = acc_ref[rows, :HD] / acc_ref[rows, HD:]
        o_ref[:, p * HD:(p + 1) * HD] = (mix_ref[rows] + gate(p, 1) * o_s).astype(o_ref.dtype)


def nsa_attention(z, zg, kv, cmp, tables, batch, seq, *, tq=256, tk=1024):
    m = z.shape[0]
    ks, vs, kw, vw = kv
    tq = _tile(seq, tq)
    tk = _tile(seq, tk)
    nt = seq // tq
    nsel = seq // NSA_SEL_BLOCK
    ncmp = seq // NSA_CMP_STRIDE
    gw = NSA_HPG * NSA_HEAD_DIM
    c_start = jnp.arange(ncmp) * NSA_CMP_STRIDE
    s_start = jnp.arange(nsel) * NSA_SEL_BLOCK
    ovl = ((c_start[None, :] <= s_start[:, None] + NSA_SEL_BLOCK - 1)
           & (c_start[None, :] + NSA_CMP_LEN - 1 >= s_start[:, None])
           & (jnp.arange(ncmp)[None, :] < ncmp - 1)).astype(BF16)
    blk_of_key = (jnp.arange(seq)[None, :] // NSA_SEL_BLOCK == jnp.arange(nsel)[:, None]).astype(BF16)
    blk_of_key = blk_of_key.reshape(nsel, seq // tk, tk).transpose(1, 0, 2)
    n_gate = 3 * NSA_HPG
    spread = (jnp.arange(LANES)[:, None] == jnp.arange(n_gate * LANES)[None, :] // LANES).astype(BF16)
    kspec =pl.BlockSpec((seq, NSA_HEAD_DIM), lambda b, g, i: (b, g))
    vspec = pl.BlockSpec((seq, 2 * NSA_HEAD_DIM), lambda b, g, i: (b, g))
    cmpspec = lambda s: pl.BlockSpec((None, None, None, ncmp, NSA_HEAD_DIM), lambda b, g, i: (s, b, g, 0, 0))
    tspec = pl.BlockSpec((tq, NSA_HEAD_DIM), lambda b, g, i: (i, 0))
    rows = NSA_HPG * tq
    mask_width = max(tk, NSA_WINDOW + tq, ncmp)
    return pl.pallas_call(
        functools.partial(_nsa_attn_kernel, tq=tq, tk=tk, seq=seq, top_n=min(NSA_TOPK, nsel)),
        grid=(batch, NSA_KV_GROUPS, nt),
        in_specs=[
            pl.BlockSpec((tq, gw), lambda b, g, i: (b * nt + i, g)),
            pl.BlockSpec((tq, LANES), lambda b, g, i: (b * nt + i, g)),
            tspec, tspec, tspec,
            cmpspec(0), cmpspec(1),
            kspec, vspec, kspec, vspec,
            pl.BlockSpec((nsel, ncmp), lambda b, g, i: (0, 0)),
            pl.BlockSpec((seq // tk, nsel, tk), lambda b, g, i: (0, 0, 0)),
            pl.BlockSpec((LANES, n_gate * LANES), lambda b, g, i: (0, 0)),
        ],
        out_specs=pl.BlockSpec((tq, gw), lambda b, g, i: (b * nt + i, g)),
        out_shape=jax.ShapeDtypeStruct((m, NSA_Q_WIDTH), BF16),
        scratch_shapes=[
            pltpu.VMEM((rows, NSA_HEAD_DIM), BF16),
            pltpu.VMEM((rows, NSA_HEAD_DIM), BF16),
            pltpu.VMEM((rows, ncmp), F32),
            pltpu.VMEM((rows, ncmp), BF16),
            pltpu.VMEM((tq, mask_width), F32),
            pltpu.VMEM((rows, 1), F32),
            pltpu.VMEM((rows, 2 * NSA_HEAD_DIM), F32),
            pltpu.VMEM((tq, ncmp), F32),
            pltpu.VMEM((tq, nsel), BF16),
            pltpu.VMEM((rows, NSA_HEAD_DIM), F32),
            pltpu.VMEM((tq, n_gate * LANES), F32),
        ],
        compiler_params=_params("parallel", "parallel", "arbitrary"),
        name="nsa_attn",
    )(z, zg, *tables, cmp, cmp, ks, vs, kw, vw, ovl, blk_of_key, spread)


def split_nsa_w_in(w_in):
    layers, d, _ = w_in.shape
    c0 = NSA_Q_WIDTH + 6 * NSA_KV_WIDTH
    per_group = 3 * NSA_HPG
    cols = []
    for g in range(NSA_KV_GROUPS):
        cols += [w_in[:, :, c0 + g * per_group:c0 + (g + 1) * per_group],
                 jnp.zeros((layers, d, LANES - per_group), BF16)]
    return w_in, jnp.concatenate(cols, axis=2)


def mixer_nsa(x, xg, ss, layer, batch, seq, w_in_parts, cmp_pos, cmp_w1, cmp_b1, cmp_w2, cmp_b2, w_out, next_g,
              cast=()):
    w, w_gates = w_in_parts
    z, *casts = _as_list(proj(xg, w, layer, ss=ss, n_cols=NSA_Q_WIDTH + 6 * NSA_KV_WIDTH, cast=cast,
                              name="nsa_in_proj"))
    zg = proj(xg, w_gates, layer, ss=ss, name="nsa_gate_proj")
    tables = _rope_tables(seq)
    kv = nsa_kv_prep(z, tables, batch, seq)
    cmp = nsa_compress(z, batch, seq, cmp_pos, cmp_w1, cmp_b1, cmp_w2, cmp_b2)
    o = nsa_attention(z, zg, kv, cmp, tables, batch, seq)
    x, xg, ss = proj(o, w_out[None], 0, residual=x, next_g=next_g, tn=512, name="nsa_out_proj")
    return x, xg, ss, tuple(casts)


def kernel(x, mem, mem_norm_g,
           ffn1_norm_g, ffn1_w_gate, ffn1_w_up, ffn1_w_down,
           mix_norm_g,
           ab_w_in, gla_w_gate_up, gla_b_gate, gla_norm_g, sg_ln_g, sg_ln_b, sg_w_s, sg_b_s, ab_w_out,
           nsa_w_in, nsa_cmp_pos, nsa_cmp_w1, nsa_cmp_b1, nsa_cmp_w2, nsa_cmp_b2, nsa_w_out,
           cross_norm_g, cross_w_q, cross_w_kv, cross_w_o,
           ffn2_norm_g, ffn2_w_gate, ffn2_w_up, ffn2_w_down,
           final_norm_g):
    batch, seq, d = x.shape
    depth = ffn1_norm_g.shape[0]
    bf = lambda w: w.astype(BF16)
    ffn1_w32 = (ffn1_w_gate, ffn1_w_up, ffn1_w_down)
    ffn2_w32 = (ffn2_w_gate, ffn2_w_up, ffn2_w_down)
    ffn_w = tuple(bf(w[0]) for w in ffn1_w32)
    mix_w_out = bf(ab_w_out[0])
    ab_w_in_parts, nsa_w_in_parts = split_ab_w_in(bf(ab_w_in)), split_nsa_w_in(bf(nsa_w_in))
    cross_w_q, cross_w_kv, cross_w_o = bf(cross_w_q), bf(cross_w_kv), bf(cross_w_o)

    mem_n = rmsnorm(mem.reshape(-1, d), mem_norm_g)
    x = x.reshape(batch * seq, d)
    xg, ss = norm_prep(x, ffn1_norm_g[0])
    for i in range(depth):
        x, xg, ss, ffn_w = swiglu_half_step(x, xg, ss, ffn_w, mix_norm_g[i], (ffn2_w32, i))
        j = i // 2
        more = i + 1 < depth
        if i % 2 == 0:
            x, xg, ss, nxt = mixer_gla_sg(x, xg, ss, j, batch, seq, ab_w_in_parts, gla_w_gate_up[j], gla_b_gate[j],
                                          gla_norm_g[j], sg_ln_g[j], sg_ln_b[j], sg_w_s[j], sg_b_s[j], mix_w_out,
                                          cross_norm_g[i], cast=((nsa_w_out, j),) if more else ())
        else:
            x, xg, ss, nxt = mixer_nsa(x, xg, ss, j, batch, seq, nsa_w_in_parts, nsa_cmp_pos[j], nsa_cmp_w1[j],
                                       nsa_cmp_b1[j], nsa_cmp_w2[j], nsa_cmp_b2[j], mix_w_out, cross_norm_g[i],
                                       cast=((ab_w_out, j + 1),) if more else ())
        mix_w_out = nxt[0] if more else None
        kv = proj(mem_n, cross_w_kv, i, out_dtype=BF16, name="cross_kv_proj")
        x, xg, ss = cross_attention(x, xg, ss, kv, batch, seq, cross_w_q, cross_w_o, i, ffn2_norm_g[i])
        last = i + 1 == depth
        x, xg, ss, ffn_w = swiglu_half_step(x, xg, ss, ffn_w, None if last else ffn1_norm_g[i + 1],
                                            None if last else (ffn1_w32, i + 1))
    return rmsnorm(x, final_norm_g, out_dtype=F32).reshape(batch, seq, d)
```
